```python
import jax, jax.numpy as jnp
from jax import lax
import numpy as np

D_MODEL = 2048
BATCH = 4
SEQ = 2048
DEPTH = 1

D_MIX = D_MODEL
MLA_HEADS = 8
QK_NOPE_DIM = 128
QK_ROPE_DIM = 64
V_HEAD_DIM = 128
Q_LORA_RANK = 768
KV_LORA_RANK = 512
ROPE_THETA = 10000.0
MLA_WIDTH = MLA_HEADS * V_HEAD_DIM
CONV_CHANNELS = D_MIX - MLA_WIDTH
CONV_WIDTH = 31
Q_BLOCK = 128
IN_COLS = Q_LORA_RANK + KV_LORA_RANK + QK_ROPE_DIM + 2 * CONV_CHANNELS
N_EXPERT_GROUPS = 8
EXPERTS_PER_GROUP = 8
N_EXPERTS = N_EXPERT_GROUPS * EXPERTS_PER_GROUP
TOP_K = 2
D_EXPERT = 512
MOE_BLOCK = 128
EPS = 1e-6

kernel_name = 'hymba_mla_conformer_hiermoe_encoder'


def rms_norm(x, g):
    xf = x.astype(jnp.float32)
    y = xf * lax.rsqrt(jnp.mean(xf * xf, axis=-1, keepdims=True) + EPS)
    return (y * g.astype(jnp.float32)).astype(x.dtype)


def layer_norm(x, g, b):
    xf = x.astype(jnp.float32)
    mu = jnp.mean(xf, axis=-1, keepdims=True)
    var = jnp.mean(jnp.square(xf - mu), axis=-1, keepdims=True)
    y = (xf - mu) * lax.rsqrt(var + EPS)
    return (y * g.astype(jnp.float32) + b.astype(jnp.float32)).astype(x.dtype)


def rope_tables(seq):
    pos = jnp.arange(seq, dtype=jnp.float32)
    inv_freq = ROPE_THETA ** (-jnp.arange(0, QK_ROPE_DIM, 2, dtype=jnp.float32) / QK_ROPE_DIM)
    ang = pos[:, None] * inv_freq[None, :]
    return jnp.cos(ang), jnp.sin(ang)


def apply_rope(x, cos, sin):
    cos = cos.astype(x.dtype)
    sin = sin.astype(x.dtype)
    x1, x2 = jnp.split(x, 2, axis=-1)
    return jnp.concatenate([x1 * cos - x2 * sin, x2 * cos + x1 * sin], axis=-1)


def bidirectional_mla_attention(q_nope, q_rope, k_nope, k_rope, v):
    b, s, h, _ = q_nope.shape
    nb = s // Q_BLOCK
    scale = (QK_NOPE_DIM + QK_ROPE_DIM) ** -0.5

    def to_blocks(t):
        return t.reshape(b, nb, Q_BLOCK, *t.shape[2:]).swapaxes(0, 1)

    def block(args):
        qn, qr = args
        sc = jnp.einsum('bqhd,bkhd->bhqk', qn, k_nope) + jnp.einsum('bqhr,bkr->bhqk', qr, k_rope)
        p = jax.nn.softmax(sc.astype(jnp.float32) * scale, axis=-1).astype(v.dtype)
        return jnp.einsum('bhqk,bkhd->bqhd', p, v)

    o = lax.map(block, (to_blocks(q_nope), to_blocks(q_rope)))
    return o.swapaxes(0, 1).reshape(b, s, h * V_HEAD_DIM)


def conformer_conv_group(u, b_glu, w_dw, b_dw, ln_g, ln_b):
    u = u + b_glu
    a, gte = jnp.split(u, 2, axis=-1)
    c = a * jax.nn.sigmoid(gte)
    c = lax.conv_general_dilated(
        c, w_dw[:, None, :].astype(c.dtype), window_strides=(1,),
        padding=[(CONV_WIDTH // 2, CONV_WIDTH // 2)],
        dimension_numbers=('NWC', 'WIO', 'NWC'),
        feature_group_count=CONV_CHANNELS) + b_dw
    c = layer_norm(c, ln_g, ln_b)
    return jax.nn.silu(c)


def hierarchical_moe(h, w_group, b_group, w_router, b_router, w_gate, w_up, w_down):
    b, s, d = h.shape
    t = b * s
    xt = h.reshape(t, d)
    g_prob = jax.nn.softmax((xt @ w_group).astype(jnp.float32) + b_group.astype(jnp.float32), axis=-1)
    p_g, g_sel = lax.top_k(g_prob, 1)
    e_logits = ((xt @ w_router).astype(jnp.float32) + b_router.astype(jnp.float32)).reshape(t, N_EXPERT_GROUPS, EXPERTS_PER_GROUP)
    e_logits = jnp.take_along_axis(e_logits, g_sel[:, :, None], axis=1)[:, 0]
    e_prob = jax.nn.softmax(e_logits, axis=-1)
    top_w, top_i = lax.top_k(e_prob, TOP_K)
    top_w = top_w / jnp.sum(top_w, axis=-1, keepdims=True)
    gates = (p_g * top_w).reshape(-1)
    expert_ids = (g_sel * EXPERTS_PER_GROUP + top_i).reshape(-1)
    token_ids = jnp.repeat(jnp.arange(t, dtype=jnp.int32), TOP_K)
    n_assign = t * TOP_K
    n_blocks = (n_assign + N_EXPERTS * (MOE_BLOCK - 1) + MOE_BLOCK - 1) // MOE_BLOCK
    n_rows = n_blocks * MOE_BLOCK
    order = jnp.argsort(expert_ids)
    sorted_e = expert_ids[order]
    counts = jnp.bincount(expert_ids, length=N_EXPERTS)
    padded = (counts + MOE_BLOCK - 1) // MOE_BLOCK * MOE_BLOCK
    pad_end = jnp.cumsum(padded)
    pad_start = pad_end - padded
    start = jnp.cumsum(counts) - counts
    dest = pad_start[sorted_e] + jnp.arange(n_assign, dtype=jnp.int32) - start[sorted_e]
    row_token = jnp.full((n_rows,), t, jnp.int32).at[dest].set(token_ids[order])
    row_gate = jnp.zeros((n_rows,), gates.dtype).at[dest].set(gates[order])
    block_expert = jnp.minimum(
        jnp.searchsorted(pad_end, jnp.arange(n_blocks, dtype=pad_end.dtype) * MOE_BLOCK, side='right'),
        N_EXPERTS - 1)
    x_pad = jnp.concatenate([xt, jnp.zeros((1, d), xt.dtype)], axis=0)

    def expert_block(args):
        tok, gate, e = args
        xb = x_pad[tok]
        hb = jax.nn.silu(xb @ w_gate[e]) * (xb @ w_up[e])
        return (hb @ w_down[e]) * gate[:, None].astype(xb.dtype)

    y = lax.map(expert_block, (row_token.reshape(n_blocks, MOE_BLOCK),
                               row_gate.reshape(n_blocks, MOE_BLOCK), block_expert))
    out = jnp.zeros((t + 1, d), y.dtype).at[row_token].add(y.reshape(n_rows, d))[:t]
    return out.reshape(b, s, d).astype(h.dtype)


def setup_inputs(seed: int = 0) -> dict:
    key = jax.random.key(seed)
    ks = jax.random.split(key, 24)

    def w(k, shape, fan_in):
        return jax.random.normal(k, shape, jnp.float32) * (fan_in ** -0.5)

    def gain(k, shape):
        return 1.0 + 0.02 * jax.random.normal(k, shape, jnp.float32)

    def bias(k, shape, s=0.02):
        return s * jax.random.normal(k, shape, jnp.float32)

    L = DEPTH
    return {
        'x': jax.random.normal(ks[0], (BATCH, SEQ, D_MODEL), jnp.float32),
        'ln1_g': gain(ks[1], (L, D_MODEL)),
        'w_in': w(ks[2], (L, D_MODEL, IN_COLS), D_MODEL),
        'b_glu': bias(ks[3], (L, 2 * CONV_CHANNELS)),
        'q_norm_g': gain(ks[4], (L, Q_LORA_RANK)),
        'w_uq': w(ks[5], (L, Q_LORA_RANK, MLA_HEADS * (QK_NOPE_DIM + QK_ROPE_DIM)), Q_LORA_RANK),
        'kv_norm_g': gain(ks[6], (L, KV_LORA_RANK)),
        'w_ukv': w(ks[7], (L, KV_LORA_RANK, MLA_HEADS * (QK_NOPE_DIM + V_HEAD_DIM)), KV_LORA_RANK),
        'w_dw': w(ks[8], (L, CONV_WIDTH, CONV_CHANNELS), CONV_WIDTH),
        'b_dw': bias(ks[9], (L, CONV_CHANNELS)),
        'conv_ln_g': gain(ks[10], (L, CONV_CHANNELS)),
        'conv_ln_b': bias(ks[11], (L, CONV_CHANNELS)),
        'w_o': w(ks[12], (L, D_MIX, D_MODEL), D_MIX),
        'ln2_g': gain(ks[13], (L, D_MODEL)),
        'w_group': w(ks[14], (L, D_MODEL, N_EXPERT_GROUPS), D_MODEL),
        'b_group': bias(ks[15], (L, N_EXPERT_GROUPS), 0.01),
        'w_router': w(ks[16], (L, D_MODEL, N_EXPERTS), D_MODEL),
        'b_router': bias(ks[17], (L, N_EXPERTS), 0.01),
        'w_gate': w(ks[18], (L, N_EXPERTS, D_MODEL, D_EXPERT), D_MODEL),
        'w_up': w(ks[19], (L, N_EXPERTS, D_MODEL, D_EXPERT), D_MODEL),
        'w_down': w(ks[20], (L, N_EXPERTS, D_EXPERT, D_MODEL), D_EXPERT),
        'final_g': gain(ks[21], (D_MODEL,)),
    }


def reference(x, ln1_g, w_in, b_glu, q_norm_g, w_uq, kv_norm_g, w_ukv, w_dw, b_dw,
              conv_ln_g, conv_ln_b, w_o, ln2_g, w_group, b_group, w_router, b_router,
              w_gate, w_up, w_down, final_g):
    b, s, _ = x.shape
    cos, sin = rope_tables(s)
    split_at = [Q_LORA_RANK, Q_LORA_RANK + KV_LORA_RANK, Q_LORA_RANK + KV_LORA_RANK + QK_ROPE_DIM]
    h = x
    for l in range(DEPTH):
        n = rms_norm(h, ln1_g[l])
        proj = n @ w_in[l]
        q_lat, kv_lat, k_rope, u = jnp.split(proj, split_at, axis=-1)
        q = (rms_norm(q_lat, q_norm_g[l]) @ w_uq[l]).reshape(b, s, MLA_HEADS, QK_NOPE_DIM + QK_ROPE_DIM)
        q_nope, q_rope = jnp.split(q, [QK_NOPE_DIM], axis=-1)
        q_rope = apply_rope(q_rope, cos[:, None, :], sin[:, None, :])
        kv = (rms_norm(kv_lat, kv_norm_g[l]) @ w_ukv[l]).reshape(b, s, MLA_HEADS, QK_NOPE_DIM + V_HEAD_DIM)
        k_nope, v = jnp.split(kv, [QK_NOPE_DIM], axis=-1)
        k_rope = apply_rope(k_rope, cos, sin)
        attn = bidirectional_mla_attention(q_nope, q_rope, k_nope, k_rope, v)
        conv = conformer_conv_group(u, b_glu[l], w_dw[l], b_dw[l], conv_ln_g[l], conv_ln_b[l])
        h = h + jnp.concatenate([attn, conv], axis=-1) @ w_o[l]
        h = h + hierarchical_moe(rms_norm(h, ln2_g[l]), w_group[l], b_group[l], w_router[l],
                                 b_router[l], w_gate[l], w_up[l], w_down[l])
    return rms_norm(h, final_g)
```

```python
import functools

import jax
import jax.numpy as jnp
from jax import lax
from jax.experimental import pallas as pl
from jax.experimental.pallas import tpu as pltpu

F32 = jnp.float32
BF16 = jnp.bfloat16
I32 = jnp.int32

MLA_HEADS = 8
QK_NOPE_DIM = 128
QK_ROPE_DIM = 64
V_HEAD_DIM = 128
ROPE_THETA = 10000.0
N_EXPERT_GROUPS = 8
EXPERTS_PER_GROUP = 8
N_EXPERTS = N_EXPERT_GROUPS * EXPERTS_PER_GROUP
EPS = 1e-6
LOG2E = 1.4426950408889634

LANES = 128
SUBLANES = 8
HEAD_SLOT = 2 * LANES
ROPE_HALF = QK_ROPE_DIM // 2
VMEM_LIMIT = 56 * 1024 * 1024

MOE_ROWS = 256
ROUTE_CHUNK = 256
DMA_UNROLL = 8
DMA_UNROLL_LOG2 = DMA_UNROLL.bit_length() - 1


def _rms(x, g):
    return x * lax.rsqrt(jnp.mean(x * x, axis=-1, keepdims=True) + EPS) * g


def _dot(a, b):
    return jnp.dot(a, b, preferred_element_type=F32)


def _whole(shape, single=False):
    mode = dict(pipeline_mode=pl.Buffered(1)) if single else {}
    return pl.BlockSpec(shape, lambda *_: (0,) * len(shape), **mode)


def _proj_kernel(x_ref, g1_ref, win_ref, bglu_ref, qg_ref, kvg_ref, wuq_ref, wukv_ref,
                 cos_ref, s1_ref, s2_ref, q_out, k_out, v_out, c_out, *, q_rank, kv_rank, conv_ch, q_scale):
    xn = _rms(x_ref[...], g1_ref[...]).astype(BF16)
    o_kv = q_rank
    o_a = o_kv + kv_rank
    o_g = o_a + conv_ch
    o_kr = o_g + conv_ch

    def proj(lo, width):
        return _dot(xn, win_ref[:, lo:lo + width])

    cos = cos_ref[...]
    s1 = s1_ref[...]
    s2 = s2_ref[...]

    def rope(t):
        return t * cos + pltpu.roll(t, ROPE_HALF, 1) * s1 + pltpu.roll(t, LANES - ROPE_HALF, 1) * s2

    a = proj(o_a, conv_ch) + bglu_ref[:, :conv_ch]
    gate = proj(o_g, conv_ch) + bglu_ref[:, conv_ch:]
    c_out[...] = (a * jax.nn.sigmoid(gate)).astype(BF16)

    kr = rope(proj(o_kr, LANES)).astype(BF16)
    qn = _rms(proj(0, q_rank), qg_ref[...]).astype(BF16)
    kvn = _rms(proj(o_kv, kv_rank), kvg_ref[...]).astype(BF16)
    ones_blk = (lax.broadcasted_iota(I32, (x_ref.shape[0], LANES), 1) == 0).astype(BF16)
    for h in range(MLA_HEADS):
        c0 = h * HEAD_SLOT
        qh = _dot(qn, wuq_ref[:, c0:c0 + HEAD_SLOT])
        q_out[:, c0:c0 + LANES] = (qh[:, :LANES] * q_scale).astype(BF16)
        q_out[:, c0 + LANES:c0 + HEAD_SLOT] = (rope(qh[:, LANES:]) * q_scale).astype(BF16)
        kvh = _dot(kvn, wukv_ref[:, c0:c0 + HEAD_SLOT])
        k_out[:, c0:c0 + LANES] = kvh[:, :LANES].astype(BF16)
        k_out[:, c0 + LANES:c0 + HEAD_SLOT] = kr
        v_out[:, c0:c0 + LANES] = kvh[:, LANES:].astype(BF16)
        v_out[:, c0 + LANES:c0 + HEAD_SLOT] = ones_blk


def _proj_call(x2, g1, win, bglu, qg, kvg, wuq, wukv, cos_t, s1_t, s2_t, *, seq, tm):
    t, d = x2.shape
    q_rank, kv_rank = qg.shape[1], kvg.shape[1]
    conv_ch = bglu.shape[1] // 2
    n_pos = seq // tm
    row = lambda i: (i, 0)
    pos = lambda i: (i % n_pos, 0)
    q_scale = float(QK_NOPE_DIM + QK_ROPE_DIM) ** -0.5 * LOG2E
    kern = functools.partial(_proj_kernel, q_rank=q_rank, kv_rank=kv_rank, conv_ch=conv_ch, q_scale=q_scale)
    slot_w = MLA_HEADS * HEAD_SLOT
    return pl.pallas_call(
        kern,
        grid=(t // tm,),
        in_specs=[
            pl.BlockSpec((tm, d), row), _whole(g1.shape), _whole(win.shape, True), _whole(bglu.shape),
            _whole(qg.shape), _whole(kvg.shape), _whole(wuq.shape, True), _whole(wukv.shape, True),
            pl.BlockSpec((tm, LANES), pos), pl.BlockSpec((tm, LANES), pos), pl.BlockSpec((tm, LANES), pos),
        ],
        out_specs=[
            pl.BlockSpec((tm, slot_w), row), pl.BlockSpec((tm, slot_w), row),
            pl.BlockSpec((tm, slot_w), row), pl.BlockSpec((tm, conv_ch), row),
        ],
        out_shape=[
            jax.ShapeDtypeStruct((t, slot_w), BF16), jax.ShapeDtypeStruct((t, slot_w), BF16),
            jax.ShapeDtypeStruct((t, slot_w), BF16), jax.ShapeDtypeStruct((t, conv_ch), BF16),
        ],
        compiler_params=pltpu.CompilerParams(dimension_semantics=("arbitrary",), vmem_limit_bytes=VMEM_LIMIT),
        name="proj",
    )(x2, g1, win, bglu, qg, kvg, wuq, wukv, cos_t, s1_t, s2_t)


def _attn_kernel(q_ref, k_ref, v_ref, o_ref, *, tq):
    k = k_ref[...]
    v = v_ref[...]
    for j in range(q_ref.shape[0] // tq):
        rows = slice(j * tq, (j + 1) * tq)
        s = lax.dot_general(q_ref[rows, :], k, (((1,), (1,)), ((), ())), preferred_element_type=F32)
        m = jnp.max(s, axis=-1, keepdims=True)
        p = jnp.exp2(s - m).astype(BF16)
        o = _dot(p, v)
        o_ref[rows, :] = (o[:, :V_HEAD_DIM] / o[:, V_HEAD_DIM:V_HEAD_DIM + 1]).astype(BF16)


def _attn_call(q, k, v, *, batch, seq, tq):
    t = q.shape[0]
    head = lambda b, h: (b, h)
    return pl.pallas_call(
        functools.partial(_attn_kernel, tq=tq),
        grid=(batch, MLA_HEADS),
        in_specs=[pl.BlockSpec((seq, HEAD_SLOT), head), pl.BlockSpec((seq, HEAD_SLOT), head),
                  pl.BlockSpec((seq, HEAD_SLOT), head)],
        out_specs=pl.BlockSpec((seq, V_HEAD_DIM), head),
        out_shape=jax.ShapeDtypeStruct((t, MLA_HEADS * V_HEAD_DIM), BF16),
        compiler_params=pltpu.CompilerParams(
            dimension_semantics=("arbitrary", "arbitrary"), vmem_limit_bytes=VMEM_LIMIT),
        name="attn",
    )(q, k, v)


CONV_PAD = 16
CONV_ROWS = 128
CONV_LANES = 128


def _conv_kernel(c_ref, w_ref, b_ref, g_ref, bb_ref, o_ref, xp_ref, y_ref, sh_ref, *, width):
    s, ch = c_ref.shape
    half = width // 2
    rows = y_ref.shape[0]
    span = sh_ref.shape[1]
    xp_ref[0:CONV_PAD, :] = jnp.zeros((CONV_PAD, ch), F32)
    xp_ref[CONV_PAD + s:, :] = jnp.zeros((CONV_PAD, ch), F32)
    xp_ref[CONV_PAD:CONV_PAD + s, :] = c_ref[...].astype(F32)

    def step(i, carry):
        base = pl.multiple_of(i * rows, rows)
        for cc in range(ch // CONV_LANES):
            cs = slice(cc * CONV_LANES, (cc + 1) * CONV_LANES)
            xw = xp_ref[pl.ds(base, rows + 2 * CONV_PAD), cs]
            for r in range(1, SUBLANES):
                sh_ref[r - 1] = xw[r:r + span, :]
            acc = jnp.zeros((rows, CONV_LANES), F32)
            for k in range(width):
                off = CONV_PAD - half + k
                r, a0 = off % SUBLANES, off - off % SUBLANES
                if r == 0:
                    tap = xp_ref[pl.ds(base + a0, rows), cs]
                else:
                    tap = sh_ref[r - 1, a0:a0 + rows, :]
                acc = acc + tap * w_ref[k:k + 1, cs]
            y_ref[:, cs] = acc + b_ref[:, cs]
        y = y_ref[...]
        mu = jnp.mean(y, axis=-1, keepdims=True)
        yc = y - mu
        var = jnp.mean(yc * yc, axis=-1, keepdims=True)
        z = yc * lax.rsqrt(var + EPS) * g_ref[...] + bb_ref[...]
        o_ref[pl.ds(base, rows), :] = (z * jax.nn.sigmoid(z)).astype(BF16)
        return carry

    lax.fori_loop(0, s // rows, step, 0)


def _conv_call(c, w_dw, b_dw, ln_g, ln_b, *, batch, seq):
    t, ch = c.shape
    width = w_dw.shape[0]
    assert width // 2 <= CONV_PAD
    rows = min(CONV_ROWS, seq)
    return pl.pallas_call(
        functools.partial(_conv_kernel, width=width),
        grid=(batch,),
        in_specs=[pl.BlockSpec((seq, ch), lambda b: (b, 0)), _whole(w_dw.shape), _whole(b_dw.shape),
                  _whole(ln_g.shape), _whole(ln_b.shape)],
        out_specs=pl.BlockSpec((seq, ch), lambda b: (b, 0)),
        out_shape=jax.ShapeDtypeStruct((t, ch), BF16),
        scratch_shapes=[pltpu.VMEM((seq + 2 * CONV_PAD, ch), F32), pltpu.VMEM((rows, ch), F32),
                        pltpu.VMEM((SUBLANES - 1, rows + 2 * CONV_PAD - SUBLANES, CONV_LANES), F32)],
        compiler_params=pltpu.CompilerParams(dimension_semantics=("arbitrary",), vmem_limit_bytes=VMEM_LIMIT),
        name="conv",
    )(c, w_dw, b_dw, ln_g, ln_b)


def _oproj_kernel(a_ref, c_ref, x_ref, wo_ref, g2_ref, wr_ref, br_ref, h_out, lg_out):
    na = a_ref.shape[1]
    h = x_ref[...] + _dot(a_ref[...], wo_ref[:na, :]) + _dot(c_ref[...], wo_ref[na:, :])
    h_out[...] = h
    hn = _rms(h, g2_ref[...])
    hi = hn.astype(BF16)
    lo = (hn - hi.astype(F32)).astype(BF16)
    r = _dot(hi, wr_ref[...])
    lg_out[...] = r[:, :LANES] + r[:, LANES:] + _dot(lo, wr_ref[:, :LANES]) + br_ref[...]


def _oproj_call(attn, conv, x2, wo, g2, wr, br, *, tm):
    t, d = x2.shape
    row = lambda i: (i, 0)
    return pl.pallas_call(
        _oproj_kernel,
        grid=(t // tm,),
        in_specs=[pl.BlockSpec((tm, attn.shape[1]), row), pl.BlockSpec((tm, conv.shape[1]), row),
                  pl.BlockSpec((tm, d), row), _whole(wo.shape, True), _whole(g2.shape), _whole(wr.shape, True),
                  _whole(br.shape)],
        out_specs=[pl.BlockSpec((tm, d), row), pl.BlockSpec((tm, LANES), row)],
        out_shape=[jax.ShapeDtypeStruct((t, d), F32), jax.ShapeDtypeStruct((t, LANES), F32)],
        compiler_params=pltpu.CompilerParams(dimension_semantics=("arbitrary",), vmem_limit_bytes=VMEM_LIMIT),
        name="oproj",
    )(attn, conv, x2, wo, g2, wr, br)


def _route_rows(lg):
    lane = lax.broadcasted_iota(I32, lg.shape, 1)
    big = jnp.int32(1 << 20)
    neg = jnp.float32(-jnp.inf)
    is_g = (lane >= N_EXPERTS) & (lane < N_EXPERTS + N_EXPERT_GROUPS)
    lgrp = jnp.where(is_g, lg, neg)
    gmax = jnp.max(lgrp, axis=-1, keepdims=True)
    gsel = jnp.min(jnp.where(lgrp == gmax, lane, big), axis=-1, keepdims=True) - N_EXPERTS
    p_g = 1.0 / jnp.sum(jnp.where(is_g, jnp.exp(lg - gmax), 0.0), axis=-1, keepdims=True)
    lo = gsel * EXPERTS_PER_GROUP
    in_grp = (lane >= lo) & (lane < lo + EXPERTS_PER_GROUP)
    le = jnp.where(in_grp, lg, neg)
    m1 = jnp.max(le, axis=-1, keepdims=True)
    i1 = jnp.min(jnp.where(le == m1, lane, big), axis=-1, keepdims=True)
    le2 = jnp.where(lane == i1, neg, le)
    m2 = jnp.max(le2, axis=-1, keepdims=True)
    i2 = jnp.min(jnp.where(le2 == m2, lane, big), axis=-1, keepdims=True)
    r = jnp.exp(m2 - m1)
    w1 = 1.0 / (1.0 + r)
    w2 = r / (1.0 + r)
    oh1 = (lane == i1).astype(F32)
    oh2 = (lane == i2).astype(F32)
    return oh1, oh2, p_g * w1, p_g * w2


def _lane_cumsum(v):
    lane = lax.broadcasted_iota(I32, v.shape, 1)
    sh = 1
    while sh < LANES:
        v = v + jnp.where(lane >= sh, pltpu.roll(v, sh, 1), 0)
        sh *= 2
    return v


def _route_kernel(lg_ref, dest_out, gate_out, meta_out, *, rows_per_block):
    t = lg_ref.shape[0]
    rc = min(ROUTE_CHUNK, t)
    n_chunks = t // rc
    shift = rows_per_block.bit_length() - 1

    def count_step(i, acc):
        base = pl.multiple_of(i * rc, rc)
        oh1, oh2, _, _ = _route_rows(lg_ref[pl.ds(base, rc), :])
        return acc + jnp.sum(oh1 + oh2, axis=0, keepdims=True)

    counts_f = lax.fori_loop(0, n_chunks, count_step, jnp.zeros((1, LANES), F32))
    counts = jnp.broadcast_to(counts_f, (SUBLANES, LANES)).astype(I32)
    padded = ((counts + (rows_per_block - 1)) >> shift) << shift
    pad_end = _lane_cumsum(padded)
    pad_start = pad_end - padded

    r_i = lax.broadcasted_iota(I32, (rc, rc), 0)
    c_i = lax.broadcasted_iota(I32, (rc, rc), 1)
    tri = (r_i > c_i).astype(BF16)
    lane_c = lax.broadcasted_iota(I32, (rc, LANES), 1)

    def dest_step(i, carry):
        base = pl.multiple_of(i * rc, rc)
        oh1, oh2, g1, g2 = _route_rows(lg_ref[pl.ds(base, rc), :])
        oh = oh1 + oh2
        pos = carry + _dot(tri, oh.astype(BF16))
        d1 = jnp.sum(oh1 * pos, axis=-1, keepdims=True).astype(I32)
        d2 = jnp.sum(oh2 * pos, axis=-1, keepdims=True).astype(I32)
        dest_out[pl.ds(base, rc), :] = jnp.where(lane_c == 0, d1, jnp.where(lane_c == 1, d2, 0))
        gate_out[pl.ds(base, rc), :] = jnp.where(lane_c == 0, g1, jnp.where(lane_c == 1, g2, 0.0))
        return carry + jnp.sum(oh, axis=0, keepdims=True)

    lax.fori_loop(0, n_chunks, dest_step, pad_start[0:1, :].astype(F32))

    nbp = meta_out.shape[0]
    lane_b = lax.broadcasted_iota(I32, (nbp, LANES), 1)
    row0 = lax.broadcasted_iota(I32, (nbp, LANES), 0) * rows_per_block
    pe = jnp.broadcast_to(pad_end[0:1, :], (nbp, LANES))
    ps = jnp.broadcast_to(pad_start[0:1, :], (nbp, LANES))
    cn = jnp.broadcast_to(counts[0:1, :], (nbp, LANES))
    is_e = lane_b < N_EXPERTS
    total = jnp.max(pe, axis=-1, keepdims=True)
    n_used = total >> shift
    last_e = jnp.max(jnp.where(is_e & (cn > 0), lane_b, 0), axis=-1, keepdims=True)
    be = jnp.sum(jnp.where(is_e & (pe <= row0), 1, 0), axis=-1, keepdims=True)
    be = jnp.minimum(be, last_e)
    sel = lane_b == be
    cnt_b = jnp.sum(jnp.where(sel, cn, 0), axis=-1, keepdims=True)
    ps_b = jnp.sum(jnp.where(sel, ps, 0), axis=-1, keepdims=True)
    blk = row0[:, 0:1]
    used = blk < total
    nvalid = jnp.where(used, jnp.clip(cnt_b - (blk - ps_b), 0, rows_per_block), 0)
    first = jnp.where(used & (blk == ps_b), 1, 0)
    meta_out[...] = jnp.where(lane_b == 0, be, jnp.where(lane_b == 1, nvalid, jnp.where(
        lane_b == 2, first, jnp.where(lane_b == 3, n_used, 0))))


def _route_call(logits, *, rows_per_block, n_blocks):
    t = logits.shape[0]
    nbp = -(-n_blocks // SUBLANES) * SUBLANES
    return pl.pallas_call(
        functools.partial(_route_kernel, rows_per_block=rows_per_block),
        in_specs=[_whole(logits.shape)],
        out_specs=[_whole((t, LANES)), _whole((t, LANES)), _whole((nbp, LANES))],
        out_shape=[jax.ShapeDtypeStruct((t, LANES), I32), jax.ShapeDtypeStruct((t, LANES), F32),
                   jax.ShapeDtypeStruct((nbp, LANES), I32)],
        grid=(1,),
        compiler_params=pltpu.CompilerParams(dimension_semantics=("arbitrary",), vmem_limit_bytes=VMEM_LIMIT),
        name="route",
    )(logits)


def _pow2_chunks(limit):
    c = 1 << (limit.bit_length() - 1)
    while c >= 1:
        yield c
        c >>= 1


def _expert_kernel(be_ref, nv_ref, first_ref, nused_ref, d0_ref, d1_ref,
                   h_hbm, g2_ref, wg_ref, wu_ref, wd_ref, y_ref,
                   tok_ref, xbuf, wg_bf, wu_bf, wd_bf, gsem):
    b = pl.program_id(0)
    n_used = nused_ref[0]
    groups = xbuf.shape[1]
    rows = groups * SUBLANES
    d = xbuf.shape[3]
    n_tok = d0_ref.shape[0]
    sub_shift = SUBLANES.bit_length() - 1

    def gather_copy(blk, slot, g, u):
        tok = tok_ref[blk * rows + g * SUBLANES + u]
        return pltpu.make_async_copy(h_hbm.at[pl.ds(tok, 1)], xbuf.at[slot, g, pl.ds(u, 1)], gsem.at[slot])

    def start_gather(blk, slot):
        n = nv_ref[blk]

        def group(g, c):
            for u in range(SUBLANES):
                gather_copy(blk, slot, g, u).start()
            return c
        full = lax.shift_right_logical(n, sub_shift)
        lax.fori_loop(0, full, group, 0)
        for u in range(SUBLANES - 1):
            @pl.when(full * SUBLANES + u < n)
            def _():
                gather_copy(blk, slot, full, u).start()

    def wait_gather(blk, slot):
        n = nv_ref[blk]
        buf = xbuf.at[slot]
        for c in _pow2_chunks(rows):
            @pl.when((n & c) != 0)
            def _():
                if c >= SUBLANES:
                    part = buf.at[pl.ds(0, c // SUBLANES)]
                else:
                    part = buf.at[0, pl.ds(0, c)]
                pltpu.make_async_copy(part, part, gsem.at[slot]).wait()

    @pl.when(b == 0)
    def _():
        def inv(g, c):
            for u in range(DMA_UNROLL):
                tk = g * DMA_UNROLL + u
                tok_ref[d0_ref[tk]] = tk
                tok_ref[d1_ref[tk]] = tk
            return c
        lax.fori_loop(0, n_tok // DMA_UNROLL, inv, 0)
        xbuf[...] = jnp.zeros(xbuf.shape, F32)
        start_gather(0, 0)

    @pl.when(b >= n_used)
    def _():
        y_ref[...] = jnp.zeros(y_ref.shape, F32)

    @pl.when(b < n_used)
    def _():
        slot = b & 1
        wait_gather(b, slot)

        @pl.when(b + 1 < n_used)
        def _():
            start_gather(b + 1, 1 - slot)

        @pl.when(first_ref[b] == 1)
        def _():
            wg_bf[...] = wg_ref[...].astype(BF16)
            wu_bf[...] = wu_ref[...].astype(BF16)
            wd_bf[...] = wd_ref[...].astype(BF16)

        hn = _rms(xbuf[slot].reshape(rows, d), g2_ref[...]).astype(BF16)
        gate = _dot(hn, wg_bf[...])
        up = _dot(hn, wu_bf[...])
        hmid = (gate * jax.nn.sigmoid(gate) * up).astype(BF16)
        y_ref[...] = _dot(hmid, wd_bf[...])


def _expert_call(be, nvalid, first, n_used, dest0, dest1, h, g2, w_gate, w_up, w_down, *, rows_per_block, n_blocks):
    t, d = h.shape
    _, _, f = w_gate.shape
    assert t % DMA_UNROLL == 0 and rows_per_block % SUBLANES == 0
    groups = rows_per_block // SUBLANES
    wmap = lambda b, be_r, *_: (be_r[b], 0, 0)
    ymap = lambda b, *_: (b, 0)
    grid_spec = pltpu.PrefetchScalarGridSpec(
        num_scalar_prefetch=6,
        grid=(n_blocks,),
        in_specs=[
            pl.BlockSpec(memory_space=pl.ANY),
            pl.BlockSpec(g2.shape, lambda b, *_: (0, 0)),
            pl.BlockSpec((None, d, f), wmap), pl.BlockSpec((None, d, f), wmap), pl.BlockSpec((None, f, d), wmap),
        ],
        out_specs=pl.BlockSpec((rows_per_block, d), ymap),
        scratch_shapes=[
            pltpu.SMEM((n_blocks * rows_per_block,), I32),
            pltpu.VMEM((2, groups, SUBLANES, d), F32),
            pltpu.VMEM((d, f), BF16), pltpu.VMEM((d, f), BF16), pltpu.VMEM((f, d), BF16),
            pltpu.SemaphoreType.DMA((2,)),
        ],
    )
    return pl.pallas_call(
        _expert_kernel,
        grid_spec=grid_spec,
        out_shape=jax.ShapeDtypeStruct((n_blocks * rows_per_block, d), F32),
        compiler_params=pltpu.CompilerParams(dimension_semantics=("arbitrary",), vmem_limit_bytes=VMEM_LIMIT),
        name="experts",
    )(be, nvalid, first, n_used, dest0, dest1, h, g2, w_gate, w_up, w_down)


def _final_kernel(d0_ref, d1_ref, h_ref, y_hbm, gate_ref, fg_ref, o_ref, ybuf, sem):
    i = pl.program_id(0)
    groups = ybuf.shape[2]
    tm = groups * SUBLANES
    d = ybuf.shape[4]

    def start_tile(tile, slot):
        def group(g, c):
            for u in range(SUBLANES):
                tk = tile * tm + g * SUBLANES + u
                for k, dref in enumerate((d0_ref, d1_ref)):
                    pltpu.make_async_copy(y_hbm.at[pl.ds(dref[tk], 1)], ybuf.at[slot, k, g, pl.ds(u, 1)],
                                          sem.at[slot]).start()
            return c
        lax.fori_loop(0, groups, group, 0)

    @pl.when(i == 0)
    def _():
        start_tile(0, 0)

    slot = i & 1

    @pl.when(i + 1 < pl.num_programs(0))
    def _():
        start_tile(i + 1, 1 - slot)

    pltpu.make_async_copy(ybuf.at[slot], ybuf.at[slot], sem.at[slot]).wait()
    y0 = ybuf[slot, 0].reshape(tm, d)
    y1 = ybuf[slot, 1].reshape(tm, d)
    out = h_ref[...] + gate_ref[:, 0:1] * y0 + gate_ref[:, 1:2] * y1
    o_ref[...] = _rms(out, fg_ref[...])


def _final_call(dest0, dest1, h, y, gates, fg, *, tm):
    t, d = h.shape
    assert tm % SUBLANES == 0
    row = lambda i, *_: (i, 0)
    grid_spec = pltpu.PrefetchScalarGridSpec(
        num_scalar_prefetch=2,
        grid=(t // tm,),
        in_specs=[pl.BlockSpec((tm, d), row), pl.BlockSpec(memory_space=pl.ANY),
                  pl.BlockSpec((tm, LANES), row), pl.BlockSpec(fg.shape, lambda i, *_: (0, 0))],
        out_specs=pl.BlockSpec((tm, d), row),
        scratch_shapes=[pltpu.VMEM((2, 2, tm // SUBLANES, SUBLANES, d), F32), pltpu.SemaphoreType.DMA((2,))],
    )
    return pl.pallas_call(
        _final_kernel,
        grid_spec=grid_spec,
        out_shape=jax.ShapeDtypeStruct((t, d), F32),
        compiler_params=pltpu.CompilerParams(dimension_semantics=("arbitrary",), vmem_limit_bytes=VMEM_LIMIT),
        name="final",
    )(dest0, dest1, h, y, gates, fg)


def _rope_tables(seq):
    pos = jnp.arange(seq, dtype=F32)
    inv_freq = ROPE_THETA ** (-jnp.arange(0, QK_ROPE_DIM, 2, dtype=F32) / QK_ROPE_DIM)
    ang = pos[:, None] * inv_freq[None, :]
    cos, sin = jnp.cos(ang), jnp.sin(ang)
    zero = jnp.zeros_like(sin)
    cos_t = jnp.concatenate([cos, cos, cos, cos], axis=1)
    s1_t = jnp.concatenate([zero, sin, zero, sin], axis=1)
    s2_t = jnp.concatenate([-sin, zero, -sin, zero], axis=1)
    return cos_t, s1_t, s2_t


def kernel(x, ln1_g, w_in, b_glu, q_norm_g, w_uq, kv_norm_g, w_ukv, w_dw, b_dw, conv_ln_g, conv_ln_b,
           w_o, ln2_g, w_group, b_group, w_router, b_router, w_gate, w_up, w_down, final_g):
    batch, seq, d = x.shape
    assert ln1_g.shape[0] == 1, "single-layer trunk"
    t = batch * seq
    q_rank = q_norm_g.shape[1]
    kv_rank = kv_norm_g.shape[1]
    x2 = x.reshape(t, d)

    wi = w_in[0]
    o_kr = q_rank + kv_rank
    o_u = o_kr + QK_ROPE_DIM
    win = jnp.concatenate([wi[:, :o_kr], wi[:, o_u:], wi[:, o_kr:o_u],
                           jnp.zeros((d, LANES - QK_ROPE_DIM), F32)], axis=1).astype(BF16)
    wuq = w_uq[0].reshape(q_rank, MLA_HEADS, QK_NOPE_DIM + QK_ROPE_DIM)
    wuq = jnp.pad(wuq, ((0, 0), (0, 0), (0, HEAD_SLOT - QK_NOPE_DIM - QK_ROPE_DIM)))
    wuq = wuq.reshape(q_rank, MLA_HEADS * HEAD_SLOT).astype(BF16)
    wukv = w_ukv[0].astype(BF16)
    wo = w_o[0].astype(BF16)
    wr = jnp.concatenate([w_router[0], w_group[0],
                          jnp.zeros((d, LANES - N_EXPERTS - N_EXPERT_GROUPS), F32)], axis=1)
    wr_hi = wr.astype(BF16)
    wr_lo = (wr - wr_hi.astype(F32)).astype(BF16)
    wr2 = jnp.concatenate([wr_hi, wr_lo], axis=1)
    br = jnp.concatenate([b_router[0], b_group[0],
                          jnp.zeros((LANES - N_EXPERTS - N_EXPERT_GROUPS,), F32)])[None, :]
    cos_t, s1_t, s2_t = _rope_tables(seq)

    tm = min(512, seq)
    q, k, v, c = _proj_call(x2, ln1_g, win, b_glu, q_norm_g, kv_norm_g, wuq, wukv, cos_t, s1_t, s2_t,
                            seq=seq, tm=tm)
    attn = _attn_call(q, k, v, batch=batch, seq=seq, tq=min(512, seq))
    conv = _conv_call(c, w_dw[0], b_dw, conv_ln_g, conv_ln_b, batch=batch, seq=seq)
    h, logits = _oproj_call(attn, conv, x2, wo, ln2_g, wr2, br, tm=tm)

    n_blocks = -(-(2 * t + N_EXPERTS * (MOE_ROWS - 1)) // MOE_ROWS)
    dest, gates, meta = _route_call(logits, rows_per_block=MOE_ROWS, n_blocks=n_blocks)
    dest0, dest1 = dest[:, 0], dest[:, 1]
    y = _expert_call(meta[:n_blocks, 0], meta[:n_blocks, 1], meta[:n_blocks, 2], meta[0:1, 3],
                     dest0, dest1, h, ln2_g, w_gate[0], w_up[0], w_down[0],
                     rows_per_block=MOE_ROWS, n_blocks=n_blocks)
    out = _final_call(dest0, dest1, h, y, gates, final_g[None, :], tm=min(256, seq))
    return out.reshape(batch, seq, d)
```

```python
import functools

import jax
import jax.numpy as jnp
from jax import lax
from jax.experimental import pallas as pl
from jax.experimental.pallas import tpu as pltpu

F32 = jnp.float32
BF16 = jnp.bfloat16
I32 = jnp.int32

MLA_HEADS = 8
QK_NOPE_DIM = 128
QK_ROPE_DIM = 64
V_HEAD_DIM = 128
ROPE_THETA = 10000.0
N_EXPERT_GROUPS = 8
EXPERTS_PER_GROUP = 8
N_EXPERTS = N_EXPERT_GROUPS * EXPERTS_PER_GROUP
EPS = 1e-6
LOG2E = 1.4426950408889634

LANES = 128
SUBLANES = 8
HEAD_SLOT = 2 * LANES
ROPE_HALF = QK_ROPE_DIM // 2
VMEM_LIMIT = 56 * 1024 * 1024

MOE_ROWS = 256
ROUTE_CHUNK = 256
EXPERT_WEIGHT_BUFFERS = 2
DMA_UNROLL = 8
DMA_UNROLL_LOG2 = DMA_UNROLL.bit_length() - 1


def _rms(x, g):
    return x * lax.rsqrt(jnp.mean(x * x, axis=-1, keepdims=True) + EPS) * g


def _dot(a, b):
    return jnp.dot(a, b, preferred_element_type=F32)


def _whole(shape, single=False):
    mode = dict(pipeline_mode=pl.Buffered(1)) if single else {}
    return pl.BlockSpec(shape, lambda *_: (0,) * len(shape), **mode)


def _proj_kernel(x_ref, g1_ref, wlat_ref, wu_ref, wkr_ref, bglu_ref, qg_ref, kvg_ref, wuq_ref, wukv_ref,
                 cos_ref, s1_ref, s2_ref, q_out, k_out, v_out, c_out, *, q_rank, kv_rank, conv_ch, q_scale):
    xn = _rms(x_ref[...], g1_ref[...]).astype(BF16)
    cos = cos_ref[...]
    s1 = s1_ref[...]
    s2 = s2_ref[...]

    def rope(t):
        return t * cos + pltpu.roll(t, ROPE_HALF, 1) * s1 + pltpu.roll(t, LANES - ROPE_HALF, 1) * s2

    a = _dot(xn, wu_ref[:, :conv_ch]) + bglu_ref[:, :conv_ch]
    gate = _dot(xn, wu_ref[:, conv_ch:]) + bglu_ref[:, conv_ch:]
    c_out[...] = (a * jax.nn.sigmoid(gate)).astype(BF16)

    kr = rope(_dot(xn, wkr_ref[...])).astype(BF16)
    qn = _rms(_dot(xn, wlat_ref[:, :q_rank]), qg_ref[...]).astype(BF16)
    kvn = _rms(_dot(xn, wlat_ref[:, q_rank:q_rank + kv_rank]), kvg_ref[...]).astype(BF16)
    ones_blk = (lax.broadcasted_iota(I32, (x_ref.shape[0], LANES), 1) == 0).astype(BF16)
    for h in range(MLA_HEADS):
        c0 = h * HEAD_SLOT
        qh = _dot(qn, wuq_ref[:, c0:c0 + HEAD_SLOT])
        q_out[:, c0:c0 + LANES] = (qh[:, :LANES] * q_scale).astype(BF16)
        q_out[:, c0 + LANES:c0 + HEAD_SLOT] = (rope(qh[:, LANES:]) * q_scale).astype(BF16)
        kvh = _dot(kvn, wukv_ref[:, c0:c0 + HEAD_SLOT])
        k_out[:, c0:c0 + LANES] = kvh[:, :LANES].astype(BF16)
        k_out[:, c0 + LANES:c0 + HEAD_SLOT] = kr
        v_out[:, c0:c0 + LANES] = kvh[:, LANES:].astype(BF16)
        v_out[:, c0 + LANES:c0 + HEAD_SLOT] = ones_blk


def _proj_call(x2, g1, wlat, wu, wkr, bglu, qg, kvg, wuq, wukv, cos_t, s1_t, s2_t, *, seq, tm):
    t, d = x2.shape
    q_rank, kv_rank = qg.shape[1], kvg.shape[1]
    conv_ch = bglu.shape[1] // 2
    n_pos = seq // tm
    row = lambda i: (i, 0)
    pos = lambda i: (i % n_pos, 0)
    q_scale = float(QK_NOPE_DIM + QK_ROPE_DIM) ** -0.5 * LOG2E
    kern = functools.partial(_proj_kernel, q_rank=q_rank, kv_rank=kv_rank, conv_ch=conv_ch, q_scale=q_scale)
    slot_w = MLA_HEADS * HEAD_SLOT
    return pl.pallas_call(
        kern,
        grid=(t // tm,),
        in_specs=[
            pl.BlockSpec((tm, d), row), _whole(g1.shape), _whole(wlat.shape, True), _whole(wu.shape, True),
            _whole(wkr.shape, True), _whole(bglu.shape),
            _whole(qg.shape), _whole(kvg.shape), _whole(wuq.shape, True), _whole(wukv.shape, True),
            pl.BlockSpec((tm, LANES), pos), pl.BlockSpec((tm, LANES), pos), pl.BlockSpec((tm, LANES), pos),
        ],
        out_specs=[
            pl.BlockSpec((tm, slot_w), row), pl.BlockSpec((tm, slot_w), row),
            pl.BlockSpec((tm, slot_w), row), pl.BlockSpec((tm, conv_ch), row),
        ],
        out_shape=[
            jax.ShapeDtypeStruct((t, slot_w), BF16), jax.ShapeDtypeStruct((t, slot_w), BF16),
            jax.ShapeDtypeStruct((t, slot_w), BF16), jax.ShapeDtypeStruct((t, conv_ch), BF16),
        ],
        compiler_params=pltpu.CompilerParams(dimension_semantics=("arbitrary",), vmem_limit_bytes=VMEM_LIMIT),
        name="proj",
    )(x2, g1, wlat, wu, wkr, bglu, qg, kvg, wuq, wukv, cos_t, s1_t, s2_t)


CONV_PAD = 16
CONV_ROWS = 128


def _attn_conv_kernel(q_ref, k_ref, v_ref, c_ref, w_ref, b_ref, o_ref, y_ref, xp_ref, sh_ref, *, tq, width):
    s_len = q_ref.shape[0]
    half = width // 2
    rows = min(CONV_ROWS, s_len)
    span = sh_ref.shape[1]
    zeros = jnp.zeros((CONV_PAD, LANES), F32)
    xp_ref[0:CONV_PAD, :] = zeros
    xp_ref[CONV_PAD + s_len:, :] = zeros
    xp_ref[CONV_PAD:CONV_PAD + s_len, :] = c_ref[...].astype(F32)

    def conv_chunk(ci):
        base = ci * rows
        xw = xp_ref[base:base + rows + 2 * CONV_PAD, :]
        for r in range(1, SUBLANES):
            sh_ref[r - 1] = xw[r:r + span, :]
        acc = jnp.zeros((rows, LANES), F32)
        for k in range(width):
            off = CONV_PAD - half + k
            r, a0 = off % SUBLANES, off - off % SUBLANES
            if r == 0:
                tap = xp_ref[base + a0:base + a0 + rows, :]
            else:
                tap = sh_ref[r - 1, a0:a0 + rows, :]
            acc = acc + tap * w_ref[k:k + 1, :]
        y_ref[base:base + rows, :] = (acc + b_ref[...]).astype(BF16)

    k = k_ref[...]
    v = v_ref[...]
    n_q = s_len // tq
    n_chunks = s_len // rows
    for j in range(n_q):
        qs = slice(j * tq, (j + 1) * tq)
        s = lax.dot_general(q_ref[qs, :], k, (((1,), (1,)), ((), ())), preferred_element_type=F32)
        m = jnp.max(s, axis=-1, keepdims=True)
        p = jnp.exp2(s - m).astype(BF16)
        o = _dot(p, v)
        o_ref[qs, :] = (o[:, :V_HEAD_DIM] / o[:, V_HEAD_DIM:V_HEAD_DIM + 1]).astype(BF16)
        for ci in range(j * n_chunks // n_q, (j + 1) * n_chunks // n_q):
            conv_chunk(ci)


def _attn_conv_call(q, k, v, c, w_dw, b_dw, *, batch, seq, tq):
    t, ch = c.shape
    width = w_dw.shape[0]
    assert width // 2 <= CONV_PAD and ch == MLA_HEADS * LANES
    rows = min(CONV_ROWS, seq)
    head = lambda b, h: (b, h)
    chan = lambda b, h: (0, h)
    return pl.pallas_call(
        functools.partial(_attn_conv_kernel, tq=tq, width=width),
        grid=(batch, MLA_HEADS),
        in_specs=[pl.BlockSpec((seq, HEAD_SLOT), head), pl.BlockSpec((seq, HEAD_SLOT), head),
                  pl.BlockSpec((seq, HEAD_SLOT), head), pl.BlockSpec((seq, LANES), head),
                  pl.BlockSpec((width, LANES), chan), pl.BlockSpec((1, LANES), chan)],
        out_specs=[pl.BlockSpec((seq, V_HEAD_DIM), head), pl.BlockSpec((seq, LANES), head)],
        out_shape=[jax.ShapeDtypeStruct((t, MLA_HEADS * V_HEAD_DIM), BF16), jax.ShapeDtypeStruct((t, ch), BF16)],
        scratch_shapes=[pltpu.VMEM((seq + 2 * CONV_PAD, LANES), F32),
                        pltpu.VMEM((SUBLANES - 1, rows + 2 * CONV_PAD - SUBLANES, LANES), F32)],
        compiler_params=pltpu.CompilerParams(
            dimension_semantics=("arbitrary", "arbitrary"), vmem_limit_bytes=VMEM_LIMIT),
        name="attn_conv",
    )(q, k, v, c, w_dw, b_dw)


def _oproj_kernel(a_ref, c_ref, x_ref, wo_ref, lg_ref, lb_ref, g2_ref, wr_ref, br_ref, h_out, lg_out):
    na = a_ref.shape[1]
    y = c_ref[...].astype(F32)
    yc = y - jnp.mean(y, axis=-1, keepdims=True)
    z = yc * lax.rsqrt(jnp.mean(yc * yc, axis=-1, keepdims=True) + EPS) * lg_ref[...] + lb_ref[...]
    act = (z * jax.nn.sigmoid(z)).astype(BF16)
    h = x_ref[...] + _dot(a_ref[...], wo_ref[:na, :]) + _dot(act, wo_ref[na:, :])
    h_out[...] = h
    hn = _rms(h, g2_ref[...])
    hi = hn.astype(BF16)
    lo = (hn - hi.astype(F32)).astype(BF16)
    r = _dot(hi, wr_ref[...])
    lg_out[...] = r[:, :LANES] + r[:, LANES:] + _dot(lo, wr_ref[:, :LANES]) + br_ref[...]


def _oproj_call(attn, conv, x2, wo, ln_g, ln_b, g2, wr, br, *, tm):
    t, d = x2.shape
    row = lambda i: (i, 0)
    return pl.pallas_call(
        _oproj_kernel,
        grid=(t // tm,),
        in_specs=[pl.BlockSpec((tm, attn.shape[1]), row), pl.BlockSpec((tm, conv.shape[1]), row),
                  pl.BlockSpec((tm, d), row), _whole(wo.shape, True), _whole(ln_g.shape), _whole(ln_b.shape),
                  _whole(g2.shape), _whole(wr.shape, True), _whole(br.shape)],
        out_specs=[pl.BlockSpec((tm, d), row), pl.BlockSpec((tm, LANES), row)],
        out_shape=[jax.ShapeDtypeStruct((t, d), F32), jax.ShapeDtypeStruct((t, LANES), F32)],
        compiler_params=pltpu.CompilerParams(dimension_semantics=("arbitrary",), vmem_limit_bytes=VMEM_LIMIT),
        name="oproj",
    )(attn, conv, x2, wo, ln_g, ln_b, g2, wr, br)


def _route_rows(lg):
    lane = lax.broadcasted_iota(I32, lg.shape, 1)
    big = jnp.int32(1 << 20)
    neg = jnp.float32(-jnp.inf)
    is_g = (lane >= N_EXPERTS) & (lane < N_EXPERTS + N_EXPERT_GROUPS)
    lgrp = jnp.where(is_g, lg, neg)
    gmax = jnp.max(lgrp, axis=-1, keepdims=True)
    gsel = jnp.min(jnp.where(lgrp == gmax, lane, big), axis=-1, keepdims=True) - N_EXPERTS
    p_g = 1.0 / jnp.sum(jnp.where(is_g, jnp.exp(lg - gmax), 0.0), axis=-1, keepdims=True)
    lo = gsel * EXPERTS_PER_GROUP
    in_grp = (lane >= lo) & (lane < lo + EXPERTS_PER_GROUP)
    le = jnp.where(in_grp, lg, neg)
    m1 = jnp.max(le, axis=-1, keepdims=True)
    i1 = jnp.min(jnp.where(le == m1, lane, big), axis=-1, keepdims=True)
    le2 = jnp.where(lane == i1, neg, le)
    m2 = jnp.max(le2, axis=-1, keepdims=True)
    i2 = jnp.min(jnp.where(le2 == m2, lane, big), axis=-1, keepdims=True)
    r = jnp.exp(m2 - m1)
    w1 = 1.0 / (1.0 + r)
    w2 = r / (1.0 + r)
    oh1 = (lane == i1).astype(F32)
    oh2 = (lane == i2).astype(F32)
    return oh1, oh2, p_g * w1, p_g * w2, i1, i2


def _lane_cumsum(v):
    lane = lax.broadcasted_iota(I32, v.shape, 1)
    sh = 1
    while sh < LANES:
        v = v + jnp.where(lane >= sh, pltpu.roll(v, sh, 1), 0)
        sh *= 2
    return v


def _route_kernel(lg_ref, d0_out, d1_out, gate_out, meta_out, idx_ref, *, rows_per_block):
    t = lg_ref.shape[0]
    n_chunks, _, rc = d0_out.shape
    shift = rows_per_block.bit_length() - 1
    lane_c = lax.broadcasted_iota(I32, (rc, LANES), 1)

    def two_cols(c0, c1, zero):
        return jnp.where(lane_c == 0, c0, jnp.where(lane_c == 1, c1, zero))

    def count_step(i, acc):
        base = pl.multiple_of(i * rc, rc)
        oh1, oh2, g1, g2, i1, i2 = _route_rows(lg_ref[pl.ds(base, rc), :])
        gate_out[pl.ds(base, rc), :] = two_cols(g1, g2, 0.0)
        idx_ref[pl.ds(base, rc), :] = two_cols(i1, i2, 0)
        return acc + jnp.sum(oh1 + oh2, axis=0, keepdims=True)

    counts_f = lax.fori_loop(0, n_chunks, count_step, jnp.zeros((1, LANES), F32))
    counts = jnp.broadcast_to(counts_f, (SUBLANES, LANES)).astype(I32)
    padded = ((counts + (rows_per_block - 1)) >> shift) << shift
    pad_end = _lane_cumsum(padded)
    pad_start = pad_end - padded

    r_i = lax.broadcasted_iota(I32, (rc, rc), 0)
    c_i = lax.broadcasted_iota(I32, (rc, rc), 1)
    tri = (r_i > c_i).astype(BF16)

    def dest_step(i, carry):
        base = pl.multiple_of(i * rc, rc)
        idx = idx_ref[pl.ds(base, rc), :]
        oh1 = (lane_c == idx[:, 0:1]).astype(F32)
        oh2 = (lane_c == idx[:, 1:2]).astype(F32)
        oh = oh1 + oh2
        pos = carry + _dot(tri, oh.astype(BF16))
        d1 = jnp.sum(oh1 * pos, axis=-1, keepdims=True)
        d2 = jnp.sum(oh2 * pos, axis=-1, keepdims=True)
        tr = jnp.transpose(two_cols(d1, d2, 0.0))
        d0_out[i] = tr[0:1, :].astype(I32)
        d1_out[i] = tr[1:2, :].astype(I32)
        return carry + jnp.sum(oh, axis=0, keepdims=True)

    lax.fori_loop(0, n_chunks, dest_step, pad_start[0:1, :].astype(F32))

    nbp = meta_out.shape[0]
    lane_b = lax.broadcasted_iota(I32, (nbp, LANES), 1)
    row0 = lax.broadcasted_iota(I32, (nbp, LANES), 0) * rows_per_block
    pe = jnp.broadcast_to(pad_end[0:1, :], (nbp, LANES))
    ps = jnp.broadcast_to(pad_start[0:1, :], (nbp, LANES))
    cn = jnp.broadcast_to(counts[0:1, :], (nbp, LANES))
    is_e = lane_b < N_EXPERTS
    total = jnp.max(pe, axis=-1, keepdims=True)
    n_used = total >> shift
    last_e = jnp.max(jnp.where(is_e & (cn > 0), lane_b, 0), axis=-1, keepdims=True)
    be = jnp.sum(jnp.where(is_e & (pe <= row0), 1, 0), axis=-1, keepdims=True)
    be = jnp.minimum(be, last_e)
    sel = lane_b == be
    cnt_b = jnp.sum(jnp.where(sel, cn, 0), axis=-1, keepdims=True)
    ps_b = jnp.sum(jnp.where(sel, ps, 0), axis=-1, keepdims=True)
    blk = row0[:, 0:1]
    used = blk < total
    nvalid = jnp.where(used, jnp.clip(cnt_b - (blk - ps_b), 0, rows_per_block), 0)
    meta_out[...] = jnp.where(lane_b == 0, be, jnp.where(lane_b == 1, nvalid, jnp.where(lane_b == 2, n_used, 0)))


def _route_call(logits, *, rows_per_block, n_blocks):
    t = logits.shape[0]
    rc = min(ROUTE_CHUNK, t)
    nbp = -(-n_blocks // SUBLANES) * SUBLANES
    dshape = (t // rc, 1, rc)
    d0, d1, gates, meta = pl.pallas_call(
        functools.partial(_route_kernel, rows_per_block=rows_per_block),
        in_specs=[_whole(logits.shape)],
        out_specs=[_whole(dshape), _whole(dshape), _whole((t, LANES)), _whole((nbp, LANES))],
        out_shape=[jax.ShapeDtypeStruct(dshape, I32), jax.ShapeDtypeStruct(dshape, I32),
                   jax.ShapeDtypeStruct((t, LANES), F32), jax.ShapeDtypeStruct((nbp, LANES), I32)],
        grid=(1,),
        scratch_shapes=[pltpu.VMEM((t, LANES), I32)],
        compiler_params=pltpu.CompilerParams(dimension_semantics=("arbitrary",), vmem_limit_bytes=VMEM_LIMIT),
        name="route",
    )(logits)
    return d0.reshape(t), d1.reshape(t), gates, meta


def _pow2_chunks(limit):
    c = 1 << (limit.bit_length() - 1)
    while c >= 1:
        yield c
        c >>= 1


def _expert_kernel(be_ref, nv_ref, nused_ref, d0_ref, d1_ref,
                   h_hbm, g2_ref, wg_ref, wu_ref, wd_ref, y_ref,
                   tok_ref, xbuf, gsem):
    b = pl.program_id(0)
    n_used = nused_ref[0]
    groups = xbuf.shape[1]
    rows = groups * SUBLANES
    d = xbuf.shape[3]
    n_tok = d0_ref.shape[0]
    sub_shift = SUBLANES.bit_length() - 1

    def gather_copy(blk, slot, g, u):
        tok = tok_ref[blk * rows + g * SUBLANES + u]
        return pltpu.make_async_copy(h_hbm.at[pl.ds(tok, 1)], xbuf.at[slot, g, pl.ds(u, 1)], gsem.at[slot])

    def start_gather(blk, slot):
        n = nv_ref[blk]

        def group(g, c):
            for u in range(SUBLANES):
                gather_copy(blk, slot, g, u).start()
            return c
        full = lax.shift_right_logical(n, sub_shift)
        lax.fori_loop(0, full, group, 0)
        for u in range(SUBLANES - 1):
            @pl.when(full * SUBLANES + u < n)
            def _():
                gather_copy(blk, slot, full, u).start()

    def wait_gather(blk, slot):
        n = nv_ref[blk]
        buf = xbuf.at[slot]
        for c in _pow2_chunks(rows):
            @pl.when((n & c) != 0)
            def _():
                if c >= SUBLANES:
                    part = buf.at[pl.ds(0, c // SUBLANES)]
                else:
                    part = buf.at[0, pl.ds(0, c)]
                pltpu.make_async_copy(part, part, gsem.at[slot]).wait()

    @pl.when(b == 0)
    def _():
        def inv(g, c):
            for u in range(DMA_UNROLL):
                tk = g * DMA_UNROLL + u
                tok_ref[d0_ref[tk]] = tk
                tok_ref[d1_ref[tk]] = tk
            return c
        lax.fori_loop(0, n_tok // DMA_UNROLL, inv, 0)
        xbuf[...] = jnp.zeros(xbuf.shape, F32)
        start_gather(0, 0)

    @pl.when(b >= n_used)
    def _():
        y_ref[...] = jnp.zeros(y_ref.shape, F32)

    @pl.when(b < n_used)
    def _():
        slot = b & 1
        wait_gather(b, slot)

        @pl.when(b + 1 < n_used)
        def _():
            start_gather(b + 1, 1 - slot)

        hn = _rms(xbuf[slot].reshape(rows, d), g2_ref[...]).astype(BF16)
        gate = _dot(hn, wg_ref[...].astype(BF16))
        up = _dot(hn, wu_ref[...].astype(BF16))
        hmid = (gate * jax.nn.sigmoid(gate) * up).astype(BF16)
        y_ref[...] = _dot(hmid, wd_ref[...].astype(BF16))


def _expert_call(be, nvalid, n_used, dest0, dest1, h, g2, w_gate, w_up, w_down, *, rows_per_block, n_blocks):
    t, d = h.shape
    _, _, f = w_gate.shape
    assert t % DMA_UNROLL == 0 and rows_per_block % SUBLANES == 0
    groups = rows_per_block // SUBLANES
    wmap = lambda b, be_r, *_: (be_r[b], 0, 0)
    ymap = lambda b, *_: (b, 0)
    deep = dict(pipeline_mode=pl.Buffered(EXPERT_WEIGHT_BUFFERS))
    grid_spec = pltpu.PrefetchScalarGridSpec(
        num_scalar_prefetch=5,
        grid=(n_blocks,),
        in_specs=[
            pl.BlockSpec(memory_space=pl.ANY),
            pl.BlockSpec(g2.shape, lambda b, *_: (0, 0)),
            pl.BlockSpec((None, d, f), wmap, **deep), pl.BlockSpec((None, d, f), wmap, **deep),
            pl.BlockSpec((None, f, d), wmap, **deep),
        ],
        out_specs=pl.BlockSpec((rows_per_block, d), ymap),
        scratch_shapes=[
            pltpu.SMEM((n_blocks * rows_per_block,), I32),
            pltpu.VMEM((2, groups, SUBLANES, d), F32),
            pltpu.SemaphoreType.DMA((2,)),
        ],
    )
    return pl.pallas_call(
        _expert_kernel,
        grid_spec=grid_spec,
        out_shape=jax.ShapeDtypeStruct((n_blocks * rows_per_block, d), F32),
        compiler_params=pltpu.CompilerParams(dimension_semantics=("arbitrary",), vmem_limit_bytes=VMEM_LIMIT),
        name="experts",
    )(be, nvalid, n_used, dest0, dest1, h, g2, w_gate, w_up, w_down)


def _final_kernel(d0_ref, d1_ref, h_ref, y_hbm, gate_ref, fg_ref, o_ref, ybuf, sem):
    i = pl.program_id(0)
    groups = ybuf.shape[2]
    tm = groups * SUBLANES
    d = ybuf.shape[4]

    def start_tile(tile, slot):
        def group(g, c):
            for u in range(SUBLANES):
                tk = tile * tm + g * SUBLANES + u
                for k, dref in enumerate((d0_ref, d1_ref)):
                    pltpu.make_async_copy(y_hbm.at[pl.ds(dref[tk], 1)], ybuf.at[slot, k, g, pl.ds(u, 1)],
                                          sem.at[slot]).start()
            return c
        lax.fori_loop(0, groups, group, 0)

    @pl.when(i == 0)
    def _():
        start_tile(0, 0)

    slot = i & 1

    @pl.when(i + 1 < pl.num_programs(0))
    def _():
        start_tile(i + 1, 1 - slot)

    pltpu.make_async_copy(ybuf.at[slot], ybuf.at[slot], sem.at[slot]).wait()
    y0 = ybuf[slot, 0].reshape(tm, d)
    y1 = ybuf[slot, 1].reshape(tm, d)
    out = h_ref[...] + gate_ref[:, 0:1] * y0 + gate_ref[:, 1:2] * y1
    o_ref[...] = _rms(out, fg_ref[...])


def _final_call(dest0, dest1, h, y, gates, fg, *, tm):
    t, d = h.shape
    assert tm % SUBLANES == 0
    row = lambda i, *_: (i, 0)
    grid_spec = pltpu.PrefetchScalarGridSpec(
        num_scalar_prefetch=2,
        grid=(t // tm,),
        in_specs=[pl.BlockSpec((tm, d), row), pl.BlockSpec(memory_space=pl.ANY),
                  pl.BlockSpec((tm, LANES), row), pl.BlockSpec(fg.shape, lambda i, *_: (0, 0))],
        out_specs=pl.BlockSpec((tm, d), row),
        scratch_shapes=[pltpu.VMEM((2, 2, tm // SUBLANES, SUBLANES, d), F32), pltpu.SemaphoreType.DMA((2,))],
    )
    return pl.pallas_call(
        _final_kernel,
        grid_spec=grid_spec,
        out_shape=jax.ShapeDtypeStruct((t, d), F32),
        compiler_params=pltpu.CompilerParams(dimension_semantics=("arbitrary",), vmem_limit_bytes=VMEM_LIMIT),
        name="final",
    )(dest0, dest1, h, y, gates, fg)


def _rope_tables(seq):
    pos = jnp.arange(seq, dtype=F32)
    inv_freq = ROPE_THETA ** (-jnp.arange(0, QK_ROPE_DIM, 2, dtype=F32) / QK_ROPE_DIM)
    ang = pos[:, None] * inv_freq[None, :]
    cos, sin = jnp.cos(ang), jnp.sin(ang)
    zero = jnp.zeros_like(sin)
    cos_t = jnp.concatenate([cos, cos, cos, cos], axis=1)
    s1_t = jnp.concatenate([zero, sin, zero, sin], axis=1)
    s2_t = jnp.concatenate([-sin, zero, -sin, zero], axis=1)
    return cos_t, s1_t, s2_t


def kernel(x, ln1_g, w_in, b_glu, q_norm_g, w_uq, kv_norm_g, w_ukv, w_dw, b_dw, conv_ln_g, conv_ln_b,
           w_o, ln2_g, w_group, b_group, w_router, b_router, w_gate, w_up, w_down, final_g):
    batch, seq, d = x.shape
    assert ln1_g.shape[0] == 1, "single-layer trunk"
    t = batch * seq
    q_rank = q_norm_g.shape[1]
    kv_rank = kv_norm_g.shape[1]
    x2 = x.reshape(t, d)

    wi = w_in[0]
    o_kr = q_rank + kv_rank
    o_u = o_kr + QK_ROPE_DIM
    wlat = wi[:, :o_kr].astype(BF16)
    wu = wi[:, o_u:].astype(BF16)
    wkr = jnp.pad(wi[:, o_kr:o_u], ((0, 0), (0, LANES - QK_ROPE_DIM))).astype(BF16)
    wuq = w_uq[0].reshape(q_rank, MLA_HEADS, QK_NOPE_DIM + QK_ROPE_DIM)
    wuq = jnp.pad(wuq, ((0, 0), (0, 0), (0, HEAD_SLOT - QK_NOPE_DIM - QK_ROPE_DIM)))
    wuq = wuq.reshape(q_rank, MLA_HEADS * HEAD_SLOT).astype(BF16)
    wukv = w_ukv[0].astype(BF16)
    wo = w_o[0].astype(BF16)
    wr = jnp.concatenate([w_router[0], w_group[0],
                          jnp.zeros((d, LANES - N_EXPERTS - N_EXPERT_GROUPS), F32)], axis=1)
    wr_hi = wr.astype(BF16)
    wr_lo = (wr - wr_hi.astype(F32)).astype(BF16)
    wr2 = jnp.concatenate([wr_hi, wr_lo], axis=1)
    br = jnp.concatenate([b_router[0], b_group[0],
                          jnp.zeros((LANES - N_EXPERTS - N_EXPERT_GROUPS,), F32)])[None, :]
    cos_t, s1_t, s2_t = _rope_tables(seq)

    tm = min(512, seq)
    q, k, v, c = _proj_call(x2, ln1_g, wlat, wu, wkr, b_glu, q_norm_g, kv_norm_g, wuq, wukv, cos_t, s1_t, s2_t,
                            seq=seq, tm=tm)
    attn, conv = _attn_conv_call(q, k, v, c, w_dw[0], b_dw, batch=batch, seq=seq, tq=min(512, seq))
    h, logits = _oproj_call(attn, conv, x2, wo, conv_ln_g, conv_ln_b, ln2_g, wr2, br, tm=tm)

    n_blocks = -(-(2 * t + N_EXPERTS * (MOE_ROWS - 1)) // MOE_ROWS)
    dest0, dest1, gates, meta = _route_call(logits, rows_per_block=MOE_ROWS, n_blocks=n_blocks)
    y = _expert_call(meta[:n_blocks, 0], meta[:n_blocks, 1], meta[0:1, 2],
                     dest0, dest1, h, ln2_g, w_gate[0], w_up[0], w_down[0],
                     rows_per_block=MOE_ROWS, n_blocks=n_blocks)
    out = _final_call(dest0, dest1, h, y, gates, final_g[None, :], tm=min(256, seq))
    return out.reshape(batch, seq, d)
```

```python
import functools

import jax
import jax.numpy as jnp
from jax import lax
from jax.experimental import pallas as pl
from jax.experimental.pallas import tpu as pltpu

F32 = jnp.float32
BF16 = jnp.bfloat16
I32 = jnp.int32

MLA_HEADS = 8
QK_NOPE_DIM = 128
QK_ROPE_DIM = 64
V_HEAD_DIM = 128
ROPE_THETA = 10000.0
N_EXPERT_GROUPS = 8
EXPERTS_PER_GROUP = 8
N_EXPERTS = N_EXPERT_GROUPS * EXPERTS_PER_GROUP
EPS = 1e-6
LOG2E = 1.4426950408889634

LANES = 128
SUBLANES = 8
HEAD_SLOT = 2 * LANES
ROPE_HALF = QK_ROPE_DIM // 2
VMEM_LIMIT = 56 * 1024 * 1024

MOE_ROWS = 256
ROUTE_CHUNK = 256
EXPERT_WEIGHT_SLOTS = 3
DMA_UNROLL = 8
DMA_UNROLL_LOG2 = DMA_UNROLL.bit_length() - 1


def _rms(x, g):
    return x * lax.rsqrt(jnp.mean(x * x, axis=-1, keepdims=True) + EPS) * g


def _dot(a, b):
    return jnp.dot(a, b, preferred_element_type=F32)


def _whole(shape, single=False):
    mode = dict(pipeline_mode=pl.Buffered(1)) if single else {}
    return pl.BlockSpec(shape, lambda *_: (0,) * len(shape), **mode)


def _proj_kernel(x_ref, g1_ref, wlat_ref, wu_ref, wkr_ref, bglu_ref, qg_ref, kvg_ref, wuq_ref, wukv_ref,
                 cos_ref, s1_ref, s2_ref, q_out, k_out, v_out, c_out, *, q_rank, kv_rank, conv_ch, q_scale):
    xn = _rms(x_ref[...], g1_ref[...]).astype(BF16)
    cos = cos_ref[...]
    s1 = s1_ref[...]
    s2 = s2_ref[...]

    def rope(t):
        return t * cos + pltpu.roll(t, ROPE_HALF, 1) * s1 + pltpu.roll(t, LANES - ROPE_HALF, 1) * s2

    a = _dot(xn, wu_ref[:, :conv_ch]) + bglu_ref[:, :conv_ch]
    gate = _dot(xn, wu_ref[:, conv_ch:]) + bglu_ref[:, conv_ch:]
    c_out[...] = (a * jax.nn.sigmoid(gate)).astype(BF16)

    kr = rope(_dot(xn, wkr_ref[...])).astype(BF16)
    qn = _rms(_dot(xn, wlat_ref[:, :q_rank]), qg_ref[...]).astype(BF16)
    kvn = _rms(_dot(xn, wlat_ref[:, q_rank:q_rank + kv_rank]), kvg_ref[...]).astype(BF16)
    ones_blk = (lax.broadcasted_iota(I32, (x_ref.shape[0], LANES), 1) == 0).astype(BF16)
    for h in range(MLA_HEADS):
        c0 = h * HEAD_SLOT
        qh = _dot(qn, wuq_ref[:, c0:c0 + HEAD_SLOT])
        q_out[:, c0:c0 + LANES] = (qh[:, :LANES] * q_scale).astype(BF16)
        q_out[:, c0 + LANES:c0 + HEAD_SLOT] = (rope(qh[:, LANES:]) * q_scale).astype(BF16)
        kvh = _dot(kvn, wukv_ref[:, c0:c0 + HEAD_SLOT])
        k_out[:, c0:c0 + LANES] = kvh[:, :LANES].astype(BF16)
        k_out[:, c0 + LANES:c0 + HEAD_SLOT] = kr
        v_out[:, c0:c0 + LANES] = kvh[:, LANES:].astype(BF16)
        v_out[:, c0 + LANES:c0 + HEAD_SLOT] = ones_blk


def _proj_call(x2, g1, wlat, wu, wkr, bglu, qg, kvg, wuq, wukv, cos_t, s1_t, s2_t, *, seq, tm):
    t, d = x2.shape
    q_rank, kv_rank = qg.shape[1], kvg.shape[1]
    conv_ch = bglu.shape[1] // 2
    n_pos = seq // tm
    row = lambda i: (i, 0)
    pos = lambda i: (i % n_pos, 0)
    q_scale = float(QK_NOPE_DIM + QK_ROPE_DIM) ** -0.5 * LOG2E
    kern = functools.partial(_proj_kernel, q_rank=q_rank, kv_rank=kv_rank, conv_ch=conv_ch, q_scale=q_scale)
    slot_w = MLA_HEADS * HEAD_SLOT
    return pl.pallas_call(
        kern,
        grid=(t // tm,),
        in_specs=[
            pl.BlockSpec((tm, d), row), _whole(g1.shape), _whole(wlat.shape, True), _whole(wu.shape, True),
            _whole(wkr.shape, True), _whole(bglu.shape),
            _whole(qg.shape), _whole(kvg.shape), _whole(wuq.shape, True), _whole(wukv.shape, True),
            pl.BlockSpec((tm, LANES), pos), pl.BlockSpec((tm, LANES), pos), pl.BlockSpec((tm, LANES), pos),
        ],
        out_specs=[
            pl.BlockSpec((tm, slot_w), row), pl.BlockSpec((tm, slot_w), row),
            pl.BlockSpec((tm, slot_w), row), pl.BlockSpec((tm, conv_ch), row),
        ],
        out_shape=[
            jax.ShapeDtypeStruct((t, slot_w), BF16), jax.ShapeDtypeStruct((t, slot_w), BF16),
            jax.ShapeDtypeStruct((t, slot_w), BF16), jax.ShapeDtypeStruct((t, conv_ch), BF16),
        ],
        compiler_params=pltpu.CompilerParams(dimension_semantics=("arbitrary",), vmem_limit_bytes=VMEM_LIMIT),
        name="proj",
    )(x2, g1, wlat, wu, wkr, bglu, qg, kvg, wuq, wukv, cos_t, s1_t, s2_t)


CONV_PAD = 16
CONV_ROWS = 128


def _attn_conv_kernel(q_ref, k_ref, v_ref, c_ref, w_ref, b_ref, o_ref, y_ref, xp_ref, sh_ref, *, tq, width):
    s_len = q_ref.shape[0]
    half = width // 2
    rows = min(CONV_ROWS, s_len)
    span = sh_ref.shape[1]
    zeros = jnp.zeros((CONV_PAD, LANES), F32)
    xp_ref[0:CONV_PAD, :] = zeros
    xp_ref[CONV_PAD + s_len:, :] = zeros
    xp_ref[CONV_PAD:CONV_PAD + s_len, :] = c_ref[...].astype(F32)

    def conv_chunk(ci):
        base = ci * rows
        xw = xp_ref[base:base + rows + 2 * CONV_PAD, :]
        for r in range(1, SUBLANES):
            sh_ref[r - 1] = xw[r:r + span, :]
        acc = jnp.zeros((rows, LANES), F32)
        for k in range(width):
            off = CONV_PAD - half + k
            r, a0 = off % SUBLANES, off - off % SUBLANES
            if r == 0:
                tap = xp_ref[base + a0:base + a0 + rows, :]
            else:
                tap = sh_ref[r - 1, a0:a0 + rows, :]
            acc = acc + tap * w_ref[k:k + 1, :]
        y_ref[base:base + rows, :] = (acc + b_ref[...]).astype(BF16)

    k = k_ref[...]
    v = v_ref[...]
    n_q = s_len // tq
    n_chunks = s_len // rows
    for j in range(n_q):
        qs = slice(j * tq, (j + 1) * tq)
        s = lax.dot_general(q_ref[qs, :], k, (((1,), (1,)), ((), ())), preferred_element_type=F32)
        m = jnp.max(s, axis=-1, keepdims=True)
        p = jnp.exp2(s - m).astype(BF16)
        o = _dot(p, v)
        o_ref[qs, :] = (o[:, :V_HEAD_DIM] / o[:, V_HEAD_DIM:V_HEAD_DIM + 1]).astype(BF16)
        for ci in range(j * n_chunks // n_q, (j + 1) * n_chunks // n_q):
            conv_chunk(ci)


def _attn_conv_call(q, k, v, c, w_dw, b_dw, *, batch, seq, tq):
    t, ch = c.shape
    width = w_dw.shape[0]
    assert width // 2 <= CONV_PAD and ch == MLA_HEADS * LANES
    rows = min(CONV_ROWS, seq)
    head = lambda b, h: (b, h)
    chan = lambda b, h: (0, h)
    return pl.pallas_call(
        functools.partial(_attn_conv_kernel, tq=tq, width=width),
        grid=(batch, MLA_HEADS),
        in_specs=[pl.BlockSpec((seq, HEAD_SLOT), head), pl.BlockSpec((seq, HEAD_SLOT), head),
                  pl.BlockSpec((seq, HEAD_SLOT), head), pl.BlockSpec((seq, LANES), head),
                  pl.BlockSpec((width, LANES), chan), pl.BlockSpec((1, LANES), chan)],
        out_specs=[pl.BlockSpec((seq, V_HEAD_DIM), head), pl.BlockSpec((seq, LANES), head)],
        out_shape=[jax.ShapeDtypeStruct((t, MLA_HEADS * V_HEAD_DIM), BF16), jax.ShapeDtypeStruct((t, ch), BF16)],
        scratch_shapes=[pltpu.VMEM((seq + 2 * CONV_PAD, LANES), F32),
                        pltpu.VMEM((SUBLANES - 1, rows + 2 * CONV_PAD - SUBLANES, LANES), F32)],
        compiler_params=pltpu.CompilerParams(
            dimension_semantics=("arbitrary", "arbitrary"), vmem_limit_bytes=VMEM_LIMIT),
        name="attn_conv",
    )(q, k, v, c, w_dw, b_dw)


def _oproj_kernel(a_ref, c_ref, x_ref, wo_ref, lg_ref, lb_ref, g2_ref, wr_ref, br_ref, h_out, lg_out):
    na = a_ref.shape[1]
    y = c_ref[...].astype(F32)
    yc = y - jnp.mean(y, axis=-1, keepdims=True)
    z = yc * lax.rsqrt(jnp.mean(yc * yc, axis=-1, keepdims=True) + EPS) * lg_ref[...] + lb_ref[...]
    act = (z * jax.nn.sigmoid(z)).astype(BF16)
    h = x_ref[...] + _dot(a_ref[...], wo_ref[:na, :]) + _dot(act, wo_ref[na:, :])
    h_out[...] = h
    hn = _rms(h, g2_ref[...])
    hi = hn.astype(BF16)
    lo = (hn - hi.astype(F32)).astype(BF16)
    r = _dot(hi, wr_ref[...])
    lg_out[...] = r[:, :LANES] + r[:, LANES:] + _dot(lo, wr_ref[:, :LANES]) + br_ref[...]


def _oproj_call(attn, conv, x2, wo, ln_g, ln_b, g2, wr, br, *, tm):
    t, d = x2.shape
    row = lambda i: (i, 0)
    return pl.pallas_call(
        _oproj_kernel,
        grid=(t // tm,),
        in_specs=[pl.BlockSpec((tm, attn.shape[1]), row), pl.BlockSpec((tm, conv.shape[1]), row),
                  pl.BlockSpec((tm, d), row), _whole(wo.shape, True), _whole(ln_g.shape), _whole(ln_b.shape),
                  _whole(g2.shape), _whole(wr.shape, True), _whole(br.shape)],
        out_specs=[pl.BlockSpec((tm, d), row), pl.BlockSpec((tm, LANES), row)],
        out_shape=[jax.ShapeDtypeStruct((t, d), F32), jax.ShapeDtypeStruct((t, LANES), F32)],
        compiler_params=pltpu.CompilerParams(dimension_semantics=("arbitrary",), vmem_limit_bytes=VMEM_LIMIT),
        name="oproj",
    )(attn, conv, x2, wo, ln_g, ln_b, g2, wr, br)


def _route_rows(lg):
    lane = lax.broadcasted_iota(I32, lg.shape, 1)
    big = jnp.int32(1 << 20)
    neg = jnp.float32(-jnp.inf)
    is_g = (lane >= N_EXPERTS) & (lane < N_EXPERTS + N_EXPERT_GROUPS)
    lgrp = jnp.where(is_g, lg, neg)
    gmax = jnp.max(lgrp, axis=-1, keepdims=True)
    gsel = jnp.min(jnp.where(lgrp == gmax, lane, big), axis=-1, keepdims=True) - N_EXPERTS
    p_g = 1.0 / jnp.sum(jnp.where(is_g, jnp.exp(lg - gmax), 0.0), axis=-1, keepdims=True)
    lo = gsel * EXPERTS_PER_GROUP
    in_grp = (lane >= lo) & (lane < lo + EXPERTS_PER_GROUP)
    le = jnp.where(in_grp, lg, neg)
    m1 = jnp.max(le, axis=-1, keepdims=True)
    i1 = jnp.min(jnp.where(le == m1, lane, big), axis=-1, keepdims=True)
    le2 = jnp.where(lane == i1, neg, le)
    m2 = jnp.max(le2, axis=-1, keepdims=True)
    i2 = jnp.min(jnp.where(le2 == m2, lane, big), axis=-1, keepdims=True)
    r = jnp.exp(m2 - m1)
    w1 = 1.0 / (1.0 + r)
    w2 = r / (1.0 + r)
    oh1 = (lane == i1).astype(F32)
    oh2 = (lane == i2).astype(F32)
    return oh1, oh2, p_g * w1, p_g * w2, i1, i2


def _lane_cumsum(v):
    lane = lax.broadcasted_iota(I32, v.shape, 1)
    sh = 1
    while sh < LANES:
        v = v + jnp.where(lane >= sh, pltpu.roll(v, sh, 1), 0)
        sh *= 2
    return v


def _route_kernel(lg_ref, d0_out, d1_out, gate_out, meta_out, idx_ref, *, rows_per_block):
    t = lg_ref.shape[0]
    n_chunks, _, rc = d0_out.shape
    shift = rows_per_block.bit_length() - 1
    lane_c = lax.broadcasted_iota(I32, (rc, LANES), 1)

    def two_cols(c0, c1, zero):
        return jnp.where(lane_c == 0, c0, jnp.where(lane_c == 1, c1, zero))

    def count_step(i, acc):
        base = pl.multiple_of(i * rc, rc)
        oh1, oh2, g1, g2, i1, i2 = _route_rows(lg_ref[pl.ds(base, rc), :])
        gate_out[pl.ds(base, rc), :] = two_cols(g1, g2, 0.0)
        idx_ref[pl.ds(base, rc), :] = two_cols(i1, i2, 0)
        return acc + jnp.sum(oh1 + oh2, axis=0, keepdims=True)

    counts_f = lax.fori_loop(0, n_chunks, count_step, jnp.zeros((1, LANES), F32))
    counts = jnp.broadcast_to(counts_f, (SUBLANES, LANES)).astype(I32)
    padded = ((counts + (rows_per_block - 1)) >> shift) << shift
    pad_end = _lane_cumsum(padded)
    pad_start = pad_end - padded

    r_i = lax.broadcasted_iota(I32, (rc, rc), 0)
    c_i = lax.broadcasted_iota(I32, (rc, rc), 1)
    tri = (r_i > c_i).astype(BF16)

    def dest_step(i, carry):
        base = pl.multiple_of(i * rc, rc)
        idx = idx_ref[pl.ds(base, rc), :]
        oh1 = (lane_c == idx[:, 0:1]).astype(F32)
        oh2 = (lane_c == idx[:, 1:2]).astype(F32)
        oh = oh1 + oh2
        pos = carry + _dot(tri, oh.astype(BF16))
        d1 = jnp.sum(oh1 * pos, axis=-1, keepdims=True)
        d2 = jnp.sum(oh2 * pos, axis=-1, keepdims=True)
        tr = jnp.transpose(two_cols(d1, d2, 0.0))
        d0_out[i] = tr[0:1, :].astype(I32)
        d1_out[i] = tr[1:2, :].astype(I32)
        return carry + jnp.sum(oh, axis=0, keepdims=True)

    lax.fori_loop(0, n_chunks, dest_step, pad_start[0:1, :].astype(F32))

    nbp = meta_out.shape[0]
    lane_b = lax.broadcasted_iota(I32, (nbp, LANES), 1)
    row0 = lax.broadcasted_iota(I32, (nbp, LANES), 0) * rows_per_block
    pe = jnp.broadcast_to(pad_end[0:1, :], (nbp, LANES))
    ps = jnp.broadcast_to(pad_start[0:1, :], (nbp, LANES))
    cn = jnp.broadcast_to(counts[0:1, :], (nbp, LANES))
    is_e = lane_b < N_EXPERTS
    total = jnp.max(pe, axis=-1, keepdims=True)
    n_used = total >> shift
    last_e = jnp.max(jnp.where(is_e & (cn > 0), lane_b, 0), axis=-1, keepdims=True)
    be = jnp.sum(jnp.where(is_e & (pe <= row0), 1, 0), axis=-1, keepdims=True)
    be = jnp.minimum(be, last_e)
    sel = lane_b == be
    cnt_b = jnp.sum(jnp.where(sel, cn, 0), axis=-1, keepdims=True)
    ps_b = jnp.sum(jnp.where(sel, ps, 0), axis=-1, keepdims=True)
    blk = row0[:, 0:1]
    used = blk < total
    nvalid = jnp.where(used, jnp.clip(cnt_b - (blk - ps_b), 0, rows_per_block), 0)
    meta_out[...] = jnp.where(lane_b == 0, be, jnp.where(lane_b == 1, nvalid, jnp.where(lane_b == 2, n_used, 0)))


def _route_call(logits, *, rows_per_block, n_blocks):
    t = logits.shape[0]
    rc = min(ROUTE_CHUNK, t)
    nbp = -(-n_blocks // SUBLANES) * SUBLANES
    dshape = (t // rc, 1, rc)
    d0, d1, gates, meta = pl.pallas_call(
        functools.partial(_route_kernel, rows_per_block=rows_per_block),
        in_specs=[_whole(logits.shape)],
        out_specs=[_whole(dshape), _whole(dshape), _whole((t, LANES)), _whole((nbp, LANES))],
        out_shape=[jax.ShapeDtypeStruct(dshape, I32), jax.ShapeDtypeStruct(dshape, I32),
                   jax.ShapeDtypeStruct((t, LANES), F32), jax.ShapeDtypeStruct((nbp, LANES), I32)],
        grid=(1,),
        scratch_shapes=[pltpu.VMEM((t, LANES), I32)],
        compiler_params=pltpu.CompilerParams(dimension_semantics=("arbitrary",), vmem_limit_bytes=VMEM_LIMIT),
        name="route",
    )(logits)
    return d0.reshape(t), d1.reshape(t), gates, meta


def _pow2_chunks(limit):
    c = 1 << (limit.bit_length() - 1)
    while c >= 1:
        yield c
        c >>= 1


def _expert_kernel(be_ref, nv_ref, nused_ref, d0_ref, d1_ref,
                   h_hbm, g2_ref, wg_hbm, wu_hbm, wd_hbm, y_ref,
                   tok_ref, ord_ref, exp_ref, nexp_ref, xbuf, wg_buf, wu_buf, wd_buf, gsem, wsem):
    b = pl.program_id(0)
    n_used = nused_ref[0]
    groups = xbuf.shape[1]
    rows = groups * SUBLANES
    d = xbuf.shape[3]
    n_tok = d0_ref.shape[0]
    sub_shift = SUBLANES.bit_length() - 1
    n_slots = wg_buf.shape[0]

    def weight_copies(j):
        e = exp_ref[j]
        slot = lax.rem(j, n_slots)
        return [pltpu.make_async_copy(src.at[e], dst.at[slot], wsem.at[slot])
                for src, dst in ((wg_hbm, wg_buf), (wu_hbm, wu_buf), (wd_hbm, wd_buf))]

    def start_weights(j):
        for c in weight_copies(j):
            c.start()

    def wait_weights(j):
        for c in weight_copies(j):
            c.wait()

    def gather_copy(blk, slot, g, u):
        tok = tok_ref[blk * rows + g * SUBLANES + u]
        return pltpu.make_async_copy(h_hbm.at[pl.ds(tok, 1)], xbuf.at[slot, g, pl.ds(u, 1)], gsem.at[slot])

    def start_gather(blk, slot):
        n = nv_ref[blk]

        def group(g, c):
            for u in range(SUBLANES):
                gather_copy(blk, slot, g, u).start()
            return c
        full = lax.shift_right_logical(n, sub_shift)
        lax.fori_loop(0, full, group, 0)
        for u in range(SUBLANES - 1):
            @pl.when(full * SUBLANES + u < n)
            def _():
                gather_copy(blk, slot, full, u).start()

    def wait_gather(blk, slot):
        n = nv_ref[blk]
        buf = xbuf.at[slot]
        for c in _pow2_chunks(rows):
            @pl.when((n & c) != 0)
            def _():
                if c >= SUBLANES:
                    part = buf.at[pl.ds(0, c // SUBLANES)]
                else:
                    part = buf.at[0, pl.ds(0, c)]
                pltpu.make_async_copy(part, part, gsem.at[slot]).wait()

    @pl.when(b == 0)
    def _():
        def inv(g, c):
            for u in range(DMA_UNROLL):
                tk = g * DMA_UNROLL + u
                tok_ref[d0_ref[tk]] = tk
                tok_ref[d1_ref[tk]] = tk
            return c
        lax.fori_loop(0, n_tok // DMA_UNROLL, inv, 0)

        def scan(blk, j):
            e = be_ref[blk]
            is_new = jnp.logical_or(blk == 0, e != be_ref[jnp.maximum(blk - 1, 0)])
            j = j + is_new.astype(I32)
            ord_ref[blk] = j - 1

            @pl.when(is_new)
            def _():
                exp_ref[j - 1] = e
            return j
        n_exp = lax.fori_loop(0, n_used, scan, jnp.int32(0))
        nexp_ref[0] = n_exp
        for j in range(n_slots - 1):
            @pl.when(j < n_exp)
            def _():
                start_weights(j)
        xbuf[...] = jnp.zeros(xbuf.shape, F32)
        start_gather(0, 0)

    @pl.when(b >= n_used)
    def _():
        y_ref[...] = jnp.zeros(y_ref.shape, F32)

    @pl.when(b < n_used)
    def _():
        slot = b & 1
        wait_gather(b, slot)

        @pl.when(b + 1 < n_used)
        def _():
            start_gather(b + 1, 1 - slot)

        j = ord_ref[b]

        @pl.when(jnp.logical_or(b == 0, ord_ref[jnp.maximum(b - 1, 0)] != j))
        def _():
            wait_weights(j)

            @pl.when(j + (n_slots - 1) < nexp_ref[0])
            def _():
                start_weights(j + (n_slots - 1))

        ws = lax.rem(j, n_slots)
        hn = _rms(xbuf[slot].reshape(rows, d), g2_ref[...]).astype(BF16)
        gate = _dot(hn, wg_buf[ws].astype(BF16))
        up = _dot(hn, wu_buf[ws].astype(BF16))
        hmid = (gate * jax.nn.sigmoid(gate) * up).astype(BF16)
        y_ref[...] = _dot(hmid, wd_buf[ws].astype(BF16))


def _expert_call(be, nvalid, n_used, dest0, dest1, h, g2, w_gate, w_up, w_down, *, rows_per_block, n_blocks):
    t, d = h.shape
    _, _, f = w_gate.shape
    assert t % DMA_UNROLL == 0 and rows_per_block % SUBLANES == 0
    groups = rows_per_block // SUBLANES
    n_exp = w_gate.shape[0]
    ymap = lambda b, *_: (b, 0)
    hbm = pl.BlockSpec(memory_space=pl.ANY)
    grid_spec = pltpu.PrefetchScalarGridSpec(
        num_scalar_prefetch=5,
        grid=(n_blocks,),
        in_specs=[hbm, pl.BlockSpec(g2.shape, lambda b, *_: (0, 0)), hbm, hbm, hbm],
        out_specs=pl.BlockSpec((rows_per_block, d), ymap),
        scratch_shapes=[
            pltpu.SMEM((n_blocks * rows_per_block,), I32),
            pltpu.SMEM((n_blocks,), I32), pltpu.SMEM((n_exp,), I32), pltpu.SMEM((1,), I32),
            pltpu.VMEM((2, groups, SUBLANES, d), F32),
            pltpu.VMEM((EXPERT_WEIGHT_SLOTS, d, f), F32), pltpu.VMEM((EXPERT_WEIGHT_SLOTS, d, f), F32),
            pltpu.VMEM((EXPERT_WEIGHT_SLOTS, f, d), F32),
            pltpu.SemaphoreType.DMA((2,)), pltpu.SemaphoreType.DMA((EXPERT_WEIGHT_SLOTS,)),
        ],
    )
    return pl.pallas_call(
        _expert_kernel,
        grid_spec=grid_spec,
        out_shape=jax.ShapeDtypeStruct((n_blocks * rows_per_block, d), F32),
        compiler_params=pltpu.CompilerParams(dimension_semantics=("arbitrary",), vmem_limit_bytes=VMEM_LIMIT),
        name="experts",
    )(be, nvalid, n_used, dest0, dest1, h, g2, w_gate, w_up, w_down)


def _final_kernel(d0_ref, d1_ref, h_ref, y_hbm, gate_ref, fg_ref, o_ref, ybuf, sem):
    i = pl.program_id(0)
    groups = ybuf.shape[2]
    tm = groups * SUBLANES
    d = ybuf.shape[4]

    def start_tile(tile, slot):
        def group(g, c):
            for u in range(SUBLANES):
                tk = tile * tm + g * SUBLANES + u
                for k, dref in enumerate((d0_ref, d1_ref)):
                    pltpu.make_async_copy(y_hbm.at[pl.ds(dref[tk], 1)], ybuf.at[slot, k, g, pl.ds(u, 1)],
                                          sem.at[slot]).start()
            return c
        lax.fori_loop(0, groups, group, 0)

    @pl.when(i == 0)
    def _():
        start_tile(0, 0)

    slot = i & 1

    @pl.when(i + 1 < pl.num_programs(0))
    def _():
        start_tile(i + 1, 1 - slot)

    pltpu.make_async_copy(ybuf.at[slot], ybuf.at[slot], sem.at[slot]).wait()
    y0 = ybuf[slot, 0].reshape(tm, d)
    y1 = ybuf[slot, 1].reshape(tm, d)
    out = h_ref[...] + gate_ref[:, 0:1] * y0 + gate_ref[:, 1:2] * y1
    o_ref[...] = _rms(out, fg_ref[...])


def _final_call(dest0, dest1, h, y, gates, fg, *, tm):
    t, d = h.shape
    assert tm % SUBLANES == 0
    row = lambda i, *_: (i, 0)
    grid_spec = pltpu.PrefetchScalarGridSpec(
        num_scalar_prefetch=2,
        grid=(t // tm,),
        in_specs=[pl.BlockSpec((tm, d), row), pl.BlockSpec(memory_space=pl.ANY),
                  pl.BlockSpec((tm, LANES), row), pl.BlockSpec(fg.shape, lambda i, *_: (0, 0))],
        out_specs=pl.BlockSpec((tm, d), row),
        scratch_shapes=[pltpu.VMEM((2, 2, tm // SUBLANES, SUBLANES, d), F32), pltpu.SemaphoreType.DMA((2,))],
    )
    return pl.pallas_call(
        _final_kernel,
        grid_spec=grid_spec,
        out_shape=jax.ShapeDtypeStruct((t, d), F32),
        compiler_params=pltpu.CompilerParams(dimension_semantics=("arbitrary",), vmem_limit_bytes=VMEM_LIMIT),
        name="final",
    )(dest0, dest1, h, y, gates, fg)


def _rope_tables(seq):
    pos = jnp.arange(seq, dtype=F32)
    inv_freq = ROPE_THETA ** (-jnp.arange(0, QK_ROPE_DIM, 2, dtype=F32) / QK_ROPE_DIM)
    ang = pos[:, None] * inv_freq[None, :]
    cos, sin = jnp.cos(ang), jnp.sin(ang)
    zero = jnp.zeros_like(sin)
    cos_t = jnp.concatenate([cos, cos, cos, cos], axis=1)
    s1_t = jnp.concatenate([zero, sin, zero, sin], axis=1)
    s2_t = jnp.concatenate([-sin, zero, -sin, zero], axis=1)
    return cos_t, s1_t, s2_t


def kernel(x, ln1_g, w_in, b_glu, q_norm_g, w_uq, kv_norm_g, w_ukv, w_dw, b_dw, conv_ln_g, conv_ln_b,
           w_o, ln2_g, w_group, b_group, w_router, b_router, w_gate, w_up, w_down, final_g):
    batch, seq, d = x.shape
    assert ln1_g.shape[0] == 1, "single-layer trunk"
    t = batch * seq
    q_rank = q_norm_g.shape[1]
    kv_rank = kv_norm_g.shape[1]
    x2 = x.reshape(t, d)

    wi = w_in[0]
    o_kr = q_rank + kv_rank
    o_u = o_kr + QK_ROPE_DIM
    wlat = wi[:, :o_kr].astype(BF16)
    wu = wi[:, o_u:].astype(BF16)
    wkr = jnp.pad(wi[:, o_kr:o_u], ((0, 0), (0, LANES - QK_ROPE_DIM))).astype(BF16)
    wuq = w_uq[0].reshape(q_rank, MLA_HEADS, QK_NOPE_DIM + QK_ROPE_DIM)
    wuq = jnp.pad(wuq, ((0, 0), (0, 0), (0, HEAD_SLOT - QK_NOPE_DIM - QK_ROPE_DIM)))
    wuq = wuq.reshape(q_rank, MLA_HEADS * HEAD_SLOT).astype(BF16)
    wukv = w_ukv[0].astype(BF16)
    wo = w_o[0].astype(BF16)
    wr = jnp.concatenate([w_router[0], w_group[0],
                          jnp.zeros((d, LANES - N_EXPERTS - N_EXPERT_GROUPS), F32)], axis=1)
    wr_hi = wr.astype(BF16)
    wr_lo = (wr - wr_hi.astype(F32)).astype(BF16)
    wr2 = jnp.concatenate([wr_hi, wr_lo], axis=1)
    br = jnp.concatenate([b_router[0], b_group[0],
                          jnp.zeros((LANES - N_EXPERTS - N_EXPERT_GROUPS,), F32)])[None, :]
    cos_t, s1_t, s2_t = _rope_tables(seq)

    tm = min(512, seq)
    q, k, v, c = _proj_call(x2, ln1_g, wlat, wu, wkr, b_glu, q_norm_g, kv_norm_g, wuq, wukv, cos_t, s1_t, s2_t,
                            seq=seq, tm=tm)
    attn, conv = _attn_conv_call(q, k, v, c, w_dw[0], b_dw, batch=batch, seq=seq, tq=min(256, seq))
    h, logits = _oproj_call(attn, conv, x2, wo, conv_ln_g, conv_ln_b, ln2_g, wr2, br, tm=tm)

    n_blocks = -(-(2 * t + N_EXPERTS * (MOE_ROWS - 1)) // MOE_ROWS)
    dest0, dest1, gates, meta = _route_call(logits, rows_per_block=MOE_ROWS, n_blocks=n_blocks)
    y = _expert_call(meta[:n_blocks, 0], meta[:n_blocks, 1], meta[0:1, 2],
                     dest0, dest1, h, ln2_g, w_gate[0], w_up[0], w_down[0],
                     rows_per_block=MOE_ROWS, n_blocks=n_blocks)
    out = _final_call(dest0, dest1, h, y, gates, final_g[None, :], tm=min(256, seq))
    return out.reshape(batch, seq, d)
```

```python
import functools

import jax
import jax.numpy as jnp
import numpy as np
from jax import lax
from jax.experimental import pallas as pl
from jax.experimental.pallas import tpu as pltpu

F32 = jnp.float32
BF16 = jnp.bfloat16
I32 = jnp.int32

MLA_HEADS = 8
QK_NOPE_DIM = 128
QK_ROPE_DIM = 64
V_HEAD_DIM = 128
ROPE_THETA = 10000.0
N_EXPERT_GROUPS = 8
EXPERTS_PER_GROUP = 8
N_EXPERTS = N_EXPERT_GROUPS * EXPERTS_PER_GROUP
EPS = 1e-6
LOG2E = 1.4426950408889634

LANES = 128
SUBLANES = 8
HEAD_SLOT = 2 * LANES
ROPE_HALF = QK_ROPE_DIM // 2
VMEM_LIMIT = 56 * 1024 * 1024

MOE_ROWS = 256
ROUTE_UNROLL = 4
EXPERT_WEIGHT_SLOTS = 3
DMA_UNROLL = 8
DMA_UNROLL_LOG2 = DMA_UNROLL.bit_length() - 1


def _rms(x, g):
    return x * lax.rsqrt(jnp.mean(x * x, axis=-1, keepdims=True) + EPS) * g


def _dot(a, b):
    return jnp.dot(a, b, preferred_element_type=F32)


def _whole(shape, single=False):
    mode = dict(pipeline_mode=pl.Buffered(1)) if single else {}
    return pl.BlockSpec(shape, lambda *_: (0,) * len(shape), **mode)


def _proj_kernel(x_ref, g1_ref, wlat_ref, wu_ref, wkr_ref, bglu_ref, qg_ref, kvg_ref, wuq_ref, wukv_ref,
                 cos_ref, s1_ref, s2_ref, q_out, k_out, v_out, c_out, *, q_rank, kv_rank, conv_ch, q_scale):
    xn = _rms(x_ref[...], g1_ref[...]).astype(BF16)
    cos = cos_ref[...]
    s1 = s1_ref[...]
    s2 = s2_ref[...]

    def rope(t):
        return t * cos + pltpu.roll(t, ROPE_HALF, 1) * s1 + pltpu.roll(t, LANES - ROPE_HALF, 1) * s2

    a = _dot(xn, wu_ref[:, :conv_ch]) + bglu_ref[:, :conv_ch]
    gate = _dot(xn, wu_ref[:, conv_ch:]) + bglu_ref[:, conv_ch:]
    c_out[...] = (a * jax.nn.sigmoid(gate)).astype(BF16)

    kr = rope(_dot(xn, wkr_ref[...])).astype(BF16)
    qn = _rms(_dot(xn, wlat_ref[:, :q_rank]), qg_ref[...]).astype(BF16)
    kvn = _rms(_dot(xn, wlat_ref[:, q_rank:q_rank + kv_rank]), kvg_ref[...]).astype(BF16)
    ones_blk = (lax.broadcasted_iota(I32, (x_ref.shape[0], LANES), 1) == 0).astype(BF16)
    for h in range(MLA_HEADS):
        c0 = h * HEAD_SLOT
        qh = _dot(qn, wuq_ref[:, c0:c0 + HEAD_SLOT])
        q_out[:, c0:c0 + LANES] = (qh[:, :LANES] * q_scale).astype(BF16)
        q_out[:, c0 + LANES:c0 + HEAD_SLOT] = (rope(qh[:, LANES:]) * q_scale).astype(BF16)
        kvh = _dot(kvn, wukv_ref[:, c0:c0 + HEAD_SLOT].astype(BF16))
        k_out[:, c0:c0 + LANES] = kvh[:, :LANES].astype(BF16)
        k_out[:, c0 + LANES:c0 + HEAD_SLOT] = kr
        v_out[:, c0:c0 + LANES] = kvh[:, LANES:].astype(BF16)
        v_out[:, c0 + LANES:c0 + HEAD_SLOT] = ones_blk


def _proj_call(x2, g1, wlat, wu, wkr, bglu, qg, kvg, wuq, wukv, cos_t, s1_t, s2_t, *, seq, tm):
    t, d = x2.shape
    q_rank, kv_rank = qg.shape[1], kvg.shape[1]
    conv_ch = bglu.shape[1] // 2
    n_pos = seq // tm
    row = lambda i: (i, 0)
    pos = lambda i: (i % n_pos, 0)
    q_scale = float(QK_NOPE_DIM + QK_ROPE_DIM) ** -0.5 * LOG2E
    kern = functools.partial(_proj_kernel, q_rank=q_rank, kv_rank=kv_rank, conv_ch=conv_ch, q_scale=q_scale)
    slot_w = MLA_HEADS * HEAD_SLOT
    return pl.pallas_call(
        kern,
        grid=(t // tm,),
        in_specs=[
            pl.BlockSpec((tm, d), row), _whole(g1.shape), _whole(wlat.shape, True), _whole(wu.shape, True),
            _whole(wkr.shape, True), _whole(bglu.shape),
            _whole(qg.shape), _whole(kvg.shape), _whole(wuq.shape, True), _whole(wukv.shape, True),
            pl.BlockSpec((tm, LANES), pos), pl.BlockSpec((tm, LANES), pos), pl.BlockSpec((tm, LANES), pos),
        ],
        out_specs=[
            pl.BlockSpec((tm, slot_w), row), pl.BlockSpec((tm, slot_w), row),
            pl.BlockSpec((tm, slot_w), row), pl.BlockSpec((tm, conv_ch), row),
        ],
        out_shape=[
            jax.ShapeDtypeStruct((t, slot_w), BF16), jax.ShapeDtypeStruct((t, slot_w), BF16),
            jax.ShapeDtypeStruct((t, slot_w), BF16), jax.ShapeDtypeStruct((t, conv_ch), BF16),
        ],
        compiler_params=pltpu.CompilerParams(dimension_semantics=("arbitrary",), vmem_limit_bytes=VMEM_LIMIT),
        name="proj",
    )(x2, g1, wlat, wu, wkr, bglu, qg, kvg, wuq, wukv, cos_t, s1_t, s2_t)


CONV_PAD = 16
CONV_ROWS = 128


def _attn_conv_kernel(q_ref, k_ref, v_ref, c_ref, w_ref, b_ref, o_ref, y_ref, xp_ref, sh_ref, *, tq, width):
    s_len = q_ref.shape[0]
    half = width // 2
    rows = min(CONV_ROWS, s_len)
    span = sh_ref.shape[1]
    zeros = jnp.zeros((CONV_PAD, LANES), F32)
    xp_ref[0:CONV_PAD, :] = zeros
    xp_ref[CONV_PAD + s_len:, :] = zeros
    xp_ref[CONV_PAD:CONV_PAD + s_len, :] = c_ref[...].astype(F32)

    def conv_chunk(ci):
        base = ci * rows
        xw = xp_ref[base:base + rows + 2 * CONV_PAD, :]
        for r in range(1, SUBLANES):
            sh_ref[r - 1] = xw[r:r + span, :]
        acc = jnp.zeros((rows, LANES), F32)
        for k in range(width):
            off = CONV_PAD - half + k
            r, a0 = off % SUBLANES, off - off % SUBLANES
            if r == 0:
                tap = xp_ref[base + a0:base + a0 + rows, :]
            else:
                tap = sh_ref[r - 1, a0:a0 + rows, :]
            acc = acc + tap * w_ref[k:k + 1, :]
        y_ref[base:base + rows, :] = (acc + b_ref[...]).astype(BF16)

    k = k_ref[...]
    v = v_ref[...]
    n_q = s_len // tq
    n_chunks = s_len // rows
    for j in range(n_q):
        qs = slice(j * tq, (j + 1) * tq)
        s = lax.dot_general(q_ref[qs, :], k, (((1,), (1,)), ((), ())), preferred_element_type=F32)
        m = jnp.max(s, axis=-1, keepdims=True)
        p = jnp.exp2(s - m).astype(BF16)
        o = _dot(p, v)
        o_ref[qs, :] = (o[:, :V_HEAD_DIM] / o[:, V_HEAD_DIM:V_HEAD_DIM + 1]).astype(BF16)
        for ci in range(j * n_chunks // n_q, (j + 1) * n_chunks // n_q):
            conv_chunk(ci)


def _attn_conv_call(q, k, v, c, w_dw, b_dw, *, batch, seq, tq):
    t, ch = c.shape
    width = w_dw.shape[0]
    assert width // 2 <= CONV_PAD and ch == MLA_HEADS * LANES
    rows = min(CONV_ROWS, seq)
    head = lambda b, h: (b, h)
    chan = lambda b, h: (0, h)
    return pl.pallas_call(
        functools.partial(_attn_conv_kernel, tq=tq, width=width),
        grid=(batch, MLA_HEADS),
        in_specs=[pl.BlockSpec((seq, HEAD_SLOT), head), pl.BlockSpec((seq, HEAD_SLOT), head),
                  pl.BlockSpec((seq, HEAD_SLOT), head), pl.BlockSpec((seq, LANES), head),
                  pl.BlockSpec((width, LANES), chan), pl.BlockSpec((1, LANES), chan)],
        out_specs=[pl.BlockSpec((seq, V_HEAD_DIM), head), pl.BlockSpec((seq, LANES), head)],
        out_shape=[jax.ShapeDtypeStruct((t, MLA_HEADS * V_HEAD_DIM), BF16), jax.ShapeDtypeStruct((t, ch), BF16)],
        scratch_shapes=[pltpu.VMEM((seq + 2 * CONV_PAD, LANES), F32),
                        pltpu.VMEM((SUBLANES - 1, rows + 2 * CONV_PAD - SUBLANES, LANES), F32)],
        compiler_params=pltpu.CompilerParams(
            dimension_semantics=("arbitrary", "arbitrary"), vmem_limit_bytes=VMEM_LIMIT),
        name="attn_conv",
    )(q, k, v, c, w_dw, b_dw)


def _oproj_kernel(a_ref, c_ref, x_ref, wo_ref, lg_ref, lb_ref, g2_ref, wr_ref, br_ref, h_out, lg_out):
    na = a_ref.shape[1]
    y = c_ref[...].astype(F32)
    yc = y - jnp.mean(y, axis=-1, keepdims=True)
    z = yc * lax.rsqrt(jnp.mean(yc * yc, axis=-1, keepdims=True) + EPS) * lg_ref[...] + lb_ref[...]
    act = (z * jax.nn.sigmoid(z)).astype(BF16)
    h = (x_ref[...] + _dot(a_ref[...], wo_ref[:na, :].astype(BF16))
         + _dot(act, wo_ref[na:, :].astype(BF16)))
    h_out[...] = h
    hn = _rms(h, g2_ref[...])
    hi = hn.astype(BF16)
    lo = (hn - hi.astype(F32)).astype(BF16)
    r = _dot(hi, wr_ref[...])
    lg_out[...] = r[:, :LANES] + r[:, LANES:] + _dot(lo, wr_ref[:, :LANES]) + br_ref[...]


def _oproj_call(attn, conv, x2, wo, ln_g, ln_b, g2, wr, br, *, tm):
    t, d = x2.shape
    row = lambda i: (i, 0)
    return pl.pallas_call(
        _oproj_kernel,
        grid=(t // tm,),
        in_specs=[pl.BlockSpec((tm, attn.shape[1]), row), pl.BlockSpec((tm, conv.shape[1]), row),
                  pl.BlockSpec((tm, d), row), _whole(wo.shape, True), _whole(ln_g.shape), _whole(ln_b.shape),
                  _whole(g2.shape), _whole(wr.shape, True), _whole(br.shape)],
        out_specs=[pl.BlockSpec((tm, d), row), pl.BlockSpec((tm, LANES), row)],
        out_shape=[jax.ShapeDtypeStruct((t, d), F32), jax.ShapeDtypeStruct((t, LANES), F32)],
        compiler_params=pltpu.CompilerParams(dimension_semantics=("arbitrary",), vmem_limit_bytes=VMEM_LIMIT),
        name="oproj",
    )(attn, conv, x2, wo, ln_g, ln_b, g2, wr, br)


def _route_tokens_on_lanes(lt):
    shape = (SUBLANES, LANES)
    row = lax.broadcasted_iota(I32, shape, 0)
    big = jnp.int32(1 << 20)
    neg = jnp.float32(-jnp.inf)

    def top(v):
        m = jnp.max(v, axis=0, keepdims=True)
        return m, jnp.min(jnp.where(v == m, row, big), axis=0, keepdims=True)

    lgrp = lt[N_EXPERTS:N_EXPERTS + N_EXPERT_GROUPS, :]
    gmax, gsel = top(lgrp)
    p_g = 1.0 / jnp.sum(jnp.exp(lgrp - gmax), axis=0, keepdims=True)
    le = jnp.zeros(shape, F32)
    for g in range(N_EXPERT_GROUPS):
        le = jnp.where(gsel == g, lt[g * EXPERTS_PER_GROUP:(g + 1) * EXPERTS_PER_GROUP, :], le)
    m1, i1 = top(le)
    m2, i2 = top(jnp.where(row == i1, neg, le))
    r = jnp.exp(m2 - m1)
    w1 = 1.0 / (1.0 + r)
    w2 = r / (1.0 + r)
    base = gsel * EXPERTS_PER_GROUP
    return base + i1, base + i2, p_g * w1, p_g * w2


def _lane_cumsum(v):
    lane = lax.broadcasted_iota(I32, v.shape, 1)
    sh = 1
    while sh < LANES:
        v = v + jnp.where(lane >= sh, pltpu.roll(v, sh, 1), 0)
        sh *= 2
    return v


def _route_kernel(lg_ref, d0_out, d1_out, gate_out, meta_out, e_ref, *, rows_per_block):
    n_chunks = d0_out.shape[0]
    shift = rows_per_block.bit_length() - 1
    sq = (LANES, LANES)
    row = lax.broadcasted_iota(I32, sq, 0)
    row8 = lax.broadcasted_iota(I32, (SUBLANES, LANES), 0)

    def one_hots(e1, e2):
        return (row == e1).astype(F32), (row == e2).astype(F32)

    def count_step(i, cnt):
        base = pl.multiple_of(i * LANES, LANES)
        e1, e2, g1, g2 = _route_tokens_on_lanes(jnp.transpose(lg_ref[pl.ds(base, LANES), :]))
        e_ref[i] = jnp.where(row8 == 0, e1, jnp.where(row8 == 1, e2, 0))
        gate_out[pl.ds(base, LANES), :] = jnp.transpose(jnp.where(row == 0, g1, jnp.where(row == 1, g2, 0.0)))
        oh1, oh2 = one_hots(e1, e2)
        return cnt + jnp.sum(oh1 + oh2, axis=1, keepdims=True)

    unroll = ROUTE_UNROLL if n_chunks % ROUTE_UNROLL == 0 else 1
    counts_col = lax.fori_loop(0, n_chunks, count_step, jnp.zeros((LANES, 1), F32), unroll=unroll)
    counts = jnp.transpose(jnp.broadcast_to(counts_col, sq))[0:SUBLANES, :].astype(I32)
    padded = ((counts + (rows_per_block - 1)) >> shift) << shift
    pad_end = _lane_cumsum(padded)
    pad_start = pad_end - padded
    start_col = jnp.transpose(jnp.broadcast_to(pad_start[0:1, :].astype(F32), sq))[:, 0:1]

    tri = (row < lax.broadcasted_iota(I32, sq, 1)).astype(BF16)

    def dest_step(i, carry):
        er = e_ref[i]
        oh1, oh2 = one_hots(er[0:1, :], er[1:2, :])
        oh = oh1 + oh2
        pos = carry + _dot(oh.astype(BF16), tri)
        d0_out[i] = jnp.sum(oh1 * pos, axis=0, keepdims=True).astype(I32)
        d1_out[i] = jnp.sum(oh2 * pos, axis=0, keepdims=True).astype(I32)
        return carry + jnp.sum(oh, axis=1, keepdims=True)

    lax.fori_loop(0, n_chunks, dest_step, start_col, unroll=unroll)

    nbp = meta_out.shape[0]
    lane_b = lax.broadcasted_iota(I32, (nbp, LANES), 1)
    row0 = lax.broadcasted_iota(I32, (nbp, LANES), 0) * rows_per_block
    pe = jnp.broadcast_to(pad_end[0:1, :], (nbp, LANES))
    ps = jnp.broadcast_to(pad_start[0:1, :], (nbp, LANES))
    cn = jnp.broadcast_to(counts[0:1, :], (nbp, LANES))
    is_e = lane_b < N_EXPERTS
    total = jnp.max(pe, axis=-1, keepdims=True)
    n_used = total >> shift
    last_e = jnp.max(jnp.where(is_e & (cn > 0), lane_b, 0), axis=-1, keepdims=True)
    be = jnp.sum(jnp.where(is_e & (pe <= row0), 1, 0), axis=-1, keepdims=True)
    be = jnp.minimum(be, last_e)
    sel = lane_b == be
    cnt_b = jnp.sum(jnp.where(sel, cn, 0), axis=-1, keepdims=True)
    ps_b = jnp.sum(jnp.where(sel, ps, 0), axis=-1, keepdims=True)
    blk = row0[:, 0:1]
    used = blk < total
    nvalid = jnp.where(used, jnp.clip(cnt_b - (blk - ps_b), 0, rows_per_block), 0)
    meta_out[...] = jnp.where(lane_b == 0, be, jnp.where(lane_b == 1, nvalid, jnp.where(lane_b == 2, n_used, 0)))


def _route_call(logits, *, rows_per_block, n_blocks):
    t = logits.shape[0]
    assert t % LANES == 0
    nbp = -(-n_blocks // SUBLANES) * SUBLANES
    dshape = (t // LANES, 1, LANES)
    d0, d1, gates, meta = pl.pallas_call(
        functools.partial(_route_kernel, rows_per_block=rows_per_block),
        in_specs=[_whole(logits.shape)],
        out_specs=[_whole(dshape), _whole(dshape), _whole((t, LANES)), _whole((nbp, LANES))],
        out_shape=[jax.ShapeDtypeStruct(dshape, I32), jax.ShapeDtypeStruct(dshape, I32),
                   jax.ShapeDtypeStruct((t, LANES), F32), jax.ShapeDtypeStruct((nbp, LANES), I32)],
        grid=(1,),
        scratch_shapes=[pltpu.VMEM((t // LANES, SUBLANES, LANES), I32)],
        compiler_params=pltpu.CompilerParams(dimension_semantics=("arbitrary",), vmem_limit_bytes=VMEM_LIMIT),
        name="route",
    )(logits)
    return d0.reshape(t), d1.reshape(t), gates, meta


def _pow2_chunks(limit):
    c = 1 << (limit.bit_length() - 1)
    while c >= 1:
        yield c
        c >>= 1


def _expert_kernel(be_ref, nv_ref, nused_ref, d0_ref, d1_ref,
                   h_hbm, g2_ref, wg_hbm, wu_hbm, wd_hbm, y_ref,
                   tok_ref, ord_ref, exp_ref, nexp_ref, xbuf, wg_buf, wu_buf, wd_buf, gsem, wsem):
    b = pl.program_id(0)
    n_used = nused_ref[0]
    groups = xbuf.shape[1]
    rows = groups * SUBLANES
    d = xbuf.shape[3]
    n_tok = d0_ref.shape[0]
    sub_shift = SUBLANES.bit_length() - 1
    n_slots = wg_buf.shape[0]

    def weight_copies(j):
        e = exp_ref[j]
        slot = lax.rem(j, n_slots)
        return [pltpu.make_async_copy(src.at[e], dst.at[slot], wsem.at[slot])
                for src, dst in ((wg_hbm, wg_buf), (wu_hbm, wu_buf), (wd_hbm, wd_buf))]

    def start_weights(j):
        for c in weight_copies(j):
            c.start()

    def wait_weights(j):
        for c in weight_copies(j):
            c.wait()

    def gather_copy(blk, slot, g, u):
        tok = tok_ref[blk * rows + g * SUBLANES + u]
        return pltpu.make_async_copy(h_hbm.at[pl.ds(tok, 1)], xbuf.at[slot, g, pl.ds(u, 1)], gsem.at[slot])

    def start_gather(blk, slot):
        n = nv_ref[blk]

        def group(g, c):
            for u in range(SUBLANES):
                gather_copy(blk, slot, g, u).start()
            return c
        full = lax.shift_right_logical(n, sub_shift)
        lax.fori_loop(0, full, group, 0)
        for u in range(SUBLANES - 1):
            @pl.when(full * SUBLANES + u < n)
            def _():
                gather_copy(blk, slot, full, u).start()

    def wait_gather(blk, slot):
        n = nv_ref[blk]
        buf = xbuf.at[slot]
        for c in _pow2_chunks(rows):
            @pl.when((n & c) != 0)
            def _():
                if c >= SUBLANES:
                    part = buf.at[pl.ds(0, c // SUBLANES)]
                else:
                    part = buf.at[0, pl.ds(0, c)]
                pltpu.make_async_copy(part, part, gsem.at[slot]).wait()

    @pl.when(b == 0)
    def _():
        def scan(blk, j):
            e = be_ref[blk]
            is_new = jnp.logical_or(blk == 0, e != be_ref[jnp.maximum(blk - 1, 0)])
            j = j + is_new.astype(I32)
            ord_ref[blk] = j - 1

            @pl.when(is_new)
            def _():
                exp_ref[j - 1] = e
            return j
        n_exp = lax.fori_loop(0, n_used, scan, jnp.int32(0))
        nexp_ref[0] = n_exp
        for j in range(n_slots - 1):
            @pl.when(j < n_exp)
            def _():
                start_weights(j)

        def inv(g, c):
            for u in range(DMA_UNROLL):
                tk = g * DMA_UNROLL + u
                tok_ref[d0_ref[tk]] = tk
                tok_ref[d1_ref[tk]] = tk
            return c
        lax.fori_loop(0, n_tok // DMA_UNROLL, inv, 0)
        xbuf[...] = jnp.zeros(xbuf.shape, F32)
        start_gather(0, 0)

    @pl.when(b >= n_used)
    def _():
        y_ref[...] = jnp.zeros(y_ref.shape, F32)

    @pl.when(b < n_used)
    def _():
        slot = b & 1
        wait_gather(b, slot)

        @pl.when(b + 1 < n_used)
        def _():
            start_gather(b + 1, 1 - slot)

        j = ord_ref[b]

        @pl.when(jnp.logical_or(b == 0, ord_ref[jnp.maximum(b - 1, 0)] != j))
        def _():
            wait_weights(j)

            @pl.when(j + (n_slots - 1) < nexp_ref[0])
            def _():
                start_weights(j + (n_slots - 1))

        ws = lax.rem(j, n_slots)
        hn = _rms(xbuf[slot].reshape(rows, d), g2_ref[...]).astype(BF16)
        gate = _dot(hn, wg_buf[ws].astype(BF16))
        up = _dot(hn, wu_buf[ws].astype(BF16))
        hmid = (gate * jax.nn.sigmoid(gate) * up).astype(BF16)
        y_ref[...] = _dot(hmid, wd_buf[ws].astype(BF16))


def _expert_call(be, nvalid, n_used, dest0, dest1, h, g2, w_gate, w_up, w_down, *, rows_per_block, n_blocks):
    t, d = h.shape
    _, _, f = w_gate.shape
    assert t % DMA_UNROLL == 0 and rows_per_block % SUBLANES == 0
    groups = rows_per_block // SUBLANES
    n_exp = w_gate.shape[0]
    ymap = lambda b, *_: (b, 0)
    hbm = pl.BlockSpec(memory_space=pl.ANY)
    grid_spec = pltpu.PrefetchScalarGridSpec(
        num_scalar_prefetch=5,
        grid=(n_blocks,),
        in_specs=[hbm, pl.BlockSpec(g2.shape, lambda b, *_: (0, 0)), hbm, hbm, hbm],
        out_specs=pl.BlockSpec((rows_per_block, d), ymap),
        scratch_shapes=[
            pltpu.SMEM((n_blocks * rows_per_block,), I32),
            pltpu.SMEM((n_blocks,), I32), pltpu.SMEM((n_exp,), I32), pltpu.SMEM((1,), I32),
            pltpu.VMEM((2, groups, SUBLANES, d), F32),
            pltpu.VMEM((EXPERT_WEIGHT_SLOTS, d, f), F32), pltpu.VMEM((EXPERT_WEIGHT_SLOTS, d, f), F32),
            pltpu.VMEM((EXPERT_WEIGHT_SLOTS, f, d), F32),
            pltpu.SemaphoreType.DMA((2,)), pltpu.SemaphoreType.DMA((EXPERT_WEIGHT_SLOTS,)),
        ],
    )
    return pl.pallas_call(
        _expert_kernel,
        grid_spec=grid_spec,
        out_shape=jax.ShapeDtypeStruct((n_blocks * rows_per_block, d), F32),
        compiler_params=pltpu.CompilerParams(dimension_semantics=("arbitrary",), vmem_limit_bytes=VMEM_LIMIT),
        name="experts",
    )(be, nvalid, n_used, dest0, dest1, h, g2, w_gate, w_up, w_down)


def _final_kernel(d0_ref, d1_ref, h_ref, y_hbm, gate_ref, fg_ref, o_ref, ybuf, sem):
    i = pl.program_id(0)
    groups = ybuf.shape[2]
    tm = groups * SUBLANES
    d = ybuf.shape[4]

    def start_tile(tile, slot):
        def group(g, c):
            for u in range(SUBLANES):
                tk = tile * tm + g * SUBLANES + u
                for k, dref in enumerate((d0_ref, d1_ref)):
                    pltpu.make_async_copy(y_hbm.at[pl.ds(dref[tk], 1)], ybuf.at[slot, k, g, pl.ds(u, 1)],
                                          sem.at[slot]).start()
            return c
        lax.fori_loop(0, groups, group, 0)

    @pl.when(i == 0)
    def _():
        start_tile(0, 0)

    slot = i & 1

    @pl.when(i + 1 < pl.num_programs(0))
    def _():
        start_tile(i + 1, 1 - slot)

    pltpu.make_async_copy(ybuf.at[slot], ybuf.at[slot], sem.at[slot]).wait()
    y0 = ybuf[slot, 0].reshape(tm, d)
    y1 = ybuf[slot, 1].reshape(tm, d)
    out = h_ref[...] + gate_ref[:, 0:1] * y0 + gate_ref[:, 1:2] * y1
    o_ref[...] = _rms(out, fg_ref[...])


def _final_call(dest0, dest1, h, y, gates, fg, *, tm):
    t, d = h.shape
    assert tm % SUBLANES == 0
    row = lambda i, *_: (i, 0)
    grid_spec = pltpu.PrefetchScalarGridSpec(
        num_scalar_prefetch=2,
        grid=(t // tm,),
        in_specs=[pl.BlockSpec((tm, d), row), pl.BlockSpec(memory_space=pl.ANY),
                  pl.BlockSpec((tm, LANES), row), pl.BlockSpec(fg.shape, lambda i, *_: (0, 0))],
        out_specs=pl.BlockSpec((tm, d), row),
        scratch_shapes=[pltpu.VMEM((2, 2, tm // SUBLANES, SUBLANES, d), F32), pltpu.SemaphoreType.DMA((2,))],
    )
    return pl.pallas_call(
        _final_kernel,
        grid_spec=grid_spec,
        out_shape=jax.ShapeDtypeStruct((t, d), F32),
        compiler_params=pltpu.CompilerParams(dimension_semantics=("arbitrary",), vmem_limit_bytes=VMEM_LIMIT),
        name="final",
    )(dest0, dest1, h, y, gates, fg)


def _wsplit_kernel(w_ref, lat_out, u_out, kr_out, *, o_kr, o_u):
    w = w_ref[...]
    lat_out[...] = w[:, :o_kr].astype(BF16)
    u_out[...] = w[:, o_u:].astype(BF16)
    kr = w[:, o_kr:o_u]
    kr_out[...] = jnp.concatenate([kr, jnp.zeros((kr.shape[0], LANES - kr.shape[1]), F32)], axis=1).astype(BF16)


def _wsplit_call(wi, *, o_kr, o_u, rows):
    d, cols = wi.shape
    row = lambda i: (i, 0)
    return pl.pallas_call(
        functools.partial(_wsplit_kernel, o_kr=o_kr, o_u=o_u),
        grid=(d // rows,),
        in_specs=[pl.BlockSpec((rows, cols), row)],
        out_specs=[pl.BlockSpec((rows, o_kr), row), pl.BlockSpec((rows, cols - o_u), row),
                   pl.BlockSpec((rows, LANES), row)],
        out_shape=[jax.ShapeDtypeStruct((d, o_kr), BF16), jax.ShapeDtypeStruct((d, cols - o_u), BF16),
                   jax.ShapeDtypeStruct((d, LANES), BF16)],
        compiler_params=pltpu.CompilerParams(dimension_semantics=("arbitrary",), vmem_limit_bytes=VMEM_LIMIT),
        name="wsplit",
    )(wi)


def _rope_tables(seq):
    pos = np.arange(seq, dtype=np.float64)
    inv_freq = ROPE_THETA ** (-np.arange(0, QK_ROPE_DIM, 2, dtype=np.float64) / QK_ROPE_DIM)
    ang = pos[:, None] * inv_freq[None, :]
    cos, sin = np.cos(ang).astype(np.float32), np.sin(ang).astype(np.float32)
    zero = np.zeros_like(sin)
    cos_t = np.concatenate([cos, cos, cos, cos], axis=1)
    s1_t = np.concatenate([zero, sin, zero, sin], axis=1)
    s2_t = np.concatenate([-sin, zero, -sin, zero], axis=1)
    return jnp.asarray(cos_t), jnp.asarray(s1_t), jnp.asarray(s2_t)


def kernel(x, ln1_g, w_in, b_glu, q_norm_g, w_uq, kv_norm_g, w_ukv, w_dw, b_dw, conv_ln_g, conv_ln_b,
           w_o, ln2_g, w_group, b_group, w_router, b_router, w_gate, w_up, w_down, final_g):
    batch, seq, d = x.shape
    assert ln1_g.shape[0] == 1, "single-layer trunk"
    t = batch * seq
    q_rank = q_norm_g.shape[1]
    kv_rank = kv_norm_g.shape[1]
    x2 = x.reshape(t, d)

    wi = w_in[0]
    o_kr = q_rank + kv_rank
    o_u = o_kr + QK_ROPE_DIM
    wlat, wu, wkr = _wsplit_call(wi, o_kr=o_kr, o_u=o_u, rows=min(256, d))
    wuq = w_uq[0].reshape(q_rank, MLA_HEADS, QK_NOPE_DIM + QK_ROPE_DIM)
    wuq = jnp.pad(wuq, ((0, 0), (0, 0), (0, HEAD_SLOT - QK_NOPE_DIM - QK_ROPE_DIM)))
    wuq = wuq.reshape(q_rank, MLA_HEADS * HEAD_SLOT).astype(BF16)
    wukv = w_ukv[0]
    wo = w_o[0]
    wr = jnp.concatenate([w_router[0], w_group[0],
                          jnp.zeros((d, LANES - N_EXPERTS - N_EXPERT_GROUPS), F32)], axis=1)
    wr_hi = wr.astype(BF16)
    wr_lo = (wr - wr_hi.astype(F32)).astype(BF16)
    wr2 = jnp.concatenate([wr_hi, wr_lo], axis=1)
    br = jnp.concatenate([b_router[0], b_group[0],
                          jnp.zeros((LANES - N_EXPERTS - N_EXPERT_GROUPS,), F32)])[None, :]
    cos_t, s1_t, s2_t = _rope_tables(seq)

    tm = min(512, seq)
    q, k, v, c = _proj_call(x2, ln1_g, wlat, wu, wkr, b_glu, q_norm_g, kv_norm_g, wuq, wukv, cos_t, s1_t, s2_t,
                            seq=seq, tm=tm)
    attn, conv = _attn_conv_call(q, k, v, c, w_dw[0], b_dw, batch=batch, seq=seq, tq=min(256, seq))
    h, logits = _oproj_call(attn, conv, x2, wo, conv_ln_g, conv_ln_b, ln2_g, wr2, br, tm=tm)

    n_blocks = -(-(2 * t + N_EXPERTS * (MOE_ROWS - 1)) // MOE_ROWS)
    dest0, dest1, gates, meta = _route_call(logits, rows_per_block=MOE_ROWS, n_blocks=n_blocks)
    y = _expert_call(meta[:n_blocks, 0], meta[:n_blocks, 1], meta[0:1, 2],
                     dest0, dest1, h, ln2_g, w_gate[0], w_up[0], w_down[0],
                     rows_per_block=MOE_ROWS, n_blocks=n_blocks)
    out = _final_call(dest0, dest1, h, y, gates, final_g[None, :], tm=min(256, seq))
    return out.reshape(batch, seq, d)
```

```python
import functools

import jax
import jax.numpy as jnp
import numpy as np
from jax import lax
from jax.experimental import pallas as pl
from jax.experimental.pallas import tpu as pltpu

F32 = jnp.float32
BF16 = jnp.bfloat16
I32 = jnp.int32

MLA_HEADS = 8
QK_NOPE_DIM = 128
QK_ROPE_DIM = 64
V_HEAD_DIM = 128
ROPE_THETA = 10000.0
N_EXPERT_GROUPS = 8
EXPERTS_PER_GROUP = 8
N_EXPERTS = N_EXPERT_GROUPS * EXPERTS_PER_GROUP
EPS = 1e-6
LOG2E = 1.4426950408889634

LANES = 128
SUBLANES = 8
HEAD_SLOT = 2 * LANES
ROPE_HALF = QK_ROPE_DIM // 2
VMEM_LIMIT = 56 * 1024 * 1024

MOE_ROWS = 256
ROUTE_UNROLL = 4
EXPERT_WEIGHT_SLOTS = 3
DMA_UNROLL = 8
DMA_UNROLL_LOG2 = DMA_UNROLL.bit_length() - 1


def _rms(x, g):
    return x * lax.rsqrt(jnp.mean(x * x, axis=-1, keepdims=True) + EPS) * g


def _dot(a, b):
    return jnp.dot(a, b, preferred_element_type=F32)


def _whole(shape, single=False):
    mode = dict(pipeline_mode=pl.Buffered(1)) if single else {}
    return pl.BlockSpec(shape, lambda *_: (0,) * len(shape), **mode)


def _proj_kernel(x_ref, g1_ref, wlat_ref, wu_ref, wkr_ref, bglu_ref, qg_ref, kvg_ref, wuq_ref, wukv_ref,
                 cos_ref, s1_ref, s2_ref, q_out, k_out, v_out, c_out, *, q_rank, kv_rank, conv_ch, q_scale):
    xn = _rms(x_ref[...], g1_ref[...]).astype(BF16)
    cos = cos_ref[...]
    s1 = s1_ref[...]
    s2 = s2_ref[...]

    def rope(t):
        return t * cos + pltpu.roll(t, ROPE_HALF, 1) * s1 + pltpu.roll(t, LANES - ROPE_HALF, 1) * s2

    a = _dot(xn, wu_ref[:, :conv_ch]) + bglu_ref[:, :conv_ch]
    gate = _dot(xn, wu_ref[:, conv_ch:]) + bglu_ref[:, conv_ch:]
    c_out[...] = (a * jax.nn.sigmoid(gate)).astype(BF16)

    kr = rope(_dot(xn, wkr_ref[...])).astype(BF16)
    qn = _rms(_dot(xn, wlat_ref[:, :q_rank]), qg_ref[...]).astype(BF16)
    kvn = _rms(_dot(xn, wlat_ref[:, q_rank:q_rank + kv_rank]), kvg_ref[...]).astype(BF16)
    ones_blk = (lax.broadcasted_iota(I32, (x_ref.shape[0], LANES), 1) == 0).astype(BF16)
    for h in range(MLA_HEADS):
        c0 = h * HEAD_SLOT
        qh = _dot(qn, wuq_ref[:, c0:c0 + HEAD_SLOT])
        q_out[:, c0:c0 + LANES] = (qh[:, :LANES] * q_scale).astype(BF16)
        q_out[:, c0 + LANES:c0 + HEAD_SLOT] = (rope(qh[:, LANES:]) * q_scale).astype(BF16)
        kvh = _dot(kvn, wukv_ref[:, c0:c0 + HEAD_SLOT].astype(BF16))
        k_out[:, c0:c0 + LANES] = kvh[:, :LANES].astype(BF16)
        k_out[:, c0 + LANES:c0 + HEAD_SLOT] = kr
        v_out[:, c0:c0 + LANES] = kvh[:, LANES:].astype(BF16)
        v_out[:, c0 + LANES:c0 + HEAD_SLOT] = ones_blk


def _proj_call(x2, g1, wlat, wu, wkr, bglu, qg, kvg, wuq, wukv, cos_t, s1_t, s2_t, *, seq, tm):
    t, d = x2.shape
    q_rank, kv_rank = qg.shape[1], kvg.shape[1]
    conv_ch = bglu.shape[1] // 2
    n_pos = seq // tm
    row = lambda i: (i, 0)
    pos = lambda i: (i % n_pos, 0)
    q_scale = float(QK_NOPE_DIM + QK_ROPE_DIM) ** -0.5 * LOG2E
    kern = functools.partial(_proj_kernel, q_rank=q_rank, kv_rank=kv_rank, conv_ch=conv_ch, q_scale=q_scale)
    slot_w = MLA_HEADS * HEAD_SLOT
    return pl.pallas_call(
        kern,
        grid=(t // tm,),
        in_specs=[
            pl.BlockSpec((tm, d), row), _whole(g1.shape), _whole(wlat.shape, True), _whole(wu.shape, True),
            _whole(wkr.shape, True), _whole(bglu.shape),
            _whole(qg.shape), _whole(kvg.shape), _whole(wuq.shape, True), _whole(wukv.shape, True),
            pl.BlockSpec((tm, LANES), pos), pl.BlockSpec((tm, LANES), pos), pl.BlockSpec((tm, LANES), pos),
        ],
        out_specs=[
            pl.BlockSpec((tm, slot_w), row), pl.BlockSpec((tm, slot_w), row),
            pl.BlockSpec((tm, slot_w), row), pl.BlockSpec((tm, conv_ch), row),
        ],
        out_shape=[
            jax.ShapeDtypeStruct((t, slot_w), BF16), jax.ShapeDtypeStruct((t, slot_w), BF16),
            jax.ShapeDtypeStruct((t, slot_w), BF16), jax.ShapeDtypeStruct((t, conv_ch), BF16),
        ],
        compiler_params=pltpu.CompilerParams(dimension_semantics=("arbitrary",), vmem_limit_bytes=VMEM_LIMIT),
        name="proj",
    )(x2, g1, wlat, wu, wkr, bglu, qg, kvg, wuq, wukv, cos_t, s1_t, s2_t)


CONV_PAD = 16
CONV_ROWS = 128


def _attn_conv_kernel(q_ref, k_ref, v_ref, c_ref, w_ref, b_ref, o_ref, y_ref, xp_ref, sh_ref, *, tq, width):
    s_len = q_ref.shape[0]
    half = width // 2
    rows = min(CONV_ROWS, s_len)
    span = sh_ref.shape[1]
    zeros = jnp.zeros((CONV_PAD, LANES), F32)
    xp_ref[0:CONV_PAD, :] = zeros
    xp_ref[CONV_PAD + s_len:, :] = zeros
    xp_ref[CONV_PAD:CONV_PAD + s_len, :] = c_ref[...].astype(F32)

    def conv_chunk(ci):
        base = ci * rows
        xw = xp_ref[base:base + rows + 2 * CONV_PAD, :]
        for r in range(1, SUBLANES):
            sh_ref[r - 1] = xw[r:r + span, :]
        acc = jnp.zeros((rows, LANES), F32)
        for k in range(width):
            off = CONV_PAD - half + k
            r, a0 = off % SUBLANES, off - off % SUBLANES
            if r == 0:
                tap = xp_ref[base + a0:base + a0 + rows, :]
            else:
                tap = sh_ref[r - 1, a0:a0 + rows, :]
            acc = acc + tap * w_ref[k:k + 1, :]
        y_ref[base:base + rows, :] = (acc + b_ref[...]).astype(BF16)

    k = k_ref[...]
    v = v_ref[...]
    n_q = s_len // tq
    n_chunks = s_len // rows
    for j in range(n_q):
        qs = slice(j * tq, (j + 1) * tq)
        s = lax.dot_general(q_ref[qs, :], k, (((1,), (1,)), ((), ())), preferred_element_type=F32)
        m = jnp.max(s, axis=-1, keepdims=True)
        p = jnp.exp2(s - m).astype(BF16)
        o = _dot(p, v)
        o_ref[qs, :] = (o[:, :V_HEAD_DIM] / o[:, V_HEAD_DIM:V_HEAD_DIM + 1]).astype(BF16)
        for ci in range(j * n_chunks // n_q, (j + 1) * n_chunks // n_q):
            conv_chunk(ci)


def _attn_conv_call(q, k, v, c, w_dw, b_dw, *, batch, seq, tq):
    t, ch = c.shape
    width = w_dw.shape[0]
    assert width // 2 <= CONV_PAD and ch == MLA_HEADS * LANES
    rows = min(CONV_ROWS, seq)
    head = lambda b, h: (b, h)
    chan = lambda b, h: (0, h)
    return pl.pallas_call(
        functools.partial(_attn_conv_kernel, tq=tq, width=width),
        grid=(batch, MLA_HEADS),
        in_specs=[pl.BlockSpec((seq, HEAD_SLOT), head), pl.BlockSpec((seq, HEAD_SLOT), head),
                  pl.BlockSpec((seq, HEAD_SLOT), head), pl.BlockSpec((seq, LANES), head),
                  pl.BlockSpec((width, LANES), chan), pl.BlockSpec((1, LANES), chan)],
        out_specs=[pl.BlockSpec((seq, V_HEAD_DIM), head), pl.BlockSpec((seq, LANES), head)],
        out_shape=[jax.ShapeDtypeStruct((t, MLA_HEADS * V_HEAD_DIM), BF16), jax.ShapeDtypeStruct((t, ch), BF16)],
        scratch_shapes=[pltpu.VMEM((seq + 2 * CONV_PAD, LANES), F32),
                        pltpu.VMEM((SUBLANES - 1, rows + 2 * CONV_PAD - SUBLANES, LANES), F32)],
        compiler_params=pltpu.CompilerParams(
            dimension_semantics=("arbitrary", "arbitrary"), vmem_limit_bytes=VMEM_LIMIT),
        name="attn_conv",
    )(q, k, v, c, w_dw, b_dw)


def _oproj_kernel(a_ref, c_ref, x_ref, wo_ref, lg_ref, lb_ref, g2_ref, wr_ref, br_ref, h_out, lg_out):
    na = a_ref.shape[1]
    y = c_ref[...].astype(F32)
    yc = y - jnp.mean(y, axis=-1, keepdims=True)
    z = yc * lax.rsqrt(jnp.mean(yc * yc, axis=-1, keepdims=True) + EPS) * lg_ref[...] + lb_ref[...]
    act = (z * jax.nn.sigmoid(z)).astype(BF16)
    h = (x_ref[...] + _dot(a_ref[...], wo_ref[:na, :].astype(BF16))
         + _dot(act, wo_ref[na:, :].astype(BF16)))
    h_out[...] = h
    hn = _rms(h, g2_ref[...])
    hi = hn.astype(BF16)
    lo = (hn - hi.astype(F32)).astype(BF16)
    r = _dot(hi, wr_ref[...])
    lg_out[...] = r[:, :LANES] + r[:, LANES:] + _dot(lo, wr_ref[:, :LANES]) + br_ref[...]


def _oproj_call(attn, conv, x2, wo, ln_g, ln_b, g2, wr, br, *, tm):
    t, d = x2.shape
    row = lambda i: (i, 0)
    return pl.pallas_call(
        _oproj_kernel,
        grid=(t // tm,),
        in_specs=[pl.BlockSpec((tm, attn.shape[1]), row), pl.BlockSpec((tm, conv.shape[1]), row),
                  pl.BlockSpec((tm, d), row), _whole(wo.shape, True), _whole(ln_g.shape), _whole(ln_b.shape),
                  _whole(g2.shape), _whole(wr.shape, True), _whole(br.shape)],
        out_specs=[pl.BlockSpec((tm, d), row), pl.BlockSpec((tm, LANES), row)],
        out_shape=[jax.ShapeDtypeStruct((t, d), F32), jax.ShapeDtypeStruct((t, LANES), F32)],
        compiler_params=pltpu.CompilerParams(dimension_semantics=("arbitrary",), vmem_limit_bytes=VMEM_LIMIT),
        name="oproj",
    )(attn, conv, x2, wo, ln_g, ln_b, g2, wr, br)


def _route_tokens_on_lanes(lt):
    shape = (SUBLANES, LANES)
    row = lax.broadcasted_iota(I32, shape, 0)
    big = jnp.int32(1 << 20)
    neg = jnp.float32(-jnp.inf)

    def top(v):
        m = jnp.max(v, axis=0, keepdims=True)
        return m, jnp.min(jnp.where(v == m, row, big), axis=0, keepdims=True)

    lgrp = lt[N_EXPERTS:N_EXPERTS + N_EXPERT_GROUPS, :]
    gmax, gsel = top(lgrp)
    p_g = 1.0 / jnp.sum(jnp.exp(lgrp - gmax), axis=0, keepdims=True)
    le = jnp.zeros(shape, F32)
    for g in range(N_EXPERT_GROUPS):
        le = jnp.where(gsel == g, lt[g * EXPERTS_PER_GROUP:(g + 1) * EXPERTS_PER_GROUP, :], le)
    m1, i1 = top(le)
    m2, i2 = top(jnp.where(row == i1, neg, le))
    r = jnp.exp(m2 - m1)
    w1 = 1.0 / (1.0 + r)
    w2 = r / (1.0 + r)
    base = gsel * EXPERTS_PER_GROUP
    return base + i1, base + i2, p_g * w1, p_g * w2


def _lane_cumsum(v):
    lane = lax.broadcasted_iota(I32, v.shape, 1)
    sh = 1
    while sh < LANES:
        v = v + jnp.where(lane >= sh, pltpu.roll(v, sh, 1), 0)
        sh *= 2
    return v


def _route_kernel(lg_ref, d0_out, d1_out, gate_out, meta_out, e_ref, *, rows_per_block):
    n_chunks = d0_out.shape[0]
    shift = rows_per_block.bit_length() - 1
    sub_shift = SUBLANES.bit_length() - 1
    sq = (LANES, LANES)
    row = lax.broadcasted_iota(I32, sq, 0)
    row8 = lax.broadcasted_iota(I32, (SUBLANES, LANES), 0)

    def one_hots(e1, e2):
        return (row == e1).astype(F32), (row == e2).astype(F32)

    def count_step(i, cnt):
        base = pl.multiple_of(i * LANES, LANES)
        e1, e2, g1, g2 = _route_tokens_on_lanes(jnp.transpose(lg_ref[pl.ds(base, LANES), :]))
        e_ref[i] = jnp.where(row8 == 0, e1, jnp.where(row8 == 1, e2, 0))
        gate_out[pl.ds(base, LANES), :] = jnp.transpose(jnp.where(row == 0, g1, jnp.where(row == 1, g2, 0.0)))
        oh1, oh2 = one_hots(e1, e2)
        return cnt + jnp.sum(oh1 + oh2, axis=1, keepdims=True)

    unroll = ROUTE_UNROLL if n_chunks % ROUTE_UNROLL == 0 else 1
    counts_col = lax.fori_loop(0, n_chunks, count_step, jnp.zeros((LANES, 1), F32), unroll=unroll)
    counts = jnp.transpose(jnp.broadcast_to(counts_col, sq))[0:SUBLANES, :].astype(I32)
    padded = ((counts + (SUBLANES - 1)) >> sub_shift) << sub_shift
    pad_end = _lane_cumsum(padded)
    pad_start = pad_end - padded
    start_col = jnp.transpose(jnp.broadcast_to(pad_start[0:1, :].astype(F32), sq))[:, 0:1]

    tri = (row < lax.broadcasted_iota(I32, sq, 1)).astype(BF16)

    def dest_step(i, carry):
        er = e_ref[i]
        oh1, oh2 = one_hots(er[0:1, :], er[1:2, :])
        oh = oh1 + oh2
        pos = carry + _dot(oh.astype(BF16), tri)
        d0_out[i] = jnp.sum(oh1 * pos, axis=0, keepdims=True).astype(I32)
        d1_out[i] = jnp.sum(oh2 * pos, axis=0, keepdims=True).astype(I32)
        return carry + jnp.sum(oh, axis=1, keepdims=True)

    lax.fori_loop(0, n_chunks, dest_step, start_col, unroll=unroll)

    nbp = meta_out.shape[0]
    lane_b = lax.broadcasted_iota(I32, (nbp, LANES), 1)
    blk = lax.broadcasted_iota(I32, (nbp, LANES), 0)
    nblk = (counts + (rows_per_block - 1)) >> shift
    blk_end = _lane_cumsum(nblk)
    bcast = lambda v: jnp.broadcast_to(v[0:1, :], (nbp, LANES))
    be_end, be_start, cn, ps = bcast(blk_end), bcast(blk_end - nblk), bcast(counts), bcast(pad_start)
    is_e = lane_b < N_EXPERTS
    lsum = lambda v: jnp.sum(v, axis=-1, keepdims=True)
    n_used = jnp.max(be_end, axis=-1, keepdims=True)
    total = jnp.max(bcast(pad_end), axis=-1, keepdims=True)
    last_e = jnp.max(jnp.where(is_e & (cn > 0), lane_b, 0), axis=-1, keepdims=True)
    be = jnp.minimum(lsum(jnp.where(is_e & (be_end <= blk), 1, 0)), last_e)
    sel = lane_b == be
    first_row = (blk[:, 0:1] - lsum(jnp.where(sel, be_start, 0))) * rows_per_block
    used = blk[:, 0:1] < n_used
    nvalid = jnp.where(used, jnp.clip(lsum(jnp.where(sel, cn, 0)) - first_row, 0, rows_per_block), 0)
    row0 = jnp.where(used, lsum(jnp.where(sel, ps, 0)) + first_row, 0)
    cols = (be, nvalid, n_used, row0, total)
    meta = jnp.zeros((nbp, LANES), I32)
    for c, v in enumerate(cols):
        meta = jnp.where(lane_b == c, v, meta)
    meta_out[...] = meta


def _route_call(logits, *, rows_per_block, n_blocks):
    t = logits.shape[0]
    assert t % LANES == 0
    nbp = -(-n_blocks // SUBLANES) * SUBLANES
    dshape = (t // LANES, 1, LANES)
    d0, d1, gates, meta = pl.pallas_call(
        functools.partial(_route_kernel, rows_per_block=rows_per_block),
        in_specs=[_whole(logits.shape)],
        out_specs=[_whole(dshape), _whole(dshape), _whole((t, LANES)), _whole((nbp, LANES))],
        out_shape=[jax.ShapeDtypeStruct(dshape, I32), jax.ShapeDtypeStruct(dshape, I32),
                   jax.ShapeDtypeStruct((t, LANES), F32), jax.ShapeDtypeStruct((nbp, LANES), I32)],
        grid=(1,),
        scratch_shapes=[pltpu.VMEM((t // LANES, SUBLANES, LANES), I32)],
        compiler_params=pltpu.CompilerParams(dimension_semantics=("arbitrary",), vmem_limit_bytes=VMEM_LIMIT),
        name="route",
    )(logits)
    return d0.reshape(t), d1.reshape(t), gates, meta


def _pow2_chunks(limit):
    c = 1 << (limit.bit_length() - 1)
    while c >= 1:
        yield c
        c >>= 1


def _expert_kernel(be_ref, nv_ref, nused_ref, row0_ref, total_ref, d0_ref, d1_ref,
                   h_hbm, g2_ref, wg_hbm, wu_hbm, wd_hbm, y_hbm,
                   tok_ref, ord_ref, exp_ref, nexp_ref, xbuf, ybuf, zbuf, wg_buf, wu_buf, wd_buf,
                   gsem, ysem, zsem, wsem):
    b = pl.program_id(0)
    n_used = nused_ref[0]
    groups = xbuf.shape[1]
    rows = groups * SUBLANES
    d = xbuf.shape[3]
    n_tok = d0_ref.shape[0]
    sub_shift = SUBLANES.bit_length() - 1
    n_slots = wg_buf.shape[0]

    def y_copies(blk, slot):
        ng = lax.shift_right_logical(nv_ref[blk] + (SUBLANES - 1), sub_shift)
        g0 = lax.shift_right_logical(row0_ref[blk], sub_shift)
        out = []
        for c in _pow2_chunks(groups):
            off = ng & ~(2 * c - 1)
            copy = pltpu.make_async_copy(ybuf.at[slot, pl.ds(off, c)], y_hbm.at[pl.ds(g0 + off, c)], ysem.at[slot])
            out.append(((ng & c) != 0, copy))
        return out

    def start_y(blk, slot):
        for cond, copy in y_copies(blk, slot):
            pl.when(cond)(copy.start)

    def wait_y(blk, slot):
        for cond, copy in y_copies(blk, slot):
            pl.when(cond)(copy.wait)

    def fill_tail():
        zbuf[...] = jnp.zeros(zbuf.shape, F32)
        first = lax.shift_right_logical(total_ref[0], sub_shift)
        copy = lambda g: pltpu.make_async_copy(zbuf, y_hbm.at[g], zsem.at[0])
        lax.fori_loop(first, y_hbm.shape[0], lambda g, c: (copy(g).start(), c)[1], 0)
        lax.fori_loop(first, y_hbm.shape[0], lambda g, c: (copy(g).wait(), c)[1], 0)

    def weight_copies(j):
        e = exp_ref[j]
        slot = lax.rem(j, n_slots)
        return [pltpu.make_async_copy(src.at[e], dst.at[slot], wsem.at[slot])
                for src, dst in ((wg_hbm, wg_buf), (wu_hbm, wu_buf), (wd_hbm, wd_buf))]

    def start_weights(j):
        for c in weight_copies(j):
            c.start()

    def wait_weights(j):
        for c in weight_copies(j):
            c.wait()

    def gather_copy(blk, slot, g, u):
        tok = tok_ref[row0_ref[blk] + g * SUBLANES + u]
        return pltpu.make_async_copy(h_hbm.at[pl.ds(tok, 1)], xbuf.at[slot, g, pl.ds(u, 1)], gsem.at[slot])

    def start_gather(blk, slot):
        n = nv_ref[blk]

        def group(g, c):
            for u in range(SUBLANES):
                gather_copy(blk, slot, g, u).start()
            return c
        full = lax.shift_right_logical(n, sub_shift)
        lax.fori_loop(0, full, group, 0)
        for u in range(SUBLANES - 1):
            @pl.when(full * SUBLANES + u < n)
            def _():
                gather_copy(blk, slot, full, u).start()

    def wait_gather(blk, slot):
        n = nv_ref[blk]
        buf = xbuf.at[slot]
        for c in _pow2_chunks(rows):
            @pl.when((n & c) != 0)
            def _():
                if c >= SUBLANES:
                    part = buf.at[pl.ds(0, c // SUBLANES)]
                else:
                    part = buf.at[0, pl.ds(0, c)]
                pltpu.make_async_copy(part, part, gsem.at[slot]).wait()

    @pl.when(b == 0)
    def _():
        def scan(blk, j):
            e = be_ref[blk]
            is_new = jnp.logical_or(blk == 0, e != be_ref[jnp.maximum(blk - 1, 0)])
            j = j + is_new.astype(I32)
            ord_ref[blk] = j - 1

            @pl.when(is_new)
            def _():
                exp_ref[j - 1] = e
            return j
        n_exp = lax.fori_loop(0, n_used, scan, jnp.int32(0))
        nexp_ref[0] = n_exp
        for j in range(n_slots - 1):
            @pl.when(j < n_exp)
            def _():
                start_weights(j)

        def inv(g, c):
            for u in range(DMA_UNROLL):
                tk = g * DMA_UNROLL + u
                tok_ref[d0_ref[tk]] = tk
                tok_ref[d1_ref[tk]] = tk
            return c
        lax.fori_loop(0, n_tok // DMA_UNROLL, inv, 0)
        xbuf[...] = jnp.zeros(xbuf.shape, F32)
        start_gather(0, 0)

    @pl.when(b < n_used)
    def _():
        slot = b & 1
        wait_gather(b, slot)

        @pl.when(b + 1 < n_used)
        def _():
            start_gather(b + 1, 1 - slot)

        j = ord_ref[b]

        @pl.when(jnp.logical_or(b == 0, ord_ref[jnp.maximum(b - 1, 0)] != j))
        def _():
            wait_weights(j)

            @pl.when(j + (n_slots - 1) < nexp_ref[0])
            def _():
                start_weights(j + (n_slots - 1))

        ws = lax.rem(j, n_slots)
        hn = _rms(xbuf[slot].reshape(rows, d), g2_ref[...]).astype(BF16)
        gate = _dot(hn, wg_buf[ws].astype(BF16))
        up = _dot(hn, wu_buf[ws].astype(BF16))
        hmid = (gate * jax.nn.sigmoid(gate) * up).astype(BF16)
        y = _dot(hmid, wd_buf[ws].astype(BF16))

        @pl.when(b >= 2)
        def _():
            wait_y(b - 2, slot)

        ybuf[slot] = y.reshape(groups, SUBLANES, d)
        start_y(b, slot)

        @pl.when(b == n_used - 1)
        def _():
            @pl.when(b >= 1)
            def _():
                wait_y(b - 1, 1 - slot)
            wait_y(b, slot)
            fill_tail()


def _expert_call(be, nvalid, n_used, row0, total, dest0, dest1, h, g2, w_gate, w_up, w_down, *,
                 rows_per_block, n_blocks):
    t, d = h.shape
    n_exp, _, f = w_gate.shape
    assert t % DMA_UNROLL == 0 and rows_per_block % SUBLANES == 0 and (2 * t) % SUBLANES == 0
    groups = rows_per_block // SUBLANES
    sorted_groups = 2 * t // SUBLANES + n_exp
    hbm = pl.BlockSpec(memory_space=pl.ANY)
    slots = EXPERT_WEIGHT_SLOTS
    grid_spec = pltpu.PrefetchScalarGridSpec(
        num_scalar_prefetch=7,
        grid=(n_blocks,),
        in_specs=[hbm, pl.BlockSpec(g2.shape, lambda b, *_: (0, 0)), hbm, hbm, hbm],
        out_specs=hbm,
        scratch_shapes=[
            pltpu.SMEM((sorted_groups * SUBLANES,), I32),
            pltpu.SMEM((n_blocks,), I32), pltpu.SMEM((n_exp,), I32), pltpu.SMEM((1,), I32),
            pltpu.VMEM((2, groups, SUBLANES, d), F32), pltpu.VMEM((2, groups, SUBLANES, d), F32),
            pltpu.VMEM((SUBLANES, d), F32),
            pltpu.VMEM((slots, d, f), F32), pltpu.VMEM((slots, d, f), F32), pltpu.VMEM((slots, f, d), F32),
            pltpu.SemaphoreType.DMA((2,)), pltpu.SemaphoreType.DMA((2,)), pltpu.SemaphoreType.DMA((1,)),
            pltpu.SemaphoreType.DMA((slots,)),
        ],
    )
    y = pl.pallas_call(
        _expert_kernel,
        grid_spec=grid_spec,
        out_shape=jax.ShapeDtypeStruct((sorted_groups, SUBLANES, d), F32),
        compiler_params=pltpu.CompilerParams(dimension_semantics=("arbitrary",), vmem_limit_bytes=VMEM_LIMIT),
        name="experts",
    )(be, nvalid, n_used, row0, total, dest0, dest1, h, g2, w_gate, w_up, w_down)
    return y.reshape(sorted_groups * SUBLANES, d)


def _final_kernel(d0_ref, d1_ref, h_ref, y_hbm, gate_ref, fg_ref, o_ref, ybuf, sem):
    i = pl.program_id(0)
    groups = ybuf.shape[2]
    tm = groups * SUBLANES
    d = ybuf.shape[4]

    def start_tile(tile, slot):
        def group(g, c):
            for u in range(SUBLANES):
                tk = tile * tm + g * SUBLANES + u
                for k, dref in enumerate((d0_ref, d1_ref)):
                    pltpu.make_async_copy(y_hbm.at[pl.ds(dref[tk], 1)], ybuf.at[slot, k, g, pl.ds(u, 1)],
                                          sem.at[slot]).start()
            return c
        lax.fori_loop(0, groups, group, 0)

    @pl.when(i == 0)
    def _():
        start_tile(0, 0)

    slot = i & 1

    @pl.when(i + 1 < pl.num_programs(0))
    def _():
        start_tile(i + 1, 1 - slot)

    pltpu.make_async_copy(ybuf.at[slot], ybuf.at[slot], sem.at[slot]).wait()
    y0 = ybuf[slot, 0].reshape(tm, d)
    y1 = ybuf[slot, 1].reshape(tm, d)
    out = h_ref[...] + gate_ref[:, 0:1] * y0 + gate_ref[:, 1:2] * y1
    o_ref[...] = _rms(out, fg_ref[...])


def _final_call(dest0, dest1, h, y, gates, fg, *, tm):
    t, d = h.shape
    assert tm % SUBLANES == 0
    row = lambda i, *_: (i, 0)
    grid_spec = pltpu.PrefetchScalarGridSpec(
        num_scalar_prefetch=2,
        grid=(t // tm,),
        in_specs=[pl.BlockSpec((tm, d), row), pl.BlockSpec(memory_space=pl.ANY),
                  pl.BlockSpec((tm, LANES), row), pl.BlockSpec(fg.shape, lambda i, *_: (0, 0))],
        out_specs=pl.BlockSpec((tm, d), row),
        scratch_shapes=[pltpu.VMEM((2, 2, tm // SUBLANES, SUBLANES, d), F32), pltpu.SemaphoreType.DMA((2,))],
    )
    return pl.pallas_call(
        _final_kernel,
        grid_spec=grid_spec,
        out_shape=jax.ShapeDtypeStruct((t, d), F32),
        compiler_params=pltpu.CompilerParams(dimension_semantics=("arbitrary",), vmem_limit_bytes=VMEM_LIMIT),
        name="final",
    )(dest0, dest1, h, y, gates, fg)


def _wsplit_kernel(lat_ref, u_ref, kr_ref, lat_out, u_out, kr_out, *, n_lat, n_kr):
    j = pl.program_id(0)
    u_out[...] = jnp.transpose(u_ref[...]).astype(BF16)

    @pl.when(j < n_lat)
    def _():
        lat_out[...] = jnp.transpose(lat_ref[...]).astype(BF16)

    @pl.when(j == 0)
    def _():
        kr = jnp.transpose(kr_ref[...])
        lane = lax.broadcasted_iota(I32, kr.shape, 1)
        kr_out[...] = jnp.where(lane < n_kr, kr, 0.0).astype(BF16)


def _wsplit_call(w_t, *, o_kr, o_u):
    cols, d = w_t.shape
    n_u = (cols - o_u) // LANES
    n_lat = o_kr // LANES
    assert o_kr % LANES == 0 and (cols - o_u) % LANES == 0 and o_u % SUBLANES == 0
    assert 0 < o_u - o_kr <= LANES and n_lat <= n_u and o_kr + LANES <= cols
    lat_blk = lambda j: jnp.minimum(j, n_lat - 1)
    return pl.pallas_call(
        functools.partial(_wsplit_kernel, n_lat=n_lat, n_kr=o_u - o_kr),
        grid=(n_u,),
        in_specs=[pl.BlockSpec((LANES, d), lambda j: (lat_blk(j), 0)),
                  pl.BlockSpec((pl.Element(LANES), pl.Element(d)),
                               lambda j: (pl.multiple_of(o_u + j * LANES, SUBLANES), 0)),
                  pl.BlockSpec((pl.Element(LANES), pl.Element(d)), lambda j: (o_kr, 0))],
        out_specs=[pl.BlockSpec((d, LANES), lambda j: (0, lat_blk(j))), pl.BlockSpec((d, LANES), lambda j: (0, j)),
                   pl.BlockSpec((d, LANES), lambda j: (0, 0))],
        out_shape=[jax.ShapeDtypeStruct((d, o_kr), BF16), jax.ShapeDtypeStruct((d, cols - o_u), BF16),
                   jax.ShapeDtypeStruct((d, LANES), BF16)],
        compiler_params=pltpu.CompilerParams(dimension_semantics=("arbitrary",), vmem_limit_bytes=VMEM_LIMIT),
        name="wsplit",
    )(w_t, w_t, w_t)


def _rope_tables(seq):
    pos = np.arange(seq, dtype=np.float64)
    inv_freq = ROPE_THETA ** (-np.arange(0, QK_ROPE_DIM, 2, dtype=np.float64) / QK_ROPE_DIM)
    ang = pos[:, None] * inv_freq[None, :]
    cos, sin = np.cos(ang).astype(np.float32), np.sin(ang).astype(np.float32)
    zero = np.zeros_like(sin)
    cos_t = np.concatenate([cos, cos, cos, cos], axis=1)
    s1_t = np.concatenate([zero, sin, zero, sin], axis=1)
    s2_t = np.concatenate([-sin, zero, -sin, zero], axis=1)
    return jnp.asarray(cos_t), jnp.asarray(s1_t), jnp.asarray(s2_t)


def kernel(x, ln1_g, w_in, b_glu, q_norm_g, w_uq, kv_norm_g, w_ukv, w_dw, b_dw, conv_ln_g, conv_ln_b,
           w_o, ln2_g, w_group, b_group, w_router, b_router, w_gate, w_up, w_down, final_g):
    batch, seq, d = x.shape
    assert ln1_g.shape[0] == 1, "single-layer trunk"
    t = batch * seq
    q_rank = q_norm_g.shape[1]
    kv_rank = kv_norm_g.shape[1]
    x2 = x.reshape(t, d)

    wi = w_in[0]
    o_kr = q_rank + kv_rank
    o_u = o_kr + QK_ROPE_DIM
    wlat, wu, wkr = _wsplit_call(jnp.transpose(wi), o_kr=o_kr, o_u=o_u)
    wuq = w_uq[0].reshape(q_rank, MLA_HEADS, QK_NOPE_DIM + QK_ROPE_DIM)
    wuq = jnp.pad(wuq, ((0, 0), (0, 0), (0, HEAD_SLOT - QK_NOPE_DIM - QK_ROPE_DIM)))
    wuq = wuq.reshape(q_rank, MLA_HEADS * HEAD_SLOT).astype(BF16)
    wukv = w_ukv[0]
    wo = w_o[0]
    wr = jnp.concatenate([w_router[0], w_group[0],
                          jnp.zeros((d, LANES - N_EXPERTS - N_EXPERT_GROUPS), F32)], axis=1)
    wr_hi = wr.astype(BF16)
    wr_lo = (wr - wr_hi.astype(F32)).astype(BF16)
    wr2 = jnp.concatenate([wr_hi, wr_lo], axis=1)
    br = jnp.concatenate([b_router[0], b_group[0],
                          jnp.zeros((LANES - N_EXPERTS - N_EXPERT_GROUPS,), F32)])[None, :]
    cos_t, s1_t, s2_t = _rope_tables(seq)

    tm = min(512, seq)
    q, k, v, c = _proj_call(x2, ln1_g, wlat, wu, wkr, b_glu, q_norm_g, kv_norm_g, wuq, wukv, cos_t, s1_t, s2_t,
                            seq=seq, tm=tm)
    attn, conv = _attn_conv_call(q, k, v, c, w_dw[0], b_dw, batch=batch, seq=seq, tq=min(256, seq))
    h, logits = _oproj_call(attn, conv, x2, wo, conv_ln_g, conv_ln_b, ln2_g, wr2, br, tm=tm)

    n_blocks = -(-(2 * t + N_EXPERTS * (MOE_ROWS - 1)) // MOE_ROWS)
    dest0, dest1, gates, meta = _route_call(logits, rows_per_block=MOE_ROWS, n_blocks=n_blocks)
    y = _expert_call(meta[:n_blocks, 0], meta[:n_blocks, 1], meta[0:1, 2], meta[:n_blocks, 3], meta[0:1, 4],
                     dest0, dest1, h, ln2_g, w_gate[0], w_up[0], w_down[0],
                     rows_per_block=MOE_ROWS, n_blocks=n_blocks)
    out = _final_call(dest0, dest1, h, y, gates, final_g[None, :], tm=min(256, seq))
    return out.reshape(batch, seq, d)
```

```python
import functools

import jax
import jax.numpy as jnp
import numpy as np
from jax import lax
from jax.experimental import pallas as pl
from jax.experimental.pallas import tpu as pltpu

F32 = jnp.float32
BF16 = jnp.bfloat16
I32 = jnp.int32

MLA_HEADS = 8
QK_NOPE_DIM = 128
QK_ROPE_DIM = 64
V_HEAD_DIM = 128
ROPE_THETA = 10000.0
N_EXPERT_GROUPS = 8
EXPERTS_PER_GROUP = 8
N_EXPERTS = N_EXPERT_GROUPS * EXPERTS_PER_GROUP
EPS = 1e-6
LOG2E = 1.4426950408889634

LANES = 128
SUBLANES = 8
HEAD_SLOT = 2 * LANES
ROPE_HALF = QK_ROPE_DIM // 2
VMEM_LIMIT = 56 * 1024 * 1024

MOE_ROWS = 256
ROUTE_UNROLL = 4
EXPERT_WEIGHT_SLOTS = 3
DMA_UNROLL = 8
DMA_UNROLL_LOG2 = DMA_UNROLL.bit_length() - 1


def _rms(x, g):
    return x * lax.rsqrt(jnp.mean(x * x, axis=-1, keepdims=True) + EPS) * g


def _dot(a, b):
    return jnp.dot(a, b, preferred_element_type=F32)


def _whole(shape, single=False):
    mode = dict(pipeline_mode=pl.Buffered(1)) if single else {}
    return pl.BlockSpec(shape, lambda *_: (0,) * len(shape), **mode)


def _proj_kernel(x_ref, g1_ref, wlat_ref, wu_ref, wkr_ref, bglu_ref, qg_ref, kvg_ref, wuq_ref, wukv_ref,
                 cos_ref, s1_ref, s2_ref, q_out, k_out, v_out, c_out, *, q_rank, kv_rank, conv_ch, q_scale):
    xn = _rms(x_ref[...], g1_ref[...]).astype(BF16)
    cos = cos_ref[...]
    s1 = s1_ref[...]
    s2 = s2_ref[...]

    def rope(t):
        return t * cos + pltpu.roll(t, ROPE_HALF, 1) * s1 + pltpu.roll(t, LANES - ROPE_HALF, 1) * s2

    a = _dot(xn, wu_ref[:, :conv_ch]) + bglu_ref[:, :conv_ch]
    gate = _dot(xn, wu_ref[:, conv_ch:]) + bglu_ref[:, conv_ch:]
    c_out[...] = (a * jax.nn.sigmoid(gate)).astype(BF16)

    kr = rope(_dot(xn, wkr_ref[...])).astype(BF16)
    qn = _rms(_dot(xn, wlat_ref[:, :q_rank]), qg_ref[...]).astype(BF16)
    kvn = _rms(_dot(xn, wlat_ref[:, q_rank:q_rank + kv_rank]), kvg_ref[...]).astype(BF16)
    ones_blk = (lax.broadcasted_iota(I32, (x_ref.shape[0], LANES), 1) == 0).astype(BF16)
    for h in range(MLA_HEADS):
        c0 = h * HEAD_SLOT
        qh = _dot(qn, wuq_ref[:, c0:c0 + HEAD_SLOT])
        q_out[:, c0:c0 + LANES] = (qh[:, :LANES] * q_scale).astype(BF16)
        q_out[:, c0 + LANES:c0 + HEAD_SLOT] = (rope(qh[:, LANES:]) * q_scale).astype(BF16)
        kvh = _dot(kvn, wukv_ref[:, c0:c0 + HEAD_SLOT].astype(BF16))
        k_out[:, c0:c0 + LANES] = kvh[:, :LANES].astype(BF16)
        k_out[:, c0 + LANES:c0 + HEAD_SLOT] = kr
        v_out[:, c0:c0 + LANES] = kvh[:, LANES:].astype(BF16)
        v_out[:, c0 + LANES:c0 + HEAD_SLOT] = ones_blk


def _proj_call(x2, g1, wlat, wu, wkr, bglu, qg, kvg, wuq, wukv, cos_t, s1_t, s2_t, *, seq, tm):
    t, d = x2.shape
    q_rank, kv_rank = qg.shape[1], kvg.shape[1]
    conv_ch = bglu.shape[1] // 2
    n_pos = seq // tm
    row = lambda i: (i, 0)
    pos = lambda i: (i % n_pos, 0)
    q_scale = float(QK_NOPE_DIM + QK_ROPE_DIM) ** -0.5 * LOG2E
    kern = functools.partial(_proj_kernel, q_rank=q_rank, kv_rank=kv_rank, conv_ch=conv_ch, q_scale=q_scale)
    slot_w = MLA_HEADS * HEAD_SLOT
    return pl.pallas_call(
        kern,
        grid=(t // tm,),
        in_specs=[
            pl.BlockSpec((tm, d), row), _whole(g1.shape), _whole(wlat.shape, True), _whole(wu.shape, True),
            _whole(wkr.shape, True), _whole(bglu.shape),
            _whole(qg.shape), _whole(kvg.shape), _whole(wuq.shape, True), _whole(wukv.shape, True),
            pl.BlockSpec((tm, LANES), pos), pl.BlockSpec((tm, LANES), pos), pl.BlockSpec((tm, LANES), pos),
        ],
        out_specs=[
            pl.BlockSpec((tm, slot_w), row), pl.BlockSpec((tm, slot_w), row),
            pl.BlockSpec((tm, slot_w), row), pl.BlockSpec((tm, conv_ch), row),
        ],
        out_shape=[
            jax.ShapeDtypeStruct((t, slot_w), BF16), jax.ShapeDtypeStruct((t, slot_w), BF16),
            jax.ShapeDtypeStruct((t, slot_w), BF16), jax.ShapeDtypeStruct((t, conv_ch), BF16),
        ],
        compiler_params=pltpu.CompilerParams(dimension_semantics=("arbitrary",), vmem_limit_bytes=VMEM_LIMIT),
        name="proj",
    )(x2, g1, wlat, wu, wkr, bglu, qg, kvg, wuq, wukv, cos_t, s1_t, s2_t)


CONV_PAD = 16
CONV_ROWS = 128


def _attn_conv_kernel(q_ref, k_ref, v_ref, c_ref, w_ref, b_ref, o_ref, y_ref, xp_ref, *, tq, width):
    s_len = q_ref.shape[0]
    half = width // 2
    rows = min(CONV_ROWS, s_len)
    win = rows + 2 * CONV_PAD
    zeros = jnp.zeros((CONV_PAD, LANES), F32)
    xp_ref[0:CONV_PAD, :] = zeros
    xp_ref[CONV_PAD + s_len:, :] = zeros
    xp_ref[CONV_PAD:CONV_PAD + s_len, :] = c_ref[...].astype(F32)

    def conv_chunk(ci):
        base = ci * rows
        xw = xp_ref[base:base + win, :]
        acc = jnp.zeros((rows, LANES), F32)
        for r in range(SUBLANES):
            shifted = xw if r == 0 else pltpu.roll(xw, win - r, 0)
            for a0 in range(0, 2 * CONV_PAD, SUBLANES):
                k = a0 + r - (CONV_PAD - half)
                if 0 <= k < width:
                    acc = acc + shifted[a0:a0 + rows, :] * w_ref[k:k + 1, :]
        y_ref[base:base + rows, :] = (acc + b_ref[...]).astype(BF16)
        return jnp.sum(acc.reshape(rows // SUBLANES, SUBLANES, LANES), axis=0)

    k = k_ref[...]
    v = v_ref[...]
    n_q = s_len // tq
    n_chunks = s_len // rows
    for j in range(n_q):
        qs = slice(j * tq, (j + 1) * tq)
        s = lax.dot_general(q_ref[qs, :], k, (((1,), (1,)), ((), ())), preferred_element_type=F32)
        part = None
        for ci in range(j * n_chunks // n_q, (j + 1) * n_chunks // n_q):
            r = conv_chunk(ci)
            part = r if part is None else part + r
        bits = lax.bitcast_convert_type(part, jnp.uint32)
        zero = ((bits >> 16) >> 16).astype(F32)[0:1, 0:1]
        m = jnp.max(s, axis=-1, keepdims=True) + zero
        p = jnp.exp2(s - m).astype(BF16)
        o = _dot(p, v)
        o_ref[qs, :] = (o[:, :V_HEAD_DIM] / o[:, V_HEAD_DIM:V_HEAD_DIM + 1]).astype(BF16)


def _attn_conv_call(q, k, v, c, w_dw, b_dw, *, batch, seq, tq):
    t, ch = c.shape
    width = w_dw.shape[0]
    assert width // 2 <= CONV_PAD and ch == MLA_HEADS * LANES
    head = lambda b, h: (b, h)
    chan = lambda b, h: (0, h)
    return pl.pallas_call(
        functools.partial(_attn_conv_kernel, tq=tq, width=width),
        grid=(batch, MLA_HEADS),
        in_specs=[pl.BlockSpec((seq, HEAD_SLOT), head), pl.BlockSpec((seq, HEAD_SLOT), head),
                  pl.BlockSpec((seq, HEAD_SLOT), head), pl.BlockSpec((seq, LANES), head),
                  pl.BlockSpec((width, LANES), chan), pl.BlockSpec((1, LANES), chan)],
        out_specs=[pl.BlockSpec((seq, V_HEAD_DIM), head), pl.BlockSpec((seq, LANES), head)],
        out_shape=[jax.ShapeDtypeStruct((t, MLA_HEADS * V_HEAD_DIM), BF16), jax.ShapeDtypeStruct((t, ch), BF16)],
        scratch_shapes=[pltpu.VMEM((seq + 2 * CONV_PAD, LANES), F32)],
        compiler_params=pltpu.CompilerParams(
            dimension_semantics=("arbitrary", "arbitrary"), vmem_limit_bytes=VMEM_LIMIT),
        name="attn_conv",
    )(q, k, v, c, w_dw, b_dw)


def _oproj_kernel(a_ref, c_ref, x_ref, wo_ref, lg_ref, lb_ref, g2_ref, wr_ref, br_ref, h_out, lg_out):
    na = a_ref.shape[1]
    y = c_ref[...].astype(F32)
    yc = y - jnp.mean(y, axis=-1, keepdims=True)
    z = yc * lax.rsqrt(jnp.mean(yc * yc, axis=-1, keepdims=True) + EPS) * lg_ref[...] + lb_ref[...]
    act = (z * jax.nn.sigmoid(z)).astype(BF16)
    h = (x_ref[...] + _dot(a_ref[...], wo_ref[:na, :].astype(BF16))
         + _dot(act, wo_ref[na:, :].astype(BF16)))
    h_out[...] = h
    hn = _rms(h, g2_ref[...])
    hi = hn.astype(BF16)
    lo = (hn - hi.astype(F32)).astype(BF16)
    r = _dot(hi, wr_ref[...])
    lg_out[...] = r[:, :LANES] + r[:, LANES:] + _dot(lo, wr_ref[:, :LANES]) + br_ref[...]


def _oproj_call(attn, conv, x2, wo, ln_g, ln_b, g2, wr, br, *, tm):
    t, d = x2.shape
    row = lambda i: (i, 0)
    return pl.pallas_call(
        _oproj_kernel,
        grid=(t // tm,),
        in_specs=[pl.BlockSpec((tm, attn.shape[1]), row), pl.BlockSpec((tm, conv.shape[1]), row),
                  pl.BlockSpec((tm, d), row), _whole(wo.shape, True), _whole(ln_g.shape), _whole(ln_b.shape),
                  _whole(g2.shape), _whole(wr.shape, True), _whole(br.shape)],
        out_specs=[pl.BlockSpec((tm, d), row), pl.BlockSpec((tm, LANES), row)],
        out_shape=[jax.ShapeDtypeStruct((t, d), F32), jax.ShapeDtypeStruct((t, LANES), F32)],
        compiler_params=pltpu.CompilerParams(dimension_semantics=("arbitrary",), vmem_limit_bytes=VMEM_LIMIT),
        name="oproj",
    )(attn, conv, x2, wo, ln_g, ln_b, g2, wr, br)


def _route_tokens_on_lanes(lt):
    shape = (SUBLANES, LANES)
    row = lax.broadcasted_iota(I32, shape, 0)
    big = jnp.int32(1 << 20)
    neg = jnp.float32(-jnp.inf)

    def top(v):
        m = jnp.max(v, axis=0, keepdims=True)
        return m, jnp.min(jnp.where(v == m, row, big), axis=0, keepdims=True)

    lgrp = lt[N_EXPERTS:N_EXPERTS + N_EXPERT_GROUPS, :]
    gmax, gsel = top(lgrp)
    p_g = 1.0 / jnp.sum(jnp.exp(lgrp - gmax), axis=0, keepdims=True)
    le = jnp.zeros(shape, F32)
    for g in range(N_EXPERT_GROUPS):
        le = jnp.where(gsel == g, lt[g * EXPERTS_PER_GROUP:(g + 1) * EXPERTS_PER_GROUP, :], le)
    m1, i1 = top(le)
    m2, i2 = top(jnp.where(row == i1, neg, le))
    r = jnp.exp(m2 - m1)
    w1 = 1.0 / (1.0 + r)
    w2 = r / (1.0 + r)
    base = gsel * EXPERTS_PER_GROUP
    return base + i1, base + i2, p_g * w1, p_g * w2


def _lane_cumsum(v):
    lane = lax.broadcasted_iota(I32, v.shape, 1)
    sh = 1
    while sh < LANES:
        v = v + jnp.where(lane >= sh, pltpu.roll(v, sh, 1), 0)
        sh *= 2
    return v


def _route_kernel(lg_ref, d0_out, d1_out, gate_out, meta_out, e_ref, *, rows_per_block):
    n_chunks = d0_out.shape[0]
    shift = rows_per_block.bit_length() - 1
    sub_shift = SUBLANES.bit_length() - 1
    sq = (LANES, LANES)
    row = lax.broadcasted_iota(I32, sq, 0)
    row8 = lax.broadcasted_iota(I32, (SUBLANES, LANES), 0)

    def one_hots(e1, e2):
        return (row == e1).astype(F32), (row == e2).astype(F32)

    def count_step(i, cnt):
        base = pl.multiple_of(i * LANES, LANES)
        e1, e2, g1, g2 = _route_tokens_on_lanes(jnp.transpose(lg_ref[pl.ds(base, LANES), :]))
        e_ref[i] = jnp.where(row8 == 0, e1, jnp.where(row8 == 1, e2, 0))
        gate_out[pl.ds(base, LANES), :] = jnp.transpose(jnp.where(row == 0, g1, jnp.where(row == 1, g2, 0.0)))
        oh1, oh2 = one_hots(e1, e2)
        return cnt + jnp.sum(oh1 + oh2, axis=1, keepdims=True)

    unroll = ROUTE_UNROLL if n_chunks % ROUTE_UNROLL == 0 else 1
    counts_col = lax.fori_loop(0, n_chunks, count_step, jnp.zeros((LANES, 1), F32), unroll=unroll)
    counts = jnp.transpose(jnp.broadcast_to(counts_col, sq))[0:SUBLANES, :].astype(I32)
    padded = ((counts + (SUBLANES - 1)) >> sub_shift) << sub_shift
    pad_end = _lane_cumsum(padded)
    pad_start = pad_end - padded
    start_col = jnp.transpose(jnp.broadcast_to(pad_start[0:1, :].astype(F32), sq))[:, 0:1]

    tri = (row < lax.broadcasted_iota(I32, sq, 1)).astype(BF16)

    def dest_step(i, carry):
        er = e_ref[i]
        oh1, oh2 = one_hots(er[0:1, :], er[1:2, :])
        oh = oh1 + oh2
        pos = carry + _dot(oh.astype(BF16), tri)
        d0_out[i] = jnp.sum(oh1 * pos, axis=0, keepdims=True).astype(I32)
        d1_out[i] = jnp.sum(oh2 * pos, axis=0, keepdims=True).astype(I32)
        return carry + jnp.sum(oh, axis=1, keepdims=True)

    lax.fori_loop(0, n_chunks, dest_step, start_col, unroll=unroll)

    nbp = meta_out.shape[0]
    lane_b = lax.broadcasted_iota(I32, (nbp, LANES), 1)
    blk = lax.broadcasted_iota(I32, (nbp, LANES), 0)
    nblk = (counts + (rows_per_block - 1)) >> shift
    blk_end = _lane_cumsum(nblk)
    bcast = lambda v: jnp.broadcast_to(v[0:1, :], (nbp, LANES))
    be_end, be_start, cn, ps = bcast(blk_end), bcast(blk_end - nblk), bcast(counts), bcast(pad_start)
    is_e = lane_b < N_EXPERTS
    lsum = lambda v: jnp.sum(v, axis=-1, keepdims=True)
    n_used = jnp.max(be_end, axis=-1, keepdims=True)
    total = jnp.max(bcast(pad_end), axis=-1, keepdims=True)
    last_e = jnp.max(jnp.where(is_e & (cn > 0), lane_b, 0), axis=-1, keepdims=True)
    be = jnp.minimum(lsum(jnp.where(is_e & (be_end <= blk), 1, 0)), last_e)
    sel = lane_b == be
    first_row = (blk[:, 0:1] - lsum(jnp.where(sel, be_start, 0))) * rows_per_block
    used = blk[:, 0:1] < n_used
    nvalid = jnp.where(used, jnp.clip(lsum(jnp.where(sel, cn, 0)) - first_row, 0, rows_per_block), 0)
    row0 = jnp.where(used, lsum(jnp.where(sel, ps, 0)) + first_row, 0)
    cols = (be, nvalid, n_used, row0, total)
    meta = jnp.zeros((nbp, LANES), I32)
    for c, v in enumerate(cols):
        meta = jnp.where(lane_b == c, v, meta)
    meta_out[...] = meta


def _route_call(logits, *, rows_per_block, n_blocks):
    t = logits.shape[0]
    assert t % LANES == 0
    nbp = -(-n_blocks // SUBLANES) * SUBLANES
    dshape = (t // LANES, 1, LANES)
    d0, d1, gates, meta = pl.pallas_call(
        functools.partial(_route_kernel, rows_per_block=rows_per_block),
        in_specs=[_whole(logits.shape)],
        out_specs=[_whole(dshape), _whole(dshape), _whole((t, LANES)), _whole((nbp, LANES))],
        out_shape=[jax.ShapeDtypeStruct(dshape, I32), jax.ShapeDtypeStruct(dshape, I32),
                   jax.ShapeDtypeStruct((t, LANES), F32), jax.ShapeDtypeStruct((nbp, LANES), I32)],
        grid=(1,),
        scratch_shapes=[pltpu.VMEM((t // LANES, SUBLANES, LANES), I32)],
        compiler_params=pltpu.CompilerParams(dimension_semantics=("arbitrary",), vmem_limit_bytes=VMEM_LIMIT),
        name="route",
    )(logits)
    return d0.reshape(t), d1.reshape(t), gates, meta


def _pow2_chunks(limit):
    c = 1 << (limit.bit_length() - 1)
    while c >= 1:
        yield c
        c >>= 1


def _expert_kernel(be_ref, nv_ref, nused_ref, row0_ref, total_ref, d0_ref, d1_ref,
                   h_hbm, g2_ref, wg_hbm, wu_hbm, wd_hbm, y_hbm,
                   tok_ref, ord_ref, exp_ref, nexp_ref, xbuf, ybuf, zbuf, wg_buf, wu_buf, wd_buf,
                   gsem, ysem, zsem, wsem):
    b = pl.program_id(0)
    n_used = nused_ref[0]
    groups = xbuf.shape[1]
    rows = groups * SUBLANES
    d = xbuf.shape[3]
    n_tok = d0_ref.shape[0]
    sub_shift = SUBLANES.bit_length() - 1
    n_slots = wg_buf.shape[0]

    def y_copies(blk, slot):
        ng = lax.shift_right_logical(nv_ref[blk] + (SUBLANES - 1), sub_shift)
        g0 = lax.shift_right_logical(row0_ref[blk], sub_shift)
        out = []
        for c in _pow2_chunks(groups):
            off = ng & ~(2 * c - 1)
            copy = pltpu.make_async_copy(ybuf.at[slot, pl.ds(off, c)], y_hbm.at[pl.ds(g0 + off, c)], ysem.at[slot])
            out.append(((ng & c) != 0, copy))
        return out

    def start_y(blk, slot):
        for cond, copy in y_copies(blk, slot):
            pl.when(cond)(copy.start)

    def wait_y(blk, slot):
        for cond, copy in y_copies(blk, slot):
            pl.when(cond)(copy.wait)

    def fill_tail():
        zbuf[...] = jnp.zeros(zbuf.shape, F32)
        first = lax.shift_right_logical(total_ref[0], sub_shift)
        copy = lambda g: pltpu.make_async_copy(zbuf, y_hbm.at[g], zsem.at[0])
        lax.fori_loop(first, y_hbm.shape[0], lambda g, c: (copy(g).start(), c)[1], 0)
        lax.fori_loop(first, y_hbm.shape[0], lambda g, c: (copy(g).wait(), c)[1], 0)

    def weight_copies(j):
        e = exp_ref[j]
        slot = lax.rem(j, n_slots)
        return [pltpu.make_async_copy(src.at[e], dst.at[slot], wsem.at[slot])
                for src, dst in ((wg_hbm, wg_buf), (wu_hbm, wu_buf), (wd_hbm, wd_buf))]

    def start_weights(j):
        for c in weight_copies(j):
            c.start(priority=1)

    def wait_weights(j):
        for c in weight_copies(j):
            c.wait()

    def gather_copy(blk, slot, g, u):
        tok = tok_ref[row0_ref[blk] + g * SUBLANES + u]
        return pltpu.make_async_copy(h_hbm.at[pl.ds(tok, 1)], xbuf.at[slot, g, pl.ds(u, 1)], gsem.at[slot])

    def start_gather(blk, slot):
        n = nv_ref[blk]

        def group(g, c):
            for u in range(SUBLANES):
                gather_copy(blk, slot, g, u).start()
            return c
        full = lax.shift_right_logical(n, sub_shift)
        lax.fori_loop(0, full, group, 0)
        for u in range(SUBLANES - 1):
            @pl.when(full * SUBLANES + u < n)
            def _():
                gather_copy(blk, slot, full, u).start()

    def wait_gather(blk, slot):
        n = nv_ref[blk]
        buf = xbuf.at[slot]
        for c in _pow2_chunks(rows):
            @pl.when((n & c) != 0)
            def _():
                if c >= SUBLANES:
                    part = buf.at[pl.ds(0, c // SUBLANES)]
                else:
                    part = buf.at[0, pl.ds(0, c)]
                pltpu.make_async_copy(part, part, gsem.at[slot]).wait()

    @pl.when(b == 0)
    def _():
        def scan(blk, j):
            e = be_ref[blk]
            is_new = jnp.logical_or(blk == 0, e != be_ref[jnp.maximum(blk - 1, 0)])
            j = j + is_new.astype(I32)
            ord_ref[blk] = j - 1

            @pl.when(is_new)
            def _():
                exp_ref[j - 1] = e
            return j
        n_exp = lax.fori_loop(0, n_used, scan, jnp.int32(0))
        nexp_ref[0] = n_exp
        for j in range(n_slots - 1):
            @pl.when(j < n_exp)
            def _():
                start_weights(j)

        def inv(g, c):
            for u in range(DMA_UNROLL):
                tk = g * DMA_UNROLL + u
                tok_ref[d0_ref[tk]] = tk
                tok_ref[d1_ref[tk]] = tk
            return c
        lax.fori_loop(0, n_tok // DMA_UNROLL, inv, 0)
        xbuf[...] = jnp.zeros(xbuf.shape, F32)
        start_gather(0, 0)

    @pl.when(b < n_used)
    def _():
        slot = b & 1
        wait_gather(b, slot)

        @pl.when(b + 1 < n_used)
        def _():
            start_gather(b + 1, 1 - slot)

        j = ord_ref[b]

        @pl.when(jnp.logical_or(b == 0, ord_ref[jnp.maximum(b - 1, 0)] != j))
        def _():
            wait_weights(j)

            @pl.when(j + (n_slots - 1) < nexp_ref[0])
            def _():
                start_weights(j + (n_slots - 1))

        ws = lax.rem(j, n_slots)
        hn = _rms(xbuf[slot].reshape(rows, d), g2_ref[...]).astype(BF16)
        gate = _dot(hn, wg_buf[ws].astype(BF16))
        up = _dot(hn, wu_buf[ws].astype(BF16))
        hmid = (gate * jax.nn.sigmoid(gate) * up).astype(BF16)
        y = _dot(hmid, wd_buf[ws].astype(BF16))

        @pl.when(b >= 2)
        def _():
            wait_y(b - 2, slot)

        ybuf[slot] = y.reshape(groups, SUBLANES, d)
        start_y(b, slot)

        @pl.when(b == n_used - 1)
        def _():
            @pl.when(b >= 1)
            def _():
                wait_y(b - 1, 1 - slot)
            wait_y(b, slot)
            fill_tail()


def _expert_call(be, nvalid, n_used, row0, total, dest0, dest1, h, g2, w_gate, w_up, w_down, *,
                 rows_per_block, n_blocks):
    t, d = h.shape
    n_exp, _, f = w_gate.shape
    assert t % DMA_UNROLL == 0 and rows_per_block % SUBLANES == 0 and (2 * t) % SUBLANES == 0
    groups = rows_per_block // SUBLANES
    sorted_groups = 2 * t // SUBLANES + n_exp
    hbm = pl.BlockSpec(memory_space=pl.ANY)
    slots = EXPERT_WEIGHT_SLOTS
    grid_spec = pltpu.PrefetchScalarGridSpec(
        num_scalar_prefetch=7,
        grid=(n_blocks,),
        in_specs=[hbm, pl.BlockSpec(g2.shape, lambda b, *_: (0, 0)), hbm, hbm, hbm],
        out_specs=hbm,
        scratch_shapes=[
            pltpu.SMEM((sorted_groups * SUBLANES,), I32),
            pltpu.SMEM((n_blocks,), I32), pltpu.SMEM((n_exp,), I32), pltpu.SMEM((1,), I32),
            pltpu.VMEM((2, groups, SUBLANES, d), F32), pltpu.VMEM((2, groups, SUBLANES, d), F32),
            pltpu.VMEM((SUBLANES, d), F32),
            pltpu.VMEM((slots, d, f), F32), pltpu.VMEM((slots, d, f), F32), pltpu.VMEM((slots, f, d), F32),
            pltpu.SemaphoreType.DMA((2,)), pltpu.SemaphoreType.DMA((2,)), pltpu.SemaphoreType.DMA((1,)),
            pltpu.SemaphoreType.DMA((slots,)),
        ],
    )
    y = pl.pallas_call(
        _expert_kernel,
        grid_spec=grid_spec,
        out_shape=jax.ShapeDtypeStruct((sorted_groups, SUBLANES, d), F32),
        compiler_params=pltpu.CompilerParams(dimension_semantics=("arbitrary",), vmem_limit_bytes=VMEM_LIMIT),
        name="experts",
    )(be, nvalid, n_used, row0, total, dest0, dest1, h, g2, w_gate, w_up, w_down)
    return y.reshape(sorted_groups * SUBLANES, d)


def _final_kernel(d0_ref, d1_ref, h_ref, y_hbm, gate_ref, fg_ref, o_ref, ybuf, sem):
    i = pl.program_id(0)
    groups = ybuf.shape[2]
    tm = groups * SUBLANES
    d = ybuf.shape[4]

    def start_tile(tile, slot):
        def group(g, c):
            for u in range(SUBLANES):
                tk = tile * tm + g * SUBLANES + u
                for k, dref in enumerate((d0_ref, d1_ref)):
                    pltpu.make_async_copy(y_hbm.at[pl.ds(dref[tk], 1)], ybuf.at[slot, k, g, pl.ds(u, 1)],
                                          sem.at[slot]).start(priority=k)
            return c
        lax.fori_loop(0, groups, group, 0)

    @pl.when(i == 0)
    def _():
        start_tile(0, 0)

    slot = i & 1

    @pl.when(i + 1 < pl.num_programs(0))
    def _():
        start_tile(i + 1, 1 - slot)

    pltpu.make_async_copy(ybuf.at[slot], ybuf.at[slot], sem.at[slot]).wait()
    y0 = ybuf[slot, 0].reshape(tm, d)
    y1 = ybuf[slot, 1].reshape(tm, d)
    out = h_ref[...] + gate_ref[:, 0:1] * y0 + gate_ref[:, 1:2] * y1
    o_ref[...] = _rms(out, fg_ref[...])


def _final_call(dest0, dest1, h, y, gates, fg, *, tm):
    t, d = h.shape
    assert tm % SUBLANES == 0
    row = lambda i, *_: (i, 0)
    grid_spec = pltpu.PrefetchScalarGridSpec(
        num_scalar_prefetch=2,
        grid=(t // tm,),
        in_specs=[pl.BlockSpec((tm, d), row), pl.BlockSpec(memory_space=pl.ANY),
                  pl.BlockSpec((tm, LANES), row), pl.BlockSpec(fg.shape, lambda i, *_: (0, 0))],
        out_specs=pl.BlockSpec((tm, d), row),
        scratch_shapes=[pltpu.VMEM((2, 2, tm // SUBLANES, SUBLANES, d), F32), pltpu.SemaphoreType.DMA((2,))],
    )
    return pl.pallas_call(
        _final_kernel,
        grid_spec=grid_spec,
        out_shape=jax.ShapeDtypeStruct((t, d), F32),
        compiler_params=pltpu.CompilerParams(dimension_semantics=("arbitrary",), vmem_limit_bytes=VMEM_LIMIT),
        name="final",
    )(dest0, dest1, h, y, gates, fg)


def _wsplit_kernel(lat_ref, u_ref, kr_ref, lat_out, u_out, kr_out, *, n_lat, n_kr):
    j = pl.program_id(0)
    u_out[...] = jnp.transpose(u_ref[...]).astype(BF16)

    @pl.when(j < n_lat)
    def _():
        lat_out[...] = jnp.transpose(lat_ref[...]).astype(BF16)

    @pl.when(j == 0)
    def _():
        kr = jnp.transpose(kr_ref[...])
        lane = lax.broadcasted_iota(I32, kr.shape, 1)
        kr_out[...] = jnp.where(lane < n_kr, kr, 0.0).astype(BF16)


def _wsplit_call(w_t, *, o_kr, o_u):
    cols, d = w_t.shape
    n_u = (cols - o_u) // LANES
    n_lat = o_kr // LANES
    assert o_kr % LANES == 0 and (cols - o_u) % LANES == 0 and o_u % SUBLANES == 0
    assert 0 < o_u - o_kr <= LANES and n_lat <= n_u and o_kr + LANES <= cols
    lat_blk = lambda j: jnp.minimum(j, n_lat - 1)
    return pl.pallas_call(
        functools.partial(_wsplit_kernel, n_lat=n_lat, n_kr=o_u - o_kr),
        grid=(n_u,),
        in_specs=[pl.BlockSpec((LANES, d), lambda j: (lat_blk(j), 0)),
                  pl.BlockSpec((pl.Element(LANES), pl.Element(d)),
                               lambda j: (pl.multiple_of(o_u + j * LANES, SUBLANES), 0)),
                  pl.BlockSpec((pl.Element(LANES), pl.Element(d)), lambda j: (o_kr, 0))],
        out_specs=[pl.BlockSpec((d, LANES), lambda j: (0, lat_blk(j))), pl.BlockSpec((d, LANES), lambda j: (0, j)),
                   pl.BlockSpec((d, LANES), lambda j: (0, 0))],
        out_shape=[jax.ShapeDtypeStruct((d, o_kr), BF16), jax.ShapeDtypeStruct((d, cols - o_u), BF16),
                   jax.ShapeDtypeStruct((d, LANES), BF16)],
        compiler_params=pltpu.CompilerParams(dimension_semantics=("arbitrary",), vmem_limit_bytes=VMEM_LIMIT),
        name="wsplit",
    )(w_t, w_t, w_t)


def _rope_tables(seq):
    pos = np.arange(seq, dtype=np.float64)
    inv_freq = ROPE_THETA ** (-np.arange(0, QK_ROPE_DIM, 2, dtype=np.float64) / QK_ROPE_DIM)
    ang = pos[:, None] * inv_freq[None, :]
    cos, sin = np.cos(ang).astype(np.float32), np.sin(ang).astype(np.float32)
    zero = np.zeros_like(sin)
    cos_t = np.concatenate([cos, cos, cos, cos], axis=1)
    s1_t = np.concatenate([zero, sin, zero, sin], axis=1)
    s2_t = np.concatenate([-sin, zero, -sin, zero], axis=1)
    return jnp.asarray(cos_t), jnp.asarray(s1_t), jnp.asarray(s2_t)


def kernel(x, ln1_g, w_in, b_glu, q_norm_g, w_uq, kv_norm_g, w_ukv, w_dw, b_dw, conv_ln_g, conv_ln_b,
           w_o, ln2_g, w_group, b_group, w_router, b_router, w_gate, w_up, w_down, final_g):
    batch, seq, d = x.shape
    assert ln1_g.shape[0] == 1, "single-layer trunk"
    t = batch * seq
    q_rank = q_norm_g.shape[1]
    kv_rank = kv_norm_g.shape[1]
    x2 = x.reshape(t, d)

    wi = w_in[0]
    o_kr = q_rank + kv_rank
    o_u = o_kr + QK_ROPE_DIM
    wlat, wu, wkr = _wsplit_call(jnp.transpose(wi), o_kr=o_kr, o_u=o_u)
    wuq = w_uq[0].reshape(q_rank, MLA_HEADS, QK_NOPE_DIM + QK_ROPE_DIM)
    wuq = jnp.pad(wuq, ((0, 0), (0, 0), (0, HEAD_SLOT - QK_NOPE_DIM - QK_ROPE_DIM)))
    wuq = wuq.reshape(q_rank, MLA_HEADS * HEAD_SLOT).astype(BF16)
    wukv = w_ukv[0]
    wo = w_o[0]
    wr = jnp.concatenate([w_router[0], w_group[0],
                          jnp.zeros((d, LANES - N_EXPERTS - N_EXPERT_GROUPS), F32)], axis=1)
    wr_hi = wr.astype(BF16)
    wr_lo = (wr - wr_hi.astype(F32)).astype(BF16)
    wr2 = jnp.concatenate([wr_hi, wr_lo], axis=1)
    br = jnp.concatenate([b_router[0], b_group[0],
                          jnp.zeros((LANES - N_EXPERTS - N_EXPERT_GROUPS,), F32)])[None, :]
    cos_t, s1_t, s2_t = _rope_tables(seq)

    tm = min(512, seq)
    q, k, v, c = _proj_call(x2, ln1_g, wlat, wu, wkr, b_glu, q_norm_g, kv_norm_g, wuq, wukv, cos_t, s1_t, s2_t,
                            seq=seq, tm=tm)
    attn, conv = _attn_conv_call(q, k, v, c, w_dw[0], b_dw, batch=batch, seq=seq, tq=min(256, seq))
    h, logits = _oproj_call(attn, conv, x2, wo, conv_ln_g, conv_ln_b, ln2_g, wr2, br, tm=tm)

    n_blocks = -(-(2 * t + N_EXPERTS * (MOE_ROWS - 1)) // MOE_ROWS)
    dest0, dest1, gates, meta = _route_call(logits, rows_per_block=MOE_ROWS, n_blocks=n_blocks)
    y = _expert_call(meta[:n_blocks, 0], meta[:n_blocks, 1], meta[0:1, 2], meta[:n_blocks, 3], meta[0:1, 4],
                     dest0, dest1, h, ln2_g, w_gate[0], w_up[0], w_down[0],
                     rows_per_block=MOE_ROWS, n_blocks=n_blocks)
    out = _final_call(dest0, dest1, h, y, gates, final_g[None, :], tm=min(256, seq))
    return out.reshape(batch, seq, d)
```

```python
import functools

import jax
import jax.numpy as jnp
import numpy as np
from jax import lax
from jax.experimental import pallas as pl
from jax.experimental.pallas import tpu as pltpu

F32 = jnp.float32
BF16 = jnp.bfloat16
I32 = jnp.int32

MLA_HEADS = 8
QK_NOPE_DIM = 128
QK_ROPE_DIM = 64
V_HEAD_DIM = 128
ROPE_THETA = 10000.0
N_EXPERT_GROUPS = 8
EXPERTS_PER_GROUP = 8
N_EXPERTS = N_EXPERT_GROUPS * EXPERTS_PER_GROUP
EPS = 1e-6
LOG2E = 1.4426950408889634

LANES = 128
SUBLANES = 8
HEAD_SLOT = 2 * LANES
ROPE_HALF = QK_ROPE_DIM // 2
VMEM_LIMIT = 56 * 1024 * 1024

MOE_ROWS = 256
ROUTE_UNROLL = 4
EXPERT_WEIGHT_SLOTS = 3
DMA_UNROLL = 8
DMA_UNROLL_LOG2 = DMA_UNROLL.bit_length() - 1


def _rms(x, g):
    return x * lax.rsqrt(jnp.mean(x * x, axis=-1, keepdims=True) + EPS) * g


def _dot(a, b):
    return jnp.dot(a, b, preferred_element_type=F32)


def _whole(shape, single=False):
    mode = dict(pipeline_mode=pl.Buffered(1)) if single else {}
    return pl.BlockSpec(shape, lambda *_: (0,) * len(shape), **mode)


def _proj_kernel(x_ref, g1_ref, wlat_ref, wu_ref, wkr_ref, bglu_ref, qg_ref, kvg_ref, wuq_ref, wukv_ref,
                 cos_ref, s1_ref, s2_ref, q_out, k_out, v_out, c_out, *, q_rank, kv_rank, conv_ch, q_scale):
    xn = _rms(x_ref[...], g1_ref[...]).astype(BF16)
    cos = cos_ref[...]
    s1 = s1_ref[...]
    s2 = s2_ref[...]

    def rope(t):
        return t * cos + pltpu.roll(t, ROPE_HALF, 1) * s1 + pltpu.roll(t, LANES - ROPE_HALF, 1) * s2

    a = _dot(xn, wu_ref[:, :conv_ch]) + bglu_ref[:, :conv_ch]
    gate = _dot(xn, wu_ref[:, conv_ch:]) + bglu_ref[:, conv_ch:]
    c_out[...] = (a * jax.nn.sigmoid(gate)).astype(BF16)

    kr = rope(_dot(xn, wkr_ref[...])).astype(BF16)
    qn = _rms(_dot(xn, wlat_ref[:, :q_rank]), qg_ref[...]).astype(BF16)
    kvn = _rms(_dot(xn, wlat_ref[:, q_rank:q_rank + kv_rank]), kvg_ref[...]).astype(BF16)
    lane = lax.broadcasted_iota(I32, (x_ref.shape[0], LANES), 1)
    ones_blk = (lane == 0).astype(BF16)
    n_nope = MLA_HEADS * QK_NOPE_DIM
    q_nope = _dot(qn, wuq_ref[:, :n_nope])
    q_rope = _dot(qn, wuq_ref[:, n_nope:])
    for h in range(MLA_HEADS):
        c0 = h * HEAD_SLOT
        q_out[:, c0:c0 + LANES] = (q_nope[:, h * QK_NOPE_DIM:(h + 1) * QK_NOPE_DIM] * q_scale).astype(BF16)
        if h % 2 == 0:
            pair = rope(q_rope[:, (h // 2) * LANES:(h // 2 + 1) * LANES]) * q_scale
        half_pair = pair if h % 2 == 0 else pltpu.roll(pair, LANES - QK_ROPE_DIM, 1)
        q_out[:, c0 + LANES:c0 + HEAD_SLOT] = jnp.where(lane < QK_ROPE_DIM, half_pair, 0.0).astype(BF16)
        kvh = _dot(kvn, wukv_ref[:, c0:c0 + HEAD_SLOT].astype(BF16))
        k_out[:, c0:c0 + LANES] = kvh[:, :LANES].astype(BF16)
        k_out[:, c0 + LANES:c0 + HEAD_SLOT] = kr
        v_out[:, c0:c0 + LANES] = kvh[:, LANES:].astype(BF16)
        v_out[:, c0 + LANES:c0 + HEAD_SLOT] = ones_blk


def _proj_call(x2, g1, wlat, wu, wkr, bglu, qg, kvg, wuq, wukv, cos_t, s1_t, s2_t, *, seq, tm):
    t, d = x2.shape
    q_rank, kv_rank = qg.shape[1], kvg.shape[1]
    conv_ch = bglu.shape[1] // 2
    n_pos = seq // tm
    row = lambda i: (i, 0)
    pos = lambda i: (i % n_pos, 0)
    q_scale = float(QK_NOPE_DIM + QK_ROPE_DIM) ** -0.5 * LOG2E
    kern = functools.partial(_proj_kernel, q_rank=q_rank, kv_rank=kv_rank, conv_ch=conv_ch, q_scale=q_scale)
    slot_w = MLA_HEADS * HEAD_SLOT
    return pl.pallas_call(
        kern,
        grid=(t // tm,),
        in_specs=[
            pl.BlockSpec((tm, d), row), _whole(g1.shape), _whole(wlat.shape, True), _whole(wu.shape, True),
            _whole(wkr.shape, True), _whole(bglu.shape),
            _whole(qg.shape), _whole(kvg.shape), _whole(wuq.shape, True), _whole(wukv.shape, True),
            pl.BlockSpec((tm, LANES), pos), pl.BlockSpec((tm, LANES), pos), pl.BlockSpec((tm, LANES), pos),
        ],
        out_specs=[
            pl.BlockSpec((tm, slot_w), row), pl.BlockSpec((tm, slot_w), row),
            pl.BlockSpec((tm, slot_w), row), pl.BlockSpec((tm, conv_ch), row),
        ],
        out_shape=[
            jax.ShapeDtypeStruct((t, slot_w), BF16), jax.ShapeDtypeStruct((t, slot_w), BF16),
            jax.ShapeDtypeStruct((t, slot_w), BF16), jax.ShapeDtypeStruct((t, conv_ch), BF16),
        ],
        compiler_params=pltpu.CompilerParams(dimension_semantics=("arbitrary",), vmem_limit_bytes=VMEM_LIMIT),
        name="proj",
    )(x2, g1, wlat, wu, wkr, bglu, qg, kvg, wuq, wukv, cos_t, s1_t, s2_t)


CONV_PAD = 16
CONV_ROWS = 128


def _attn_conv_kernel(q_ref, k_ref, v_ref, c_ref, w_ref, b_ref, o_ref, y_ref, xp_ref, *, tq, width):
    s_len = q_ref.shape[0]
    half = width // 2
    rows = min(CONV_ROWS, s_len)
    win = rows + 2 * CONV_PAD
    zeros = jnp.zeros((CONV_PAD, LANES), F32)
    xp_ref[0:CONV_PAD, :] = zeros
    xp_ref[CONV_PAD + s_len:, :] = zeros
    xp_ref[CONV_PAD:CONV_PAD + s_len, :] = c_ref[...].astype(F32)

    def conv_chunk(ci):
        base = ci * rows
        xw = xp_ref[base:base + win, :]
        acc = jnp.zeros((rows, LANES), F32)
        for r in range(SUBLANES):
            shifted = xw if r == 0 else pltpu.roll(xw, win - r, 0)
            for a0 in range(0, 2 * CONV_PAD, SUBLANES):
                k = a0 + r - (CONV_PAD - half)
                if 0 <= k < width:
                    acc = acc + shifted[a0:a0 + rows, :] * w_ref[k:k + 1, :]
        y_ref[base:base + rows, :] = (acc + b_ref[...]).astype(BF16)

    k = k_ref[...]
    v = v_ref[...]
    n_q = s_len // tq
    n_chunks = s_len // rows
    for j in range(n_q):
        qs = slice(j * tq, (j + 1) * tq)
        s = lax.dot_general(q_ref[qs, :], k, (((1,), (1,)), ((), ())), preferred_element_type=F32)
        m = jnp.max(s, axis=-1, keepdims=True)
        p = jnp.exp2(s - m).astype(BF16)
        o = _dot(p, v)
        o_ref[qs, :] = (o[:, :V_HEAD_DIM] / o[:, V_HEAD_DIM:V_HEAD_DIM + 1]).astype(BF16)
        for ci in range(j * n_chunks // n_q, (j + 1) * n_chunks // n_q):
            conv_chunk(ci)


def _attn_conv_call(q, k, v, c, w_dw, b_dw, *, batch, seq, tq):
    t, ch = c.shape
    width = w_dw.shape[0]
    assert width // 2 <= CONV_PAD and ch == MLA_HEADS * LANES
    head = lambda b, h: (b, h)
    chan = lambda b, h: (0, h)
    return pl.pallas_call(
        functools.partial(_attn_conv_kernel, tq=tq, width=width),
        grid=(batch, MLA_HEADS),
        in_specs=[pl.BlockSpec((seq, HEAD_SLOT), head), pl.BlockSpec((seq, HEAD_SLOT), head),
                  pl.BlockSpec((seq, HEAD_SLOT), head), pl.BlockSpec((seq, LANES), head),
                  pl.BlockSpec((width, LANES), chan), pl.BlockSpec((1, LANES), chan)],
        out_specs=[pl.BlockSpec((seq, V_HEAD_DIM), head), pl.BlockSpec((seq, LANES), head)],
        out_shape=[jax.ShapeDtypeStruct((t, MLA_HEADS * V_HEAD_DIM), BF16), jax.ShapeDtypeStruct((t, ch), BF16)],
        scratch_shapes=[pltpu.VMEM((seq + 2 * CONV_PAD, LANES), F32)],
        compiler_params=pltpu.CompilerParams(
            dimension_semantics=("arbitrary", "arbitrary"), vmem_limit_bytes=VMEM_LIMIT),
        name="attn_conv",
    )(q, k, v, c, w_dw, b_dw)


def _oproj_kernel(a_ref, c_ref, x_ref, wo_ref, lg_ref, lb_ref, g2_ref, wr_ref, br_ref, h_out, lg_out):
    na = a_ref.shape[1]
    y = c_ref[...].astype(F32)
    yc = y - jnp.mean(y, axis=-1, keepdims=True)
    z = yc * lax.rsqrt(jnp.mean(yc * yc, axis=-1, keepdims=True) + EPS) * lg_ref[...] + lb_ref[...]
    act = (z * jax.nn.sigmoid(z)).astype(BF16)
    h = (x_ref[...] + _dot(a_ref[...], wo_ref[:na, :].astype(BF16))
         + _dot(act, wo_ref[na:, :].astype(BF16)))
    h_out[...] = h
    hn = _rms(h, g2_ref[...])
    hi = hn.astype(BF16)
    lo = (hn - hi.astype(F32)).astype(BF16)
    r = _dot(hi, wr_ref[...])
    lg_out[...] = r[:, :LANES] + r[:, LANES:] + _dot(lo, wr_ref[:, :LANES]) + br_ref[...]


def _oproj_call(attn, conv, x2, wo, ln_g, ln_b, g2, wr, br, *, tm):
    t, d = x2.shape
    row = lambda i: (i, 0)
    return pl.pallas_call(
        _oproj_kernel,
        grid=(t // tm,),
        in_specs=[pl.BlockSpec((tm, attn.shape[1]), row), pl.BlockSpec((tm, conv.shape[1]), row),
                  pl.BlockSpec((tm, d), row), _whole(wo.shape, True), _whole(ln_g.shape), _whole(ln_b.shape),
                  _whole(g2.shape), _whole(wr.shape, True), _whole(br.shape)],
        out_specs=[pl.BlockSpec((tm, d), row), pl.BlockSpec((tm, LANES), row)],
        out_shape=[jax.ShapeDtypeStruct((t, d), F32), jax.ShapeDtypeStruct((t, LANES), F32)],
        compiler_params=pltpu.CompilerParams(dimension_semantics=("arbitrary",), vmem_limit_bytes=VMEM_LIMIT),
        name="oproj",
    )(attn, conv, x2, wo, ln_g, ln_b, g2, wr, br)


def _route_tokens_on_lanes(lt):
    shape = (SUBLANES, LANES)
    row = lax.broadcasted_iota(I32, shape, 0)
    big = jnp.int32(1 << 20)
    neg = jnp.float32(-jnp.inf)

    def top(v):
        m = jnp.max(v, axis=0, keepdims=True)
        return m, jnp.min(jnp.where(v == m, row, big), axis=0, keepdims=True)

    lgrp = lt[N_EXPERTS:N_EXPERTS + N_EXPERT_GROUPS, :]
    gmax, gsel = top(lgrp)
    p_g = 1.0 / jnp.sum(jnp.exp(lgrp - gmax), axis=0, keepdims=True)
    le = jnp.zeros(shape, F32)
    for g in range(N_EXPERT_GROUPS):
        le = jnp.where(gsel == g, lt[g * EXPERTS_PER_GROUP:(g + 1) * EXPERTS_PER_GROUP, :], le)
    m1, i1 = top(le)
    m2, i2 = top(jnp.where(row == i1, neg, le))
    r = jnp.exp(m2 - m1)
    w1 = 1.0 / (1.0 + r)
    w2 = r / (1.0 + r)
    base = gsel * EXPERTS_PER_GROUP
    return base + i1, base + i2, p_g * w1, p_g * w2


def _lane_cumsum(v):
    lane = lax.broadcasted_iota(I32, v.shape, 1)
    sh = 1
    while sh < LANES:
        v = v + jnp.where(lane >= sh, pltpu.roll(v, sh, 1), 0)
        sh *= 2
    return v


def _route_kernel(lg_ref, d0_out, d1_out, gate_out, meta_out, e_ref, *, rows_per_block):
    n_chunks = d0_out.shape[0]
    shift = rows_per_block.bit_length() - 1
    sub_shift = SUBLANES.bit_length() - 1
    sq = (LANES, LANES)
    row = lax.broadcasted_iota(I32, sq, 0)
    row8 = lax.broadcasted_iota(I32, (SUBLANES, LANES), 0)

    def one_hots(e1, e2):
        return (row == e1).astype(F32), (row == e2).astype(F32)

    def count_step(i, cnt):
        base = pl.multiple_of(i * LANES, LANES)
        e1, e2, g1, g2 = _route_tokens_on_lanes(jnp.transpose(lg_ref[pl.ds(base, LANES), :]))
        e_ref[i] = jnp.where(row8 == 0, e1, jnp.where(row8 == 1, e2, 0))
        gate_out[pl.ds(base, LANES), :] = jnp.transpose(jnp.where(row == 0, g1, jnp.where(row == 1, g2, 0.0)))
        oh1, oh2 = one_hots(e1, e2)
        return cnt + jnp.sum(oh1 + oh2, axis=1, keepdims=True)

    unroll = ROUTE_UNROLL if n_chunks % ROUTE_UNROLL == 0 else 1
    counts_col = lax.fori_loop(0, n_chunks, count_step, jnp.zeros((LANES, 1), F32), unroll=unroll)
    counts = jnp.transpose(jnp.broadcast_to(counts_col, sq))[0:SUBLANES, :].astype(I32)
    padded = ((counts + (SUBLANES - 1)) >> sub_shift) << sub_shift
    pad_end = _lane_cumsum(padded)
    pad_start = pad_end - padded
    start_col = jnp.transpose(jnp.broadcast_to(pad_start[0:1, :].astype(F32), sq))[:, 0:1]

    tri = (row < lax.broadcasted_iota(I32, sq, 1)).astype(BF16)

    def dest_step(i, carry):
        er = e_ref[i]
        oh1, oh2 = one_hots(er[0:1, :], er[1:2, :])
        oh = oh1 + oh2
        pos = carry + _dot(oh.astype(BF16), tri)
        d0_out[i] = jnp.sum(oh1 * pos, axis=0, keepdims=True).astype(I32)
        d1_out[i] = jnp.sum(oh2 * pos, axis=0, keepdims=True).astype(I32)
        return carry + jnp.sum(oh, axis=1, keepdims=True)

    lax.fori_loop(0, n_chunks, dest_step, start_col, unroll=unroll)

    nbp = meta_out.shape[0]
    lane_b = lax.broadcasted_iota(I32, (nbp, LANES), 1)
    blk = lax.broadcasted_iota(I32, (nbp, LANES), 0)
    nblk = (counts + (rows_per_block - 1)) >> shift
    blk_end = _lane_cumsum(nblk)
    bcast = lambda v: jnp.broadcast_to(v[0:1, :], (nbp, LANES))
    be_end, be_start, cn, ps = bcast(blk_end), bcast(blk_end - nblk), bcast(counts), bcast(pad_start)
    is_e = lane_b < N_EXPERTS
    lsum = lambda v: jnp.sum(v, axis=-1, keepdims=True)
    n_used = jnp.max(be_end, axis=-1, keepdims=True)
    total = jnp.max(bcast(pad_end), axis=-1, keepdims=True)
    last_e = jnp.max(jnp.where(is_e & (cn > 0), lane_b, 0), axis=-1, keepdims=True)
    be = jnp.minimum(lsum(jnp.where(is_e & (be_end <= blk), 1, 0)), last_e)
    sel = lane_b == be
    first_row = (blk[:, 0:1] - lsum(jnp.where(sel, be_start, 0))) * rows_per_block
    used = blk[:, 0:1] < n_used
    nvalid = jnp.where(used, jnp.clip(lsum(jnp.where(sel, cn, 0)) - first_row, 0, rows_per_block), 0)
    row0 = jnp.where(used, lsum(jnp.where(sel, ps, 0)) + first_row, 0)
    cols = (be, nvalid, n_used, row0, total)
    meta = jnp.zeros((nbp, LANES), I32)
    for c, v in enumerate(cols):
        meta = jnp.where(lane_b == c, v, meta)
    meta_out[...] = meta


def _route_call(logits, *, rows_per_block, n_blocks):
    t = logits.shape[0]
    assert t % LANES == 0
    nbp = -(-n_blocks // SUBLANES) * SUBLANES
    dshape = (t // LANES, 1, LANES)
    d0, d1, gates, meta = pl.pallas_call(
        functools.partial(_route_kernel, rows_per_block=rows_per_block),
        in_specs=[_whole(logits.shape)],
        out_specs=[_whole(dshape), _whole(dshape), _whole((t, LANES)), _whole((nbp, LANES))],
        out_shape=[jax.ShapeDtypeStruct(dshape, I32), jax.ShapeDtypeStruct(dshape, I32),
                   jax.ShapeDtypeStruct((t, LANES), F32), jax.ShapeDtypeStruct((nbp, LANES), I32)],
        grid=(1,),
        scratch_shapes=[pltpu.VMEM((t // LANES, SUBLANES, LANES), I32)],
        compiler_params=pltpu.CompilerParams(dimension_semantics=("arbitrary",), vmem_limit_bytes=VMEM_LIMIT),
        name="route",
    )(logits)
    return d0.reshape(t), d1.reshape(t), gates, meta


def _pow2_chunks(limit):
    c = 1 << (limit.bit_length() - 1)
    while c >= 1:
        yield c
        c >>= 1


def _expert_kernel(be_ref, nv_ref, nused_ref, row0_ref, total_ref, d0_ref, d1_ref,
                   h_hbm, g2_ref, wg_hbm, wu_hbm, wd_hbm, y_hbm,
                   tok_ref, ord_ref, exp_ref, nexp_ref, xbuf, ybuf, zbuf, wg_buf, wu_buf, wd_buf,
                   gsem, ysem, zsem, wsem):
    b = pl.program_id(0)
    n_used = nused_ref[0]
    groups = xbuf.shape[1]
    rows = groups * SUBLANES
    d = xbuf.shape[3]
    n_tok = d0_ref.shape[0]
    sub_shift = SUBLANES.bit_length() - 1
    n_slots = wg_buf.shape[0]

    def y_copies(blk, slot):
        ng = lax.shift_right_logical(nv_ref[blk] + (SUBLANES - 1), sub_shift)
        g0 = lax.shift_right_logical(row0_ref[blk], sub_shift)
        out = []
        for c in _pow2_chunks(groups):
            off = ng & ~(2 * c - 1)
            copy = pltpu.make_async_copy(ybuf.at[slot, pl.ds(off, c)], y_hbm.at[pl.ds(g0 + off, c)], ysem.at[slot])
            out.append(((ng & c) != 0, copy))
        return out

    def start_y(blk, slot):
        for cond, copy in y_copies(blk, slot):
            pl.when(cond)(copy.start)

    def wait_y(blk, slot):
        for cond, copy in y_copies(blk, slot):
            pl.when(cond)(copy.wait)

    def fill_tail():
        zbuf[...] = jnp.zeros(zbuf.shape, F32)
        first = lax.shift_right_logical(total_ref[0], sub_shift)
        copy = lambda g: pltpu.make_async_copy(zbuf, y_hbm.at[g], zsem.at[0])
        lax.fori_loop(first, y_hbm.shape[0], lambda g, c: (copy(g).start(), c)[1], 0)
        lax.fori_loop(first, y_hbm.shape[0], lambda g, c: (copy(g).wait(), c)[1], 0)

    def weight_copies(j):
        e = exp_ref[j]
        slot = lax.rem(j, n_slots)
        return [pltpu.make_async_copy(src.at[e], dst.at[slot], wsem.at[slot])
                for src, dst in ((wg_hbm, wg_buf), (wu_hbm, wu_buf), (wd_hbm, wd_buf))]

    def start_weights(j):
        for c in weight_copies(j):
            c.start(priority=1)

    def wait_weights(j):
        for c in weight_copies(j):
            c.wait()

    def gather_copy(blk, slot, g, u):
        tok = tok_ref[row0_ref[blk] + g * SUBLANES + u]
        return pltpu.make_async_copy(h_hbm.at[pl.ds(tok, 1)], xbuf.at[slot, g, pl.ds(u, 1)], gsem.at[slot])

    def start_gather(blk, slot):
        n = nv_ref[blk]

        def group(g, c):
            for u in range(SUBLANES):
                gather_copy(blk, slot, g, u).start()
            return c
        full = lax.shift_right_logical(n, sub_shift)
        lax.fori_loop(0, full, group, 0)
        for u in range(SUBLANES - 1):
            @pl.when(full * SUBLANES + u < n)
            def _():
                gather_copy(blk, slot, full, u).start()

    def wait_gather(blk, slot):
        n = nv_ref[blk]
        buf = xbuf.at[slot]
        for c in _pow2_chunks(rows):
            @pl.when((n & c) != 0)
            def _():
                if c >= SUBLANES:
                    part = buf.at[pl.ds(0, c // SUBLANES)]
                else:
                    part = buf.at[0, pl.ds(0, c)]
                pltpu.make_async_copy(part, part, gsem.at[slot]).wait()

    @pl.when(b == 0)
    def _():
        def scan(blk, j):
            e = be_ref[blk]
            is_new = jnp.logical_or(blk == 0, e != be_ref[jnp.maximum(blk - 1, 0)])
            j = j + is_new.astype(I32)
            ord_ref[blk] = j - 1

            @pl.when(is_new)
            def _():
                exp_ref[j - 1] = e
            return j
        n_exp = lax.fori_loop(0, n_used, scan, jnp.int32(0))
        nexp_ref[0] = n_exp
        for j in range(n_slots):
            @pl.when(j < n_exp)
            def _():
                start_weights(j)

        def inv(g, c):
            for u in range(DMA_UNROLL):
                tk = g * DMA_UNROLL + u
                tok_ref[d0_ref[tk]] = tk
                tok_ref[d1_ref[tk]] = tk
            return c
        lax.fori_loop(0, n_tok // DMA_UNROLL, inv, 0)
        xbuf[...] = jnp.zeros(xbuf.shape, F32)
        start_gather(0, 0)

    @pl.when(b < n_used)
    def _():
        slot = b & 1
        wait_gather(b, slot)

        @pl.when(b + 1 < n_used)
        def _():
            start_gather(b + 1, 1 - slot)

        j = ord_ref[b]

        @pl.when(jnp.logical_or(b == 0, ord_ref[jnp.maximum(b - 1, 0)] != j))
        def _():
            wait_weights(j)

            @pl.when(jnp.logical_and(j >= 1, j + (n_slots - 1) < nexp_ref[0]))
            def _():
                start_weights(j + (n_slots - 1))

        ws = lax.rem(j, n_slots)
        hn = _rms(xbuf[slot].reshape(rows, d), g2_ref[...]).astype(BF16)
        gate = _dot(hn, wg_buf[ws].astype(BF16))
        up = _dot(hn, wu_buf[ws].astype(BF16))
        hmid = (gate * jax.nn.sigmoid(gate) * up).astype(BF16)
        y = _dot(hmid, wd_buf[ws].astype(BF16))

        @pl.when(b >= 2)
        def _():
            wait_y(b - 2, slot)

        ybuf[slot] = y.reshape(groups, SUBLANES, d)
        start_y(b, slot)

        @pl.when(b == n_used - 1)
        def _():
            @pl.when(b >= 1)
            def _():
                wait_y(b - 1, 1 - slot)
            wait_y(b, slot)
            fill_tail()


def _expert_call(be, nvalid, n_used, row0, total, dest0, dest1, h, g2, w_gate, w_up, w_down, *,
                 rows_per_block, n_blocks):
    t, d = h.shape
    n_exp, _, f = w_gate.shape
    assert t % DMA_UNROLL == 0 and rows_per_block % SUBLANES == 0 and (2 * t) % SUBLANES == 0
    groups = rows_per_block // SUBLANES
    sorted_groups = 2 * t // SUBLANES + n_exp
    hbm = pl.BlockSpec(memory_space=pl.ANY)
    slots = EXPERT_WEIGHT_SLOTS
    grid_spec = pltpu.PrefetchScalarGridSpec(
        num_scalar_prefetch=7,
        grid=(n_blocks,),
        in_specs=[hbm, pl.BlockSpec(g2.shape, lambda b, *_: (0, 0)), hbm, hbm, hbm],
        out_specs=hbm,
        scratch_shapes=[
            pltpu.SMEM((sorted_groups * SUBLANES,), I32),
            pltpu.SMEM((n_blocks,), I32), pltpu.SMEM((n_exp,), I32), pltpu.SMEM((1,), I32),
            pltpu.VMEM((2, groups, SUBLANES, d), F32), pltpu.VMEM((2, groups, SUBLANES, d), F32),
            pltpu.VMEM((SUBLANES, d), F32),
            pltpu.VMEM((slots, d, f), F32), pltpu.VMEM((slots, d, f), F32), pltpu.VMEM((slots, f, d), F32),
            pltpu.SemaphoreType.DMA((2,)), pltpu.SemaphoreType.DMA((2,)), pltpu.SemaphoreType.DMA((1,)),
            pltpu.SemaphoreType.DMA((slots,)),
        ],
    )
    y = pl.pallas_call(
        _expert_kernel,
        grid_spec=grid_spec,
        out_shape=jax.ShapeDtypeStruct((sorted_groups, SUBLANES, d), F32),
        compiler_params=pltpu.CompilerParams(dimension_semantics=("arbitrary",), vmem_limit_bytes=VMEM_LIMIT),
        name="experts",
    )(be, nvalid, n_used, row0, total, dest0, dest1, h, g2, w_gate, w_up, w_down)
    return y.reshape(sorted_groups * SUBLANES, d)


def _final_kernel(d0_ref, d1_ref, h_ref, y_hbm, gate_ref, fg_ref, o_ref, ybuf, sem):
    i = pl.program_id(0)
    groups = ybuf.shape[2]
    tm = groups * SUBLANES
    d = ybuf.shape[4]

    def start_tile(tile, slot):
        def group(g, c):
            for u in range(SUBLANES):
                tk = tile * tm + g * SUBLANES + u
                for k, dref in enumerate((d0_ref, d1_ref)):
                    pltpu.make_async_copy(y_hbm.at[pl.ds(dref[tk], 1)], ybuf.at[slot, k, g, pl.ds(u, 1)],
                                          sem.at[slot]).start()
            return c
        lax.fori_loop(0, groups, group, 0)

    @pl.when(i == 0)
    def _():
        start_tile(0, 0)

    slot = i & 1

    @pl.when(i + 1 < pl.num_programs(0))
    def _():
        start_tile(i + 1, 1 - slot)

    pltpu.make_async_copy(ybuf.at[slot], ybuf.at[slot], sem.at[slot]).wait()
    y0 = ybuf[slot, 0].reshape(tm, d)
    y1 = ybuf[slot, 1].reshape(tm, d)
    out = h_ref[...] + gate_ref[:, 0:1] * y0 + gate_ref[:, 1:2] * y1
    o_ref[...] = _rms(out, fg_ref[...])


def _final_call(dest0, dest1, h, y, gates, fg, *, tm):
    t, d = h.shape
    assert tm % SUBLANES == 0
    row = lambda i, *_: (i, 0)
    grid_spec = pltpu.PrefetchScalarGridSpec(
        num_scalar_prefetch=2,
        grid=(t // tm,),
        in_specs=[pl.BlockSpec((tm, d), row), pl.BlockSpec(memory_space=pl.ANY),
                  pl.BlockSpec((tm, LANES), row), pl.BlockSpec(fg.shape, lambda i, *_: (0, 0))],
        out_specs=pl.BlockSpec((tm, d), row),
        scratch_shapes=[pltpu.VMEM((2, 2, tm // SUBLANES, SUBLANES, d), F32), pltpu.SemaphoreType.DMA((2,))],
    )
    return pl.pallas_call(
        _final_kernel,
        grid_spec=grid_spec,
        out_shape=jax.ShapeDtypeStruct((t, d), F32),
        compiler_params=pltpu.CompilerParams(dimension_semantics=("arbitrary",), vmem_limit_bytes=VMEM_LIMIT),
        name="final",
    )(dest0, dest1, h, y, gates, fg)


def _wsplit_kernel(lat_ref, u_ref, kr_ref, lat_out, u_out, kr_out, *, n_lat, n_kr):
    j = pl.program_id(0)
    u_out[...] = jnp.transpose(u_ref[...]).astype(BF16)

    @pl.when(j < n_lat)
    def _():
        lat_out[...] = jnp.transpose(lat_ref[...]).astype(BF16)

    @pl.when(j == 0)
    def _():
        kr = jnp.transpose(kr_ref[...])
        lane = lax.broadcasted_iota(I32, kr.shape, 1)
        kr_out[...] = jnp.where(lane < n_kr, kr, 0.0).astype(BF16)


def _wsplit_call(w_t, *, o_kr, o_u):
    cols, d = w_t.shape
    n_u = (cols - o_u) // LANES
    n_lat = o_kr // LANES
    assert o_kr % LANES == 0 and (cols - o_u) % LANES == 0 and o_u % SUBLANES == 0
    assert 0 < o_u - o_kr <= LANES and n_lat <= n_u and o_kr + LANES <= cols
    lat_blk = lambda j: jnp.minimum(j, n_lat - 1)
    return pl.pallas_call(
        functools.partial(_wsplit_kernel, n_lat=n_lat, n_kr=o_u - o_kr),
        grid=(n_u,),
        in_specs=[pl.BlockSpec((LANES, d), lambda j: (lat_blk(j), 0)),
                  pl.BlockSpec((pl.Element(LANES), pl.Element(d)),
                               lambda j: (pl.multiple_of(o_u + j * LANES, SUBLANES), 0)),
                  pl.BlockSpec((pl.Element(LANES), pl.Element(d)), lambda j: (o_kr, 0))],
        out_specs=[pl.BlockSpec((d, LANES), lambda j: (0, lat_blk(j))), pl.BlockSpec((d, LANES), lambda j: (0, j)),
                   pl.BlockSpec((d, LANES), lambda j: (0, 0))],
        out_shape=[jax.ShapeDtypeStruct((d, o_kr), BF16), jax.ShapeDtypeStruct((d, cols - o_u), BF16),
                   jax.ShapeDtypeStruct((d, LANES), BF16)],
        compiler_params=pltpu.CompilerParams(dimension_semantics=("arbitrary",), vmem_limit_bytes=VMEM_LIMIT),
        name="wsplit",
    )(w_t, w_t, w_t)


def _rope_tables(seq):
    pos = np.arange(seq, dtype=np.float64)
    inv_freq = ROPE_THETA ** (-np.arange(0, QK_ROPE_DIM, 2, dtype=np.float64) / QK_ROPE_DIM)
    ang = pos[:, None] * inv_freq[None, :]
    cos, sin = np.cos(ang).astype(np.float32), np.sin(ang).astype(np.float32)
    zero = np.zeros_like(sin)
    cos_t = np.concatenate([cos, cos, cos, cos], axis=1)
    s1_t = np.concatenate([zero, sin, zero, sin], axis=1)
    s2_t = np.concatenate([-sin, zero, -sin, zero], axis=1)
    return jnp.asarray(cos_t), jnp.asarray(s1_t), jnp.asarray(s2_t)


def kernel(x, ln1_g, w_in, b_glu, q_norm_g, w_uq, kv_norm_g, w_ukv, w_dw, b_dw, conv_ln_g, conv_ln_b,
           w_o, ln2_g, w_group, b_group, w_router, b_router, w_gate, w_up, w_down, final_g):
    batch, seq, d = x.shape
    assert ln1_g.shape[0] == 1, "single-layer trunk"
    t = batch * seq
    q_rank = q_norm_g.shape[1]
    kv_rank = kv_norm_g.shape[1]
    x2 = x.reshape(t, d)

    wi = w_in[0]
    o_kr = q_rank + kv_rank
    o_u = o_kr + QK_ROPE_DIM
    wlat, wu, wkr = _wsplit_call(jnp.transpose(wi), o_kr=o_kr, o_u=o_u)
    wuq = w_uq[0].reshape(q_rank, MLA_HEADS, QK_NOPE_DIM + QK_ROPE_DIM)
    wuq = jnp.concatenate([wuq[:, :, :QK_NOPE_DIM].reshape(q_rank, MLA_HEADS * QK_NOPE_DIM),
                           wuq[:, :, QK_NOPE_DIM:].reshape(q_rank, MLA_HEADS * QK_ROPE_DIM)], axis=1).astype(BF16)
    wukv = w_ukv[0]
    wo = w_o[0]
    wr = jnp.concatenate([w_router[0], w_group[0],
                          jnp.zeros((d, LANES - N_EXPERTS - N_EXPERT_GROUPS), F32)], axis=1)
    wr_hi = wr.astype(BF16)
    wr_lo = (wr - wr_hi.astype(F32)).astype(BF16)
    wr2 = jnp.concatenate([wr_hi, wr_lo], axis=1)
    br = jnp.concatenate([b_router[0], b_group[0],
                          jnp.zeros((LANES - N_EXPERTS - N_EXPERT_GROUPS,), F32)])[None, :]
    cos_t, s1_t, s2_t = _rope_tables(seq)

    tm = min(512, seq)
    q, k, v, c = _proj_call(x2, ln1_g, wlat, wu, wkr, b_glu, q_norm_g, kv_norm_g, wuq, wukv, cos_t, s1_t, s2_t,
                            seq=seq, tm=tm)
    attn, conv = _attn_conv_call(q, k, v, c, w_dw[0], b_dw, batch=batch, seq=seq, tq=min(256, seq))
    h, logits = _oproj_call(attn, conv, x2, wo, conv_ln_g, conv_ln_b, ln2_g, wr2, br, tm=tm)

    n_blocks = -(-(2 * t + N_EXPERTS * (MOE_ROWS - 1)) // MOE_ROWS)
    dest0, dest1, gates, meta = _route_call(logits, rows_per_block=MOE_ROWS, n_blocks=n_blocks)
    y = _expert_call(meta[:n_blocks, 0], meta[:n_blocks, 1], meta[0:1, 2], meta[:n_blocks, 3], meta[0:1, 4],
                     dest0, dest1, h, ln2_g, w_gate[0], w_up[0], w_down[0],
                     rows_per_block=MOE_ROWS, n_blocks=n_blocks)
    out = _final_call(dest0, dest1, h, y, gates, final_g[None, :], tm=min(256, seq))
    return out.reshape(batch, seq, d)
```

```python
import functools

import jax
import jax.numpy as jnp
import numpy as np
from jax import lax
from jax.experimental import pallas as pl
from jax.experimental.pallas import tpu as pltpu

F32 = jnp.float32
BF16 = jnp.bfloat16
I32 = jnp.int32

MLA_HEADS = 8
QK_NOPE_DIM = 128
QK_ROPE_DIM = 64
V_HEAD_DIM = 128
ROPE_THETA = 10000.0
N_EXPERT_GROUPS = 8
EXPERTS_PER_GROUP = 8
N_EXPERTS = N_EXPERT_GROUPS * EXPERTS_PER_GROUP
EPS = 1e-6
LOG2E = 1.4426950408889634

LANES = 128
SUBLANES = 8
HEAD_SLOT = 2 * LANES
ROPE_HALF = QK_ROPE_DIM // 2
VMEM_LIMIT = 56 * 1024 * 1024

MOE_ROWS = 256
FINAL_SLOTS = 3
FINAL_UNROLL = 8
ROUTE_UNROLL = 4
EXPERT_WEIGHT_SLOTS = 3
DMA_UNROLL = 8
DMA_UNROLL_LOG2 = DMA_UNROLL.bit_length() - 1


def _rms(x, g):
    return x * lax.rsqrt(jnp.mean(x * x, axis=-1, keepdims=True) + EPS) * g


def _dot(a, b):
    return jnp.dot(a, b, preferred_element_type=F32)


def _whole(shape, single=False):
    mode = dict(pipeline_mode=pl.Buffered(1)) if single else {}
    return pl.BlockSpec(shape, lambda *_: (0,) * len(shape), **mode)


def _proj_kernel(x_ref, g1_ref, wlat_ref, wu_ref, wkr_ref, bglu_ref, qg_ref, kvg_ref, wuq_ref, wukv_ref,
                 cos_ref, s1_ref, s2_ref, q_out, k_out, v_out, c_out, *, q_rank, kv_rank, conv_ch, q_scale):
    xn = _rms(x_ref[...], g1_ref[...]).astype(BF16)
    cos = cos_ref[...]
    s1 = s1_ref[...]
    s2 = s2_ref[...]

    def rope(t):
        return t * cos + pltpu.roll(t, ROPE_HALF, 1) * s1 + pltpu.roll(t, LANES - ROPE_HALF, 1) * s2

    a = _dot(xn, wu_ref[:, :conv_ch]) + bglu_ref[:, :conv_ch]
    gate = _dot(xn, wu_ref[:, conv_ch:]) + bglu_ref[:, conv_ch:]
    c_out[...] = (a * jax.nn.sigmoid(gate)).astype(BF16)

    kr = rope(_dot(xn, wkr_ref[...])).astype(BF16)
    qn = _rms(_dot(xn, wlat_ref[:, :q_rank]), qg_ref[...]).astype(BF16)
    kvn = _rms(_dot(xn, wlat_ref[:, q_rank:q_rank + kv_rank]), kvg_ref[...]).astype(BF16)
    lane = lax.broadcasted_iota(I32, (x_ref.shape[0], LANES), 1)
    ones_blk = (lane == 0).astype(BF16)
    n_nope = MLA_HEADS * QK_NOPE_DIM
    q_nope = _dot(qn, wuq_ref[:, :n_nope])
    q_rope = _dot(qn, wuq_ref[:, n_nope:])
    for h in range(MLA_HEADS):
        c0 = h * HEAD_SLOT
        q_out[:, c0:c0 + LANES] = (q_nope[:, h * QK_NOPE_DIM:(h + 1) * QK_NOPE_DIM] * q_scale).astype(BF16)
        if h % 2 == 0:
            pair = rope(q_rope[:, (h // 2) * LANES:(h // 2 + 1) * LANES]) * q_scale
        half_pair = pair if h % 2 == 0 else pltpu.roll(pair, LANES - QK_ROPE_DIM, 1)
        q_out[:, c0 + LANES:c0 + HEAD_SLOT] = jnp.where(lane < QK_ROPE_DIM, half_pair, 0.0).astype(BF16)
        kvh = _dot(kvn, wukv_ref[:, c0:c0 + HEAD_SLOT].astype(BF16))
        k_out[:, c0:c0 + LANES] = kvh[:, :LANES].astype(BF16)
        k_out[:, c0 + LANES:c0 + HEAD_SLOT] = kr
        v_out[:, c0:c0 + LANES] = kvh[:, LANES:].astype(BF16)
        v_out[:, c0 + LANES:c0 + HEAD_SLOT] = ones_blk


def _proj_call(x2, g1, wlat, wu, wkr, bglu, qg, kvg, wuq, wukv, cos_t, s1_t, s2_t, *, seq, tm):
    t, d = x2.shape
    q_rank, kv_rank = qg.shape[1], kvg.shape[1]
    conv_ch = bglu.shape[1] // 2
    n_pos = seq // tm
    row = lambda i: (i, 0)
    pos = lambda i: (i % n_pos, 0)
    q_scale = float(QK_NOPE_DIM + QK_ROPE_DIM) ** -0.5 * LOG2E
    kern = functools.partial(_proj_kernel, q_rank=q_rank, kv_rank=kv_rank, conv_ch=conv_ch, q_scale=q_scale)
    slot_w = MLA_HEADS * HEAD_SLOT
    return pl.pallas_call(
        kern,
        grid=(t // tm,),
        in_specs=[
            pl.BlockSpec((tm, d), row), _whole(g1.shape), _whole(wlat.shape, True), _whole(wu.shape, True),
            _whole(wkr.shape, True), _whole(bglu.shape),
            _whole(qg.shape), _whole(kvg.shape), _whole(wuq.shape, True), _whole(wukv.shape, True),
            pl.BlockSpec((tm, LANES), pos), pl.BlockSpec((tm, LANES), pos), pl.BlockSpec((tm, LANES), pos),
        ],
        out_specs=[
            pl.BlockSpec((tm, slot_w), row), pl.BlockSpec((tm, slot_w), row),
            pl.BlockSpec((tm, slot_w), row), pl.BlockSpec((tm, conv_ch), row),
        ],
        out_shape=[
            jax.ShapeDtypeStruct((t, slot_w), BF16), jax.ShapeDtypeStruct((t, slot_w), BF16),
            jax.ShapeDtypeStruct((t, slot_w), BF16), jax.ShapeDtypeStruct((t, conv_ch), BF16),
        ],
        compiler_params=pltpu.CompilerParams(dimension_semantics=("arbitrary",), vmem_limit_bytes=VMEM_LIMIT),
        name="proj",
    )(x2, g1, wlat, wu, wkr, bglu, qg, kvg, wuq, wukv, cos_t, s1_t, s2_t)


CONV_PAD = 16
CONV_ROWS = 128


def _attn_conv_kernel(q_ref, k_ref, v_ref, c_ref, w_ref, b_ref, o_ref, y_ref, xp_ref, *, tq, width):
    s_len = q_ref.shape[0]
    half = width // 2
    rows = min(CONV_ROWS, s_len)
    win = rows + 2 * CONV_PAD
    zeros = jnp.zeros((CONV_PAD, LANES), F32)
    xp_ref[0:CONV_PAD, :] = zeros
    xp_ref[CONV_PAD + s_len:, :] = zeros
    xp_ref[CONV_PAD:CONV_PAD + s_len, :] = c_ref[...].astype(F32)

    def conv_chunk(ci):
        base = ci * rows
        xw = xp_ref[base:base + win, :]
        acc = jnp.zeros((rows, LANES), F32)
        for r in range(SUBLANES):
            shifted = xw if r == 0 else pltpu.roll(xw, win - r, 0)
            for a0 in range(0, 2 * CONV_PAD, SUBLANES):
                k = a0 + r - (CONV_PAD - half)
                if 0 <= k < width:
                    acc = acc + shifted[a0:a0 + rows, :] * w_ref[k:k + 1, :]
        y_ref[base:base + rows, :] = (acc + b_ref[...]).astype(BF16)

    k = k_ref[...]
    v = v_ref[...]
    n_q = s_len // tq
    n_chunks = s_len // rows
    for j in range(n_q):
        qs = slice(j * tq, (j + 1) * tq)
        s = lax.dot_general(q_ref[qs, :], k, (((1,), (1,)), ((), ())), preferred_element_type=F32)
        m = jnp.max(s, axis=-1, keepdims=True)
        p = jnp.exp2(s - m).astype(BF16)
        o = _dot(p, v)
        o_ref[qs, :] = (o[:, :V_HEAD_DIM] / o[:, V_HEAD_DIM:V_HEAD_DIM + 1]).astype(BF16)
        for ci in range(j * n_chunks // n_q, (j + 1) * n_chunks // n_q):
            conv_chunk(ci)


def _attn_conv_call(q, k, v, c, w_dw, b_dw, *, batch, seq, tq):
    t, ch = c.shape
    width = w_dw.shape[0]
    assert width // 2 <= CONV_PAD and ch == MLA_HEADS * LANES
    head = lambda b, h: (b, h)
    chan = lambda b, h: (0, h)
    return pl.pallas_call(
        functools.partial(_attn_conv_kernel, tq=tq, width=width),
        grid=(batch, MLA_HEADS),
        in_specs=[pl.BlockSpec((seq, HEAD_SLOT), head), pl.BlockSpec((seq, HEAD_SLOT), head),
                  pl.BlockSpec((seq, HEAD_SLOT), head), pl.BlockSpec((seq, LANES), head),
                  pl.BlockSpec((width, LANES), chan), pl.BlockSpec((1, LANES), chan)],
        out_specs=[pl.BlockSpec((seq, V_HEAD_DIM), head), pl.BlockSpec((seq, LANES), head)],
        out_shape=[jax.ShapeDtypeStruct((t, MLA_HEADS * V_HEAD_DIM), BF16), jax.ShapeDtypeStruct((t, ch), BF16)],
        scratch_shapes=[pltpu.VMEM((seq + 2 * CONV_PAD, LANES), F32)],
        compiler_params=pltpu.CompilerParams(
            dimension_semantics=("arbitrary", "arbitrary"), vmem_limit_bytes=VMEM_LIMIT),
        name="attn_conv",
    )(q, k, v, c, w_dw, b_dw)


def _oproj_kernel(a_ref, c_ref, x_ref, wo_ref, lg_ref, lb_ref, g2_ref, wr_ref, br_ref, h_out, lg_out):
    na = a_ref.shape[1]
    y = c_ref[...].astype(F32)
    yc = y - jnp.mean(y, axis=-1, keepdims=True)
    z = yc * lax.rsqrt(jnp.mean(yc * yc, axis=-1, keepdims=True) + EPS) * lg_ref[...] + lb_ref[...]
    act = (z * jax.nn.sigmoid(z)).astype(BF16)
    h = (x_ref[...] + _dot(a_ref[...], wo_ref[:na, :].astype(BF16))
         + _dot(act, wo_ref[na:, :].astype(BF16)))
    h_out[...] = h
    hn = _rms(h, g2_ref[...])
    hi = hn.astype(BF16)
    lo = (hn - hi.astype(F32)).astype(BF16)
    r = _dot(hi, wr_ref[...])
    lg_out[...] = r[:, :LANES] + r[:, LANES:] + _dot(lo, wr_ref[:, :LANES]) + br_ref[...]


def _oproj_call(attn, conv, x2, wo, ln_g, ln_b, g2, wr, br, *, tm):
    t, d = x2.shape
    row = lambda i: (i, 0)
    return pl.pallas_call(
        _oproj_kernel,
        grid=(t // tm,),
        in_specs=[pl.BlockSpec((tm, attn.shape[1]), row), pl.BlockSpec((tm, conv.shape[1]), row),
                  pl.BlockSpec((tm, d), row), _whole(wo.shape, True), _whole(ln_g.shape), _whole(ln_b.shape),
                  _whole(g2.shape), _whole(wr.shape, True), _whole(br.shape)],
        out_specs=[pl.BlockSpec((tm, d), row), pl.BlockSpec((tm, LANES), row)],
        out_shape=[jax.ShapeDtypeStruct((t, d), F32), jax.ShapeDtypeStruct((t, LANES), F32)],
        compiler_params=pltpu.CompilerParams(dimension_semantics=("arbitrary",), vmem_limit_bytes=VMEM_LIMIT),
        name="oproj",
    )(attn, conv, x2, wo, ln_g, ln_b, g2, wr, br)


def _route_tokens_on_lanes(lt):
    shape = (SUBLANES, LANES)
    row = lax.broadcasted_iota(I32, shape, 0)
    big = jnp.int32(1 << 20)
    neg = jnp.float32(-jnp.inf)

    def top(v):
        m = jnp.max(v, axis=0, keepdims=True)
        return m, jnp.min(jnp.where(v == m, row, big), axis=0, keepdims=True)

    lgrp = lt[N_EXPERTS:N_EXPERTS + N_EXPERT_GROUPS, :]
    gmax, gsel = top(lgrp)
    p_g = 1.0 / jnp.sum(jnp.exp(lgrp - gmax), axis=0, keepdims=True)
    le = jnp.zeros(shape, F32)
    for g in range(N_EXPERT_GROUPS):
        le = jnp.where(gsel == g, lt[g * EXPERTS_PER_GROUP:(g + 1) * EXPERTS_PER_GROUP, :], le)
    m1, i1 = top(le)
    m2, i2 = top(jnp.where(row == i1, neg, le))
    r = jnp.exp(m2 - m1)
    w1 = 1.0 / (1.0 + r)
    w2 = r / (1.0 + r)
    base = gsel * EXPERTS_PER_GROUP
    return base + i1, base + i2, p_g * w1, p_g * w2


def _lane_cumsum(v):
    lane = lax.broadcasted_iota(I32, v.shape, 1)
    sh = 1
    while sh < LANES:
        v = v + jnp.where(lane >= sh, pltpu.roll(v, sh, 1), 0)
        sh *= 2
    return v


def _route_kernel(lg_ref, d0_out, d1_out, gate_out, meta_out, e_ref, *, rows_per_block):
    n_chunks = d0_out.shape[0]
    shift = rows_per_block.bit_length() - 1
    sub_shift = SUBLANES.bit_length() - 1
    sq = (LANES, LANES)
    row = lax.broadcasted_iota(I32, sq, 0)
    row8 = lax.broadcasted_iota(I32, (SUBLANES, LANES), 0)

    def one_hots(e1, e2):
        return (row == e1).astype(F32), (row == e2).astype(F32)

    def count_step(i, cnt):
        base = pl.multiple_of(i * LANES, LANES)
        e1, e2, g1, g2 = _route_tokens_on_lanes(jnp.transpose(lg_ref[pl.ds(base, LANES), :]))
        e_ref[i] = jnp.where(row8 == 0, e1, jnp.where(row8 == 1, e2, 0))
        gate_out[pl.ds(base, LANES), :] = jnp.transpose(jnp.where(row == 0, g1, jnp.where(row == 1, g2, 0.0)))
        oh1, oh2 = one_hots(e1, e2)
        return cnt + jnp.sum(oh1 + oh2, axis=1, keepdims=True)

    unroll = ROUTE_UNROLL if n_chunks % ROUTE_UNROLL == 0 else 1
    counts_col = lax.fori_loop(0, n_chunks, count_step, jnp.zeros((LANES, 1), F32), unroll=unroll)
    counts = jnp.transpose(jnp.broadcast_to(counts_col, sq))[0:SUBLANES, :].astype(I32)
    padded = ((counts + (SUBLANES - 1)) >> sub_shift) << sub_shift
    pad_end = _lane_cumsum(padded)
    pad_start = pad_end - padded
    start_col = jnp.transpose(jnp.broadcast_to(pad_start[0:1, :].astype(F32), sq))[:, 0:1]

    tri = (row < lax.broadcasted_iota(I32, sq, 1)).astype(BF16)

    def dest_step(i, carry):
        er = e_ref[i]
        oh1, oh2 = one_hots(er[0:1, :], er[1:2, :])
        oh = oh1 + oh2
        pos = carry + _dot(oh.astype(BF16), tri)
        d0_out[i] = jnp.sum(oh1 * pos, axis=0, keepdims=True).astype(I32)
        d1_out[i] = jnp.sum(oh2 * pos, axis=0, keepdims=True).astype(I32)
        return carry + jnp.sum(oh, axis=1, keepdims=True)

    lax.fori_loop(0, n_chunks, dest_step, start_col, unroll=unroll)

    nbp = meta_out.shape[0]
    lane_b = lax.broadcasted_iota(I32, (nbp, LANES), 1)
    blk = lax.broadcasted_iota(I32, (nbp, LANES), 0)
    nblk = (counts + (rows_per_block - 1)) >> shift
    blk_end = _lane_cumsum(nblk)
    bcast = lambda v: jnp.broadcast_to(v[0:1, :], (nbp, LANES))
    be_end, be_start, cn, ps = bcast(blk_end), bcast(blk_end - nblk), bcast(counts), bcast(pad_start)
    is_e = lane_b < N_EXPERTS
    lsum = lambda v: jnp.sum(v, axis=-1, keepdims=True)
    n_used = jnp.max(be_end, axis=-1, keepdims=True)
    total = jnp.max(bcast(pad_end), axis=-1, keepdims=True)
    last_e = jnp.max(jnp.where(is_e & (cn > 0), lane_b, 0), axis=-1, keepdims=True)
    be = jnp.minimum(lsum(jnp.where(is_e & (be_end <= blk), 1, 0)), last_e)
    sel = lane_b == be
    first_row = (blk[:, 0:1] - lsum(jnp.where(sel, be_start, 0))) * rows_per_block
    used = blk[:, 0:1] < n_used
    nvalid = jnp.where(used, jnp.clip(lsum(jnp.where(sel, cn, 0)) - first_row, 0, rows_per_block), 0)
    row0 = jnp.where(used, lsum(jnp.where(sel, ps, 0)) + first_row, 0)
    cols = (be, nvalid, n_used, row0, total)
    meta = jnp.zeros((nbp, LANES), I32)
    for c, v in enumerate(cols):
        meta = jnp.where(lane_b == c, v, meta)
    meta_out[...] = meta


def _route_call(logits, *, rows_per_block, n_blocks):
    t = logits.shape[0]
    assert t % LANES == 0
    nbp = -(-n_blocks // SUBLANES) * SUBLANES
    dshape = (t // LANES, 1, LANES)
    d0, d1, gates, meta = pl.pallas_call(
        functools.partial(_route_kernel, rows_per_block=rows_per_block),
        in_specs=[_whole(logits.shape)],
        out_specs=[_whole(dshape), _whole(dshape), _whole((t, LANES)), _whole((nbp, LANES))],
        out_shape=[jax.ShapeDtypeStruct(dshape, I32), jax.ShapeDtypeStruct(dshape, I32),
                   jax.ShapeDtypeStruct((t, LANES), F32), jax.ShapeDtypeStruct((nbp, LANES), I32)],
        grid=(1,),
        scratch_shapes=[pltpu.VMEM((t // LANES, SUBLANES, LANES), I32)],
        compiler_params=pltpu.CompilerParams(dimension_semantics=("arbitrary",), vmem_limit_bytes=VMEM_LIMIT),
        name="route",
    )(logits)
    return d0.reshape(t), d1.reshape(t), gates, meta


def _pow2_chunks(limit):
    c = 1 << (limit.bit_length() - 1)
    while c >= 1:
        yield c
        c >>= 1


def _expert_kernel(be_ref, nv_ref, nused_ref, row0_ref, total_ref, d0_ref, d1_ref,
                   h_hbm, g2_ref, wg_hbm, wu_hbm, wd_hbm, y_hbm,
                   tok_ref, ord_ref, exp_ref, nexp_ref, xbuf, ybuf, zbuf, wg_buf, wu_buf, wd_buf,
                   gsem, ysem, zsem, wsem):
    b = pl.program_id(0)
    n_used = nused_ref[0]
    groups = xbuf.shape[1]
    rows = groups * SUBLANES
    d = xbuf.shape[3]
    n_tok = d0_ref.shape[0]
    sub_shift = SUBLANES.bit_length() - 1
    n_slots = wg_buf.shape[0]

    def y_copies(blk, slot):
        ng = lax.shift_right_logical(nv_ref[blk] + (SUBLANES - 1), sub_shift)
        g0 = lax.shift_right_logical(row0_ref[blk], sub_shift)
        out = []
        for c in _pow2_chunks(groups):
            off = ng & ~(2 * c - 1)
            copy = pltpu.make_async_copy(ybuf.at[slot, pl.ds(off, c)], y_hbm.at[pl.ds(g0 + off, c)], ysem.at[slot])
            out.append(((ng & c) != 0, copy))
        return out

    def start_y(blk, slot):
        for cond, copy in y_copies(blk, slot):
            pl.when(cond)(copy.start)

    def wait_y(blk, slot):
        for cond, copy in y_copies(blk, slot):
            pl.when(cond)(copy.wait)

    def fill_tail():
        zbuf[...] = jnp.zeros(zbuf.shape, F32)
        first = lax.shift_right_logical(total_ref[0], sub_shift)
        copy = lambda g: pltpu.make_async_copy(zbuf, y_hbm.at[g], zsem.at[0])
        lax.fori_loop(first, y_hbm.shape[0], lambda g, c: (copy(g).start(), c)[1], 0)
        lax.fori_loop(first, y_hbm.shape[0], lambda g, c: (copy(g).wait(), c)[1], 0)

    def weight_copies(j):
        e = exp_ref[j]
        slot = lax.rem(j, n_slots)
        return [pltpu.make_async_copy(src.at[e], dst.at[slot], wsem.at[slot])
                for src, dst in ((wg_hbm, wg_buf), (wu_hbm, wu_buf), (wd_hbm, wd_buf))]

    def start_weights(j):
        for c in weight_copies(j):
            c.start(priority=1)

    def wait_weights(j):
        for c in weight_copies(j):
            c.wait()

    def gather_copy(blk, slot, g, u):
        tok = tok_ref[row0_ref[blk] + g * SUBLANES + u]
        return pltpu.make_async_copy(h_hbm.at[pl.ds(tok, 1)], xbuf.at[slot, g, pl.ds(u, 1)], gsem.at[slot])

    def start_gather(blk, slot):
        n = nv_ref[blk]

        def group(g, c):
            for u in range(SUBLANES):
                gather_copy(blk, slot, g, u).start()
            return c
        full = lax.shift_right_logical(n, sub_shift)
        lax.fori_loop(0, full, group, 0)
        for u in range(SUBLANES - 1):
            @pl.when(full * SUBLANES + u < n)
            def _():
                gather_copy(blk, slot, full, u).start()

    def wait_gather(blk, slot):
        n = nv_ref[blk]
        buf = xbuf.at[slot]
        for c in _pow2_chunks(rows):
            @pl.when((n & c) != 0)
            def _():
                if c >= SUBLANES:
                    part = buf.at[pl.ds(0, c // SUBLANES)]
                else:
                    part = buf.at[0, pl.ds(0, c)]
                pltpu.make_async_copy(part, part, gsem.at[slot]).wait()

    @pl.when(b == 0)
    def _():
        def scan(blk, j):
            e = be_ref[blk]
            is_new = jnp.logical_or(blk == 0, e != be_ref[jnp.maximum(blk - 1, 0)])
            j = j + is_new.astype(I32)
            ord_ref[blk] = j - 1

            @pl.when(is_new)
            def _():
                exp_ref[j - 1] = e
            return j
        n_exp = lax.fori_loop(0, n_used, scan, jnp.int32(0))
        nexp_ref[0] = n_exp
        for j in range(n_slots):
            @pl.when(j < n_exp)
            def _():
                start_weights(j)

        def inv(g, c):
            for u in range(DMA_UNROLL):
                tk = g * DMA_UNROLL + u
                tok_ref[d0_ref[tk]] = tk
                tok_ref[d1_ref[tk]] = tk
            return c
        lax.fori_loop(0, n_tok // DMA_UNROLL, inv, 0)
        xbuf[...] = jnp.zeros(xbuf.shape, F32)
        start_gather(0, 0)

    @pl.when(b < n_used)
    def _():
        slot = b & 1
        wait_gather(b, slot)

        @pl.when(b + 1 < n_used)
        def _():
            start_gather(b + 1, 1 - slot)

        j = ord_ref[b]

        @pl.when(jnp.logical_or(b == 0, ord_ref[jnp.maximum(b - 1, 0)] != j))
        def _():
            wait_weights(j)

            @pl.when(jnp.logical_and(j >= 1, j + (n_slots - 1) < nexp_ref[0]))
            def _():
                start_weights(j + (n_slots - 1))

        ws = lax.rem(j, n_slots)
        hn = _rms(xbuf[slot].reshape(rows, d), g2_ref[...]).astype(BF16)
        gate = _dot(hn, wg_buf[ws].astype(BF16))
        up = _dot(hn, wu_buf[ws].astype(BF16))
        hmid = (gate * jax.nn.sigmoid(gate) * up).astype(BF16)
        y = _dot(hmid, wd_buf[ws].astype(BF16))

        @pl.when(b >= 2)
        def _():
            wait_y(b - 2, slot)

        ybuf[slot] = y.reshape(groups, SUBLANES, d)
        start_y(b, slot)

        @pl.when(b == n_used - 1)
        def _():
            @pl.when(b >= 1)
            def _():
                wait_y(b - 1, 1 - slot)
            wait_y(b, slot)
            fill_tail()


def _expert_call(be, nvalid, n_used, row0, total, dest0, dest1, h, g2, w_gate, w_up, w_down, *,
                 rows_per_block, n_blocks):
    t, d = h.shape
    n_exp, _, f = w_gate.shape
    assert t % DMA_UNROLL == 0 and rows_per_block % SUBLANES == 0 and (2 * t) % SUBLANES == 0
    groups = rows_per_block // SUBLANES
    sorted_groups = 2 * t // SUBLANES + n_exp
    hbm = pl.BlockSpec(memory_space=pl.ANY)
    slots = EXPERT_WEIGHT_SLOTS
    grid_spec = pltpu.PrefetchScalarGridSpec(
        num_scalar_prefetch=7,
        grid=(n_blocks,),
        in_specs=[hbm, pl.BlockSpec(g2.shape, lambda b, *_: (0, 0)), hbm, hbm, hbm],
        out_specs=hbm,
        scratch_shapes=[
            pltpu.SMEM((sorted_groups * SUBLANES,), I32),
            pltpu.SMEM((n_blocks,), I32), pltpu.SMEM((n_exp,), I32), pltpu.SMEM((1,), I32),
            pltpu.VMEM((2, groups, SUBLANES, d), F32), pltpu.VMEM((2, groups, SUBLANES, d), F32),
            pltpu.VMEM((SUBLANES, d), F32),
            pltpu.VMEM((slots, d, f), F32), pltpu.VMEM((slots, d, f), F32), pltpu.VMEM((slots, f, d), F32),
            pltpu.SemaphoreType.DMA((2,)), pltpu.SemaphoreType.DMA((2,)), pltpu.SemaphoreType.DMA((1,)),
            pltpu.SemaphoreType.DMA((slots,)),
        ],
    )
    y = pl.pallas_call(
        _expert_kernel,
        grid_spec=grid_spec,
        out_shape=jax.ShapeDtypeStruct((sorted_groups, SUBLANES, d), F32),
        compiler_params=pltpu.CompilerParams(dimension_semantics=("arbitrary",), vmem_limit_bytes=VMEM_LIMIT),
        name="experts",
    )(be, nvalid, n_used, row0, total, dest0, dest1, h, g2, w_gate, w_up, w_down)
    return y.reshape(sorted_groups * SUBLANES, d)


def _final_kernel(d0_ref, d1_ref, h_ref, y_hbm, gate_ref, fg_ref, o_ref, ybuf, sem):
    i = pl.program_id(0)
    n = pl.num_programs(0)
    n_slots = ybuf.shape[0]
    groups = ybuf.shape[2]
    tm = groups * SUBLANES

    def issue_group(tile, slot, g):
        for u in range(SUBLANES):
            tk = tile * tm + g * SUBLANES + u
            for k, dref in enumerate((d0_ref, d1_ref)):
                pltpu.make_async_copy(y_hbm.at[pl.ds(dref[tk], 1)], ybuf.at[slot, k, g, pl.ds(u, 1)],
                                      sem.at[slot]).start()

    def combine_group(slot, g, issue):
        r0 = pl.multiple_of(g * SUBLANES, SUBLANES)
        gt = gate_ref[pl.ds(r0, SUBLANES), :]
        out = h_ref[pl.ds(r0, SUBLANES), :] + gt[:, 0:1] * ybuf[slot, 0, g] + gt[:, 1:2] * ybuf[slot, 1, g]
        issue()
        o_ref[pl.ds(r0, SUBLANES), :] = _rms(out, fg_ref[...])

    @pl.when(i == 0)
    def _():
        for tile in range(n_slots - 1):
            @pl.when(tile < n)
            def _():
                lax.fori_loop(0, groups, lambda g, c: (issue_group(tile, tile, g), c)[1], 0)

    slot = lax.rem(i, n_slots)
    pltpu.make_async_copy(ybuf.at[slot], ybuf.at[slot], sem.at[slot]).wait()
    ahead = i + (n_slots - 1)
    unroll = FINAL_UNROLL if groups % FINAL_UNROLL == 0 else 1

    @pl.when(ahead < n)
    def _():
        nslot = lax.rem(ahead, n_slots)
        lax.fori_loop(0, groups, lambda g, c: (combine_group(
            slot, g, lambda: issue_group(ahead, nslot, g)), c)[1], 0, unroll=unroll)

    @pl.when(ahead >= n)
    def _():
        lax.fori_loop(0, groups, lambda g, c: (combine_group(slot, g, lambda: None), c)[1], 0, unroll=unroll)


def _final_call(dest0, dest1, h, y, gates, fg, *, tm):
    t, d = h.shape
    assert tm % SUBLANES == 0
    row = lambda i, *_: (i, 0)
    grid_spec = pltpu.PrefetchScalarGridSpec(
        num_scalar_prefetch=2,
        grid=(t // tm,),
        in_specs=[pl.BlockSpec((tm, d), row), pl.BlockSpec(memory_space=pl.ANY),
                  pl.BlockSpec((tm, LANES), row), pl.BlockSpec(fg.shape, lambda i, *_: (0, 0))],
        out_specs=pl.BlockSpec((tm, d), row),
        scratch_shapes=[pltpu.VMEM((FINAL_SLOTS, 2, tm // SUBLANES, SUBLANES, d), F32),
                        pltpu.SemaphoreType.DMA((FINAL_SLOTS,))],
    )
    return pl.pallas_call(
        _final_kernel,
        grid_spec=grid_spec,
        out_shape=jax.ShapeDtypeStruct((t, d), F32),
        compiler_params=pltpu.CompilerParams(dimension_semantics=("arbitrary",), vmem_limit_bytes=VMEM_LIMIT),
        name="final",
    )(dest0, dest1, h, y, gates, fg)


def _wsplit_kernel(lat_ref, u_ref, kr_ref, lat_out, u_out, kr_out, *, n_lat, n_kr):
    j = pl.program_id(0)
    u_out[...] = jnp.transpose(u_ref[...]).astype(BF16)

    @pl.when(j < n_lat)
    def _():
        lat_out[...] = jnp.transpose(lat_ref[...]).astype(BF16)

    @pl.when(j == 0)
    def _():
        kr = jnp.transpose(kr_ref[...])
        lane = lax.broadcasted_iota(I32, kr.shape, 1)
        kr_out[...] = jnp.where(lane < n_kr, kr, 0.0).astype(BF16)


def _wsplit_call(w_t, *, o_kr, o_u):
    cols, d = w_t.shape
    n_u = (cols - o_u) // LANES
    n_lat = o_kr // LANES
    assert o_kr % LANES == 0 and (cols - o_u) % LANES == 0 and o_u % SUBLANES == 0
    assert 0 < o_u - o_kr <= LANES and n_lat <= n_u and o_kr + LANES <= cols
    lat_blk = lambda j: jnp.minimum(j, n_lat - 1)
    return pl.pallas_call(
        functools.partial(_wsplit_kernel, n_lat=n_lat, n_kr=o_u - o_kr),
        grid=(n_u,),
        in_specs=[pl.BlockSpec((LANES, d), lambda j: (lat_blk(j), 0)),
                  pl.BlockSpec((pl.Element(LANES), pl.Element(d)),
                               lambda j: (pl.multiple_of(o_u + j * LANES, SUBLANES), 0)),
                  pl.BlockSpec((pl.Element(LANES), pl.Element(d)), lambda j: (o_kr, 0))],
        out_specs=[pl.BlockSpec((d, LANES), lambda j: (0, lat_blk(j))), pl.BlockSpec((d, LANES), lambda j: (0, j)),
                   pl.BlockSpec((d, LANES), lambda j: (0, 0))],
        out_shape=[jax.ShapeDtypeStruct((d, o_kr), BF16), jax.ShapeDtypeStruct((d, cols - o_u), BF16),
                   jax.ShapeDtypeStruct((d, LANES), BF16)],
        compiler_params=pltpu.CompilerParams(dimension_semantics=("arbitrary",), vmem_limit_bytes=VMEM_LIMIT),
        name="wsplit",
    )(w_t, w_t, w_t)


def _rope_tables(seq):
    pos = np.arange(seq, dtype=np.float64)
    inv_freq = ROPE_THETA ** (-np.arange(0, QK_ROPE_DIM, 2, dtype=np.float64) / QK_ROPE_DIM)
    ang = pos[:, None] * inv_freq[None, :]
    cos, sin = np.cos(ang).astype(np.float32), np.sin(ang).astype(np.float32)
    zero = np.zeros_like(sin)
    cos_t = np.concatenate([cos, cos, cos, cos], axis=1)
    s1_t = np.concatenate([zero, sin, zero, sin], axis=1)
    s2_t = np.concatenate([-sin, zero, -sin, zero], axis=1)
    return jnp.asarray(cos_t), jnp.asarray(s1_t), jnp.asarray(s2_t)


def kernel(x, ln1_g, w_in, b_glu, q_norm_g, w_uq, kv_norm_g, w_ukv, w_dw, b_dw, conv_ln_g, conv_ln_b,
           w_o, ln2_g, w_group, b_group, w_router, b_router, w_gate, w_up, w_down, final_g):
    batch, seq, d = x.shape
    assert ln1_g.shape[0] == 1, "single-layer trunk"
    t = batch * seq
    q_rank = q_norm_g.shape[1]
    kv_rank = kv_norm_g.shape[1]
    x2 = x.reshape(t, d)

    wi = w_in[0]
    o_kr = q_rank + kv_rank
    o_u = o_kr + QK_ROPE_DIM
    wlat, wu, wkr = _wsplit_call(jnp.transpose(wi), o_kr=o_kr, o_u=o_u)
    wuq = w_uq[0].reshape(q_rank, MLA_HEADS, QK_NOPE_DIM + QK_ROPE_DIM)
    wuq = jnp.concatenate([wuq[:, :, :QK_NOPE_DIM].reshape(q_rank, MLA_HEADS * QK_NOPE_DIM),
                           wuq[:, :, QK_NOPE_DIM:].reshape(q_rank, MLA_HEADS * QK_ROPE_DIM)], axis=1).astype(BF16)
    wukv = w_ukv[0]
    wo = w_o[0]
    wr = jnp.concatenate([w_router[0], w_group[0],
                          jnp.zeros((d, LANES - N_EXPERTS - N_EXPERT_GROUPS), F32)], axis=1)
    wr_hi = wr.astype(BF16)
    wr_lo = (wr - wr_hi.astype(F32)).astype(BF16)
    wr2 = jnp.concatenate([wr_hi, wr_lo], axis=1)
    br = jnp.concatenate([b_router[0], b_group[0],
                          jnp.zeros((LANES - N_EXPERTS - N_EXPERT_GROUPS,), F32)])[None, :]
    cos_t, s1_t, s2_t = _rope_tables(seq)

    tm = min(512, seq)
    q, k, v, c = _proj_call(x2, ln1_g, wlat, wu, wkr, b_glu, q_norm_g, kv_norm_g, wuq, wukv, cos_t, s1_t, s2_t,
                            seq=seq, tm=tm)
    attn, conv = _attn_conv_call(q, k, v, c, w_dw[0], b_dw, batch=batch, seq=seq, tq=min(256, seq))
    h, logits = _oproj_call(attn, conv, x2, wo, conv_ln_g, conv_ln_b, ln2_g, wr2, br, tm=tm)

    n_blocks = -(-(2 * t + N_EXPERTS * (MOE_ROWS - 1)) // MOE_ROWS)
    dest0, dest1, gates, meta = _route_call(logits, rows_per_block=MOE_ROWS, n_blocks=n_blocks)
    y = _expert_call(meta[:n_blocks, 0], meta[:n_blocks, 1], meta[0:1, 2], meta[:n_blocks, 3], meta[0:1, 4],
                     dest0, dest1, h, ln2_g, w_gate[0], w_up[0], w_down[0],
                     rows_per_block=MOE_ROWS, n_blocks=n_blocks)
    out = _final_call(dest0, dest1, h, y, gates, final_g[None, :], tm=tm)
    return out.reshape(batch, seq, d)
```

```python
import functools

import jax
import jax.numpy as jnp
import numpy as np
from jax import lax
from jax.experimental import pallas as pl
from jax.experimental.pallas import tpu as pltpu

F32 = jnp.float32
BF16 = jnp.bfloat16
I32 = jnp.int32

MLA_HEADS = 8
QK_NOPE_DIM = 128
QK_ROPE_DIM = 64
V_HEAD_DIM = 128
ROPE_THETA = 10000.0
N_EXPERT_GROUPS = 8
EXPERTS_PER_GROUP = 8
N_EXPERTS = N_EXPERT_GROUPS * EXPERTS_PER_GROUP
EPS = 1e-6
LOG2E = 1.4426950408889634

LANES = 128
SUBLANES = 8
HEAD_SLOT = 2 * LANES
ROPE_HALF = QK_ROPE_DIM // 2
VMEM_LIMIT = 56 * 1024 * 1024

MOE_ROWS = 256
FINAL_SLOTS = 3
FINAL_UNROLL = 8
ROUTE_UNROLL = 4
EXPERT_WEIGHT_SLOTS = 3
DMA_UNROLL = 8


def _rms(x, g):
    return x * lax.rsqrt(jnp.mean(x * x, axis=-1, keepdims=True) + EPS) * g


def _dot(a, b):
    return jnp.dot(a, b, preferred_element_type=F32)


def _whole(shape, single=False):
    mode = dict(pipeline_mode=pl.Buffered(1)) if single else {}
    return pl.BlockSpec(shape, lambda *_: (0,) * len(shape), **mode)


def _proj_kernel(x_ref, g1_ref, wlat_ref, wu_ref, wkr_ref, bglu_ref, qg_ref, kvg_ref, wuq_ref, wukv_ref,
                 cos_ref, s1_ref, s2_ref, q_out, k_out, v_out, c_out, *, q_rank, kv_rank, conv_ch, q_scale):
    xn = _rms(x_ref[...], g1_ref[...]).astype(BF16)
    cos = cos_ref[...]
    s1 = s1_ref[...]
    s2 = s2_ref[...]

    def rope(t):
        return t * cos + pltpu.roll(t, ROPE_HALF, 1) * s1 + pltpu.roll(t, LANES - ROPE_HALF, 1) * s2

    a = _dot(xn, wu_ref[:, :conv_ch]) + bglu_ref[:, :conv_ch]
    gate = _dot(xn, wu_ref[:, conv_ch:]) + bglu_ref[:, conv_ch:]
    c_out[...] = (a * jax.nn.sigmoid(gate)).astype(BF16)

    kr = rope(_dot(xn, wkr_ref[...])).astype(BF16)
    qn = _rms(_dot(xn, wlat_ref[:, :q_rank]), qg_ref[...]).astype(BF16)
    kvn = _rms(_dot(xn, wlat_ref[:, q_rank:q_rank + kv_rank]), kvg_ref[...]).astype(BF16)
    lane = lax.broadcasted_iota(I32, (x_ref.shape[0], LANES), 1)
    ones_blk = (lane == 0).astype(BF16)
    n_nope = MLA_HEADS * QK_NOPE_DIM
    q_nope = _dot(qn, wuq_ref[:, :n_nope])
    q_rope = _dot(qn, wuq_ref[:, n_nope:])
    for h in range(MLA_HEADS):
        c0 = h * HEAD_SLOT
        q_out[:, c0:c0 + LANES] = (q_nope[:, h * QK_NOPE_DIM:(h + 1) * QK_NOPE_DIM] * q_scale).astype(BF16)
        if h % 2 == 0:
            pair = rope(q_rope[:, (h // 2) * LANES:(h // 2 + 1) * LANES]) * q_scale
        half_pair = pair if h % 2 == 0 else pltpu.roll(pair, LANES - QK_ROPE_DIM, 1)
        q_out[:, c0 + LANES:c0 + HEAD_SLOT] = jnp.where(lane < QK_ROPE_DIM, half_pair, 0.0).astype(BF16)
        kvh = _dot(kvn, wukv_ref[:, c0:c0 + HEAD_SLOT].astype(BF16))
        k_out[:, c0:c0 + LANES] = kvh[:, :LANES].astype(BF16)
        k_out[:, c0 + LANES:c0 + HEAD_SLOT] = kr
        v_out[:, c0:c0 + LANES] = kvh[:, LANES:].astype(BF16)
        v_out[:, c0 + LANES:c0 + HEAD_SLOT] = ones_blk


def _proj_call(x2, g1, wlat, wu, wkr, bglu, qg, kvg, wuq, wukv, cos_t, s1_t, s2_t, *, seq, tm):
    t, d = x2.shape
    q_rank, kv_rank = qg.shape[1], kvg.shape[1]
    conv_ch = bglu.shape[1] // 2
    n_pos = seq // tm
    row = lambda i: (i, 0)
    pos = lambda i: (i % n_pos, 0)
    q_scale = float(QK_NOPE_DIM + QK_ROPE_DIM) ** -0.5 * LOG2E
    kern = functools.partial(_proj_kernel, q_rank=q_rank, kv_rank=kv_rank, conv_ch=conv_ch, q_scale=q_scale)
    slot_w = MLA_HEADS * HEAD_SLOT
    return pl.pallas_call(
        kern,
        grid=(t // tm,),
        in_specs=[
            pl.BlockSpec((tm, d), row), _whole(g1.shape), _whole(wlat.shape, True), _whole(wu.shape, True),
            _whole(wkr.shape, True), _whole(bglu.shape),
            _whole(qg.shape), _whole(kvg.shape), _whole(wuq.shape, True), _whole(wukv.shape, True),
            pl.BlockSpec((tm, LANES), pos), pl.BlockSpec((tm, LANES), pos), pl.BlockSpec((tm, LANES), pos),
        ],
        out_specs=[
            pl.BlockSpec((tm, slot_w), row), pl.BlockSpec((tm, slot_w), row),
            pl.BlockSpec((tm, slot_w), row), pl.BlockSpec((tm, conv_ch), row),
        ],
        out_shape=[
            jax.ShapeDtypeStruct((t, slot_w), BF16), jax.ShapeDtypeStruct((t, slot_w), BF16),
            jax.ShapeDtypeStruct((t, slot_w), BF16), jax.ShapeDtypeStruct((t, conv_ch), BF16),
        ],
        compiler_params=pltpu.CompilerParams(dimension_semantics=("arbitrary",), vmem_limit_bytes=VMEM_LIMIT),
        name="proj",
    )(x2, g1, wlat, wu, wkr, bglu, qg, kvg, wuq, wukv, cos_t, s1_t, s2_t)


CONV_PAD = 16
CONV_ROWS = 128


def _attn_conv_kernel(q_ref, k_ref, v_ref, c_ref, w_ref, b_ref, o_ref, y_ref, xp_ref, *, tq, width):
    s_len = q_ref.shape[0]
    half = width // 2
    rows = min(CONV_ROWS, s_len)
    win = rows + 2 * CONV_PAD
    zeros = jnp.zeros((CONV_PAD, LANES), F32)
    xp_ref[0:CONV_PAD, :] = zeros
    xp_ref[CONV_PAD + s_len:, :] = zeros
    xp_ref[CONV_PAD:CONV_PAD + s_len, :] = c_ref[...].astype(F32)

    def conv_chunk(ci):
        base = ci * rows
        xw = xp_ref[base:base + win, :]
        acc = jnp.zeros((rows, LANES), F32)
        for r in range(SUBLANES):
            shifted = xw if r == 0 else pltpu.roll(xw, win - r, 0)
            for a0 in range(0, 2 * CONV_PAD, SUBLANES):
                k = a0 + r - (CONV_PAD - half)
                if 0 <= k < width:
                    acc = acc + shifted[a0:a0 + rows, :] * w_ref[k:k + 1, :]
        y_ref[base:base + rows, :] = (acc + b_ref[...]).astype(BF16)

    k = k_ref[...]
    v = v_ref[...]
    n_q = s_len // tq
    n_chunks = s_len // rows
    for j in range(n_q):
        qs = slice(j * tq, (j + 1) * tq)
        s = lax.dot_general(q_ref[qs, :], k, (((1,), (1,)), ((), ())), preferred_element_type=F32)
        m = jnp.max(s, axis=-1, keepdims=True)
        p = jnp.exp2(s - m).astype(BF16)
        o = _dot(p, v)
        o_ref[qs, :] = (o[:, :V_HEAD_DIM] / o[:, V_HEAD_DIM:V_HEAD_DIM + 1]).astype(BF16)
        for ci in range(j * n_chunks // n_q, (j + 1) * n_chunks // n_q):
            conv_chunk(ci)


def _attn_conv_call(q, k, v, c, w_dw, b_dw, *, batch, seq, tq):
    t, ch = c.shape
    width = w_dw.shape[0]
    assert width // 2 <= CONV_PAD and ch == MLA_HEADS * LANES
    head = lambda b, h: (b, h)
    chan = lambda b, h: (0, h)
    return pl.pallas_call(
        functools.partial(_attn_conv_kernel, tq=tq, width=width),
        grid=(batch, MLA_HEADS),
        in_specs=[pl.BlockSpec((seq, HEAD_SLOT), head), pl.BlockSpec((seq, HEAD_SLOT), head),
                  pl.BlockSpec((seq, HEAD_SLOT), head), pl.BlockSpec((seq, LANES), head),
                  pl.BlockSpec((width, LANES), chan), pl.BlockSpec((1, LANES), chan)],
        out_specs=[pl.BlockSpec((seq, V_HEAD_DIM), head), pl.BlockSpec((seq, LANES), head)],
        out_shape=[jax.ShapeDtypeStruct((t, MLA_HEADS * V_HEAD_DIM), BF16), jax.ShapeDtypeStruct((t, ch), BF16)],
        scratch_shapes=[pltpu.VMEM((seq + 2 * CONV_PAD, LANES), F32)],
        compiler_params=pltpu.CompilerParams(
            dimension_semantics=("arbitrary", "arbitrary"), vmem_limit_bytes=VMEM_LIMIT),
        name="attn_conv",
    )(q, k, v, c, w_dw, b_dw)


def _oproj_kernel(a_ref, c_ref, x_ref, wo_ref, lg_ref, lb_ref, g2_ref, wr_ref, br_ref, h_out, lg_out):
    na = a_ref.shape[1]
    y = c_ref[...].astype(F32)
    yc = y - jnp.mean(y, axis=-1, keepdims=True)
    z = yc * lax.rsqrt(jnp.mean(yc * yc, axis=-1, keepdims=True) + EPS) * lg_ref[...] + lb_ref[...]
    act = (z * jax.nn.sigmoid(z)).astype(BF16)
    h = (x_ref[...] + _dot(a_ref[...], wo_ref[:na, :].astype(BF16))
         + _dot(act, wo_ref[na:, :].astype(BF16)))
    h_out[...] = h
    hn = _rms(h, g2_ref[...])
    hi = hn.astype(BF16)
    lo = (hn - hi.astype(F32)).astype(BF16)
    r = _dot(hi, wr_ref[...])
    lg_out[...] = r[:, :LANES] + r[:, LANES:] + _dot(lo, wr_ref[:, :LANES]) + br_ref[...]


def _oproj_call(attn, conv, x2, wo, ln_g, ln_b, g2, wr, br, *, tm):
    t, d = x2.shape
    row = lambda i: (i, 0)
    return pl.pallas_call(
        _oproj_kernel,
        grid=(t // tm,),
        in_specs=[pl.BlockSpec((tm, attn.shape[1]), row), pl.BlockSpec((tm, conv.shape[1]), row),
                  pl.BlockSpec((tm, d), row), _whole(wo.shape, True), _whole(ln_g.shape), _whole(ln_b.shape),
                  _whole(g2.shape), _whole(wr.shape, True), _whole(br.shape)],
        out_specs=[pl.BlockSpec((tm, d), row), pl.BlockSpec((tm, LANES), row)],
        out_shape=[jax.ShapeDtypeStruct((t, d), F32), jax.ShapeDtypeStruct((t, LANES), F32)],
        compiler_params=pltpu.CompilerParams(dimension_semantics=("arbitrary",), vmem_limit_bytes=VMEM_LIMIT),
        name="oproj",
    )(attn, conv, x2, wo, ln_g, ln_b, g2, wr, br)


def _route_tokens_on_lanes(lt):
    shape = (SUBLANES, LANES)
    row = lax.broadcasted_iota(I32, shape, 0)
    big = jnp.int32(1 << 20)
    neg = jnp.float32(-jnp.inf)

    def top(v):
        m = jnp.max(v, axis=0, keepdims=True)
        return m, jnp.min(jnp.where(v == m, row, big), axis=0, keepdims=True)

    lgrp = lt[N_EXPERTS:N_EXPERTS + N_EXPERT_GROUPS, :]
    gmax, gsel = top(lgrp)
    p_g = 1.0 / jnp.sum(jnp.exp(lgrp - gmax), axis=0, keepdims=True)
    le = jnp.zeros(shape, F32)
    for g in range(N_EXPERT_GROUPS):
        le = jnp.where(gsel == g, lt[g * EXPERTS_PER_GROUP:(g + 1) * EXPERTS_PER_GROUP, :], le)
    m1, i1 = top(le)
    m2, i2 = top(jnp.where(row == i1, neg, le))
    r = jnp.exp(m2 - m1)
    w1 = 1.0 / (1.0 + r)
    w2 = r / (1.0 + r)
    base = gsel * EXPERTS_PER_GROUP
    return base + i1, base + i2, p_g * w1, p_g * w2


def _lane_cumsum(v):
    lane = lax.broadcasted_iota(I32, v.shape, 1)
    sh = 1
    while sh < LANES:
        v = v + jnp.where(lane >= sh, pltpu.roll(v, sh, 1), 0)
        sh *= 2
    return v


def _route_kernel(lg_ref, d0_out, d1_out, gate_out, meta_out, e_ref, *, rows_per_block):
    n_chunks = d0_out.shape[0]
    shift = rows_per_block.bit_length() - 1
    sub_shift = SUBLANES.bit_length() - 1
    sq = (LANES, LANES)
    row = lax.broadcasted_iota(I32, sq, 0)
    row8 = lax.broadcasted_iota(I32, (SUBLANES, LANES), 0)

    def one_hots(e1, e2):
        return (row == e1).astype(F32), (row == e2).astype(F32)

    def count_step(i, cnt):
        base = pl.multiple_of(i * LANES, LANES)
        e1, e2, g1, g2 = _route_tokens_on_lanes(jnp.transpose(lg_ref[pl.ds(base, LANES), :]))
        e_ref[i] = jnp.where(row8 == 0, e1, jnp.where(row8 == 1, e2, 0))
        gate_out[pl.ds(base, LANES), :] = jnp.transpose(jnp.where(row == 0, g1, jnp.where(row == 1, g2, 0.0)))
        oh1, oh2 = one_hots(e1, e2)
        return cnt + jnp.sum(oh1 + oh2, axis=1, keepdims=True)

    unroll = ROUTE_UNROLL if n_chunks % ROUTE_UNROLL == 0 else 1
    counts_col = lax.fori_loop(0, n_chunks, count_step, jnp.zeros((LANES, 1), F32), unroll=unroll)
    counts = jnp.transpose(jnp.broadcast_to(counts_col, sq))[0:SUBLANES, :].astype(I32)
    padded = ((counts + (SUBLANES - 1)) >> sub_shift) << sub_shift
    pad_end = _lane_cumsum(padded)
    pad_start = pad_end - padded
    start_col = jnp.transpose(jnp.broadcast_to(pad_start[0:1, :].astype(F32), sq))[:, 0:1]

    tri = (row < lax.broadcasted_iota(I32, sq, 1)).astype(BF16)

    def dest_step(i, carry):
        er = e_ref[i]
        oh1, oh2 = one_hots(er[0:1, :], er[1:2, :])
        oh = oh1 + oh2
        pos = carry + _dot(oh.astype(BF16), tri)
        d0_out[i] = jnp.sum(oh1 * pos, axis=0, keepdims=True).astype(I32)
        d1_out[i] = jnp.sum(oh2 * pos, axis=0, keepdims=True).astype(I32)
        return carry + jnp.sum(oh, axis=1, keepdims=True)

    lax.fori_loop(0, n_chunks, dest_step, start_col, unroll=unroll)

    nbp = meta_out.shape[0]
    lane_b = lax.broadcasted_iota(I32, (nbp, LANES), 1)
    blk = lax.broadcasted_iota(I32, (nbp, LANES), 0)
    nblk = (counts + (rows_per_block - 1)) >> shift
    blk_end = _lane_cumsum(nblk)
    bcast = lambda v: jnp.broadcast_to(v[0:1, :], (nbp, LANES))
    be_end, be_start, cn, ps = bcast(blk_end), bcast(blk_end - nblk), bcast(counts), bcast(pad_start)
    is_e = lane_b < N_EXPERTS
    lsum = lambda v: jnp.sum(v, axis=-1, keepdims=True)
    n_used = jnp.max(be_end, axis=-1, keepdims=True)
    total = jnp.max(bcast(pad_end), axis=-1, keepdims=True)
    last_e = jnp.max(jnp.where(is_e & (cn > 0), lane_b, 0), axis=-1, keepdims=True)
    be = jnp.minimum(lsum(jnp.where(is_e & (be_end <= blk), 1, 0)), last_e)
    sel = lane_b == be
    first_row = (blk[:, 0:1] - lsum(jnp.where(sel, be_start, 0))) * rows_per_block
    used = blk[:, 0:1] < n_used
    nvalid = jnp.where(used, jnp.clip(lsum(jnp.where(sel, cn, 0)) - first_row, 0, rows_per_block), 0)
    row0 = jnp.where(used, lsum(jnp.where(sel, ps, 0)) + first_row, 0)
    cols = (be, nvalid, n_used, row0, total)
    meta = jnp.zeros((nbp, LANES), I32)
    for c, v in enumerate(cols):
        meta = jnp.where(lane_b == c, v, meta)
    meta_out[...] = meta


def _route_call(logits, *, rows_per_block, n_blocks):
    t = logits.shape[0]
    assert t % LANES == 0
    nbp = -(-n_blocks // SUBLANES) * SUBLANES
    dshape = (t // LANES, 1, LANES)
    d0, d1, gates, meta = pl.pallas_call(
        functools.partial(_route_kernel, rows_per_block=rows_per_block),
        in_specs=[_whole(logits.shape)],
        out_specs=[_whole(dshape), _whole(dshape), _whole((t, LANES)), _whole((nbp, LANES))],
        out_shape=[jax.ShapeDtypeStruct(dshape, I32), jax.ShapeDtypeStruct(dshape, I32),
                   jax.ShapeDtypeStruct((t, LANES), F32), jax.ShapeDtypeStruct((nbp, LANES), I32)],
        grid=(1,),
        scratch_shapes=[pltpu.VMEM((t // LANES, SUBLANES, LANES), I32)],
        compiler_params=pltpu.CompilerParams(dimension_semantics=("arbitrary",), vmem_limit_bytes=VMEM_LIMIT),
        name="route",
    )(logits)
    return d0.reshape(t), d1.reshape(t), gates, meta


def _pow2_chunks(limit):
    c = 1 << (limit.bit_length() - 1)
    while c >= 1:
        yield c
        c >>= 1


def _expert_kernel(be_ref, nv_ref, nused_ref, row0_ref, total_ref, d0_ref, d1_ref,
                   h_hbm, g2_ref, wg_hbm, wu_hbm, wd_hbm, y_hbm,
                   tok_ref, ord_ref, exp_ref, nexp_ref, xbuf, ybuf, zbuf, wg_buf, wu_buf, wd_buf,
                   gsem, ysem, zsem, wsem):
    b = pl.program_id(0)
    n_used = nused_ref[0]
    groups = xbuf.shape[1]
    rows = groups * SUBLANES
    d = xbuf.shape[3]
    n_tok = d0_ref.shape[0]
    sub_shift = SUBLANES.bit_length() - 1
    n_slots = wg_buf.shape[0]

    def y_copies(blk, slot):
        ng = lax.shift_right_logical(nv_ref[blk] + (SUBLANES - 1), sub_shift)
        g0 = lax.shift_right_logical(row0_ref[blk], sub_shift)
        out = []
        for c in _pow2_chunks(groups):
            off = ng & ~(2 * c - 1)
            copy = pltpu.make_async_copy(ybuf.at[slot, pl.ds(off, c)], y_hbm.at[pl.ds(g0 + off, c)], ysem.at[slot])
            out.append(((ng & c) != 0, copy))
        return out

    def start_y(blk, slot):
        for cond, copy in y_copies(blk, slot):
            pl.when(cond)(copy.start)

    def wait_y(blk, slot):
        for cond, copy in y_copies(blk, slot):
            pl.when(cond)(copy.wait)

    def fill_tail():
        zbuf[...] = jnp.zeros(zbuf.shape, F32)
        first = lax.shift_right_logical(total_ref[0], sub_shift)
        copy = lambda g: pltpu.make_async_copy(zbuf, y_hbm.at[g], zsem.at[0])
        lax.fori_loop(first, y_hbm.shape[0], lambda g, c: (copy(g).start(), c)[1], 0)
        lax.fori_loop(first, y_hbm.shape[0], lambda g, c: (copy(g).wait(), c)[1], 0)

    def weight_copies(j):
        e = exp_ref[j]
        slot = lax.rem(j, n_slots)
        return [pltpu.make_async_copy(src.at[e], dst.at[slot], wsem.at[slot])
                for src, dst in ((wg_hbm, wg_buf), (wu_hbm, wu_buf), (wd_hbm, wd_buf))]

    def start_weights(j):
        for c in weight_copies(j):
            c.start(priority=1)

    def wait_weights(j):
        for c in weight_copies(j):
            c.wait()

    def gather_copy(blk, slot, g, u):
        tok = tok_ref[row0_ref[blk] + g * SUBLANES + u]
        return pltpu.make_async_copy(h_hbm.at[pl.ds(tok, 1)], xbuf.at[slot, g, pl.ds(u, 1)], gsem.at[slot])

    def start_gather(blk, slot):
        n = nv_ref[blk]

        def group(g, c):
            for u in range(SUBLANES):
                gather_copy(blk, slot, g, u).start()
            return c
        full = lax.shift_right_logical(n, sub_shift)
        lax.fori_loop(0, full, group, 0)
        for u in range(SUBLANES - 1):
            @pl.when(full * SUBLANES + u < n)
            def _():
                gather_copy(blk, slot, full, u).start()

    def wait_gather(blk, slot):
        n = nv_ref[blk]
        buf = xbuf.at[slot]
        for c in _pow2_chunks(rows):
            @pl.when((n & c) != 0)
            def _():
                if c >= SUBLANES:
                    part = buf.at[pl.ds(0, c // SUBLANES)]
                else:
                    part = buf.at[0, pl.ds(0, c)]
                pltpu.make_async_copy(part, part, gsem.at[slot]).wait()

    @pl.when(b == 0)
    def _():
        def scan(blk, j):
            e = be_ref[blk]
            is_new = jnp.logical_or(blk == 0, e != be_ref[jnp.maximum(blk - 1, 0)])
            j = j + is_new.astype(I32)
            ord_ref[blk] = j - 1

            @pl.when(is_new)
            def _():
                exp_ref[j - 1] = e
            return j
        n_exp = lax.fori_loop(0, n_used, scan, jnp.int32(0))
        nexp_ref[0] = n_exp
        for j in range(n_slots):
            @pl.when(j < n_exp)
            def _():
                start_weights(j)

        def inv(g, c):
            for u in range(DMA_UNROLL):
                tk = g * DMA_UNROLL + u
                tok_ref[d0_ref[tk]] = tk
                tok_ref[d1_ref[tk]] = tk
            return c
        lax.fori_loop(0, n_tok // DMA_UNROLL, inv, 0)
        xbuf[...] = jnp.zeros(xbuf.shape, F32)
        start_gather(0, 0)

    @pl.when(b < n_used)
    def _():
        slot = b & 1
        wait_gather(b, slot)

        @pl.when(b + 1 < n_used)
        def _():
            start_gather(b + 1, 1 - slot)

        j = ord_ref[b]

        @pl.when(jnp.logical_or(b == 0, ord_ref[jnp.maximum(b - 1, 0)] != j))
        def _():
            wait_weights(j)

            @pl.when(jnp.logical_and(j >= 1, j + (n_slots - 1) < nexp_ref[0]))
            def _():
                start_weights(j + (n_slots - 1))

        ws = lax.rem(j, n_slots)
        hn = _rms(xbuf[slot].reshape(rows, d), g2_ref[...]).astype(BF16)
        gate = _dot(hn, wg_buf[ws].astype(BF16))
        up = _dot(hn, wu_buf[ws].astype(BF16))
        hmid = (gate * jax.nn.sigmoid(gate) * up).astype(BF16)
        y = _dot(hmid, wd_buf[ws].astype(BF16))

        @pl.when(b >= 2)
        def _():
            wait_y(b - 2, slot)

        ybuf[slot] = y.reshape(groups, SUBLANES, d)
        start_y(b, slot)

        @pl.when(b == n_used - 1)
        def _():
            @pl.when(b >= 1)
            def _():
                wait_y(b - 1, 1 - slot)
            wait_y(b, slot)
            fill_tail()


def _expert_call(be, nvalid, n_used, row0, total, dest0, dest1, h, g2, w_gate, w_up, w_down, *,
                 rows_per_block, n_blocks):
    t, d = h.shape
    n_exp, _, f = w_gate.shape
    assert t % DMA_UNROLL == 0 and rows_per_block % SUBLANES == 0 and (2 * t) % SUBLANES == 0
    groups = rows_per_block // SUBLANES
    sorted_groups = 2 * t // SUBLANES + n_exp
    hbm = pl.BlockSpec(memory_space=pl.ANY)
    slots = EXPERT_WEIGHT_SLOTS
    grid_spec = pltpu.PrefetchScalarGridSpec(
        num_scalar_prefetch=7,
        grid=(n_blocks,),
        in_specs=[hbm, pl.BlockSpec(g2.shape, lambda b, *_: (0, 0)), hbm, hbm, hbm],
        out_specs=hbm,
        scratch_shapes=[
            pltpu.SMEM((sorted_groups * SUBLANES,), I32),
            pltpu.SMEM((n_blocks,), I32), pltpu.SMEM((n_exp,), I32), pltpu.SMEM((1,), I32),
            pltpu.VMEM((2, groups, SUBLANES, d), F32), pltpu.VMEM((2, groups, SUBLANES, d), F32),
            pltpu.VMEM((SUBLANES, d), F32),
            pltpu.VMEM((slots, d, f), F32), pltpu.VMEM((slots, d, f), F32), pltpu.VMEM((slots, f, d), F32),
            pltpu.SemaphoreType.DMA((2,)), pltpu.SemaphoreType.DMA((2,)), pltpu.SemaphoreType.DMA((1,)),
            pltpu.SemaphoreType.DMA((slots,)),
        ],
    )
    y = pl.pallas_call(
        _expert_kernel,
        grid_spec=grid_spec,
        out_shape=jax.ShapeDtypeStruct((sorted_groups, SUBLANES, d), F32),
        compiler_params=pltpu.CompilerParams(dimension_semantics=("arbitrary",), vmem_limit_bytes=VMEM_LIMIT),
        name="experts",
    )(be, nvalid, n_used, row0, total, dest0, dest1, h, g2, w_gate, w_up, w_down)
    return y.reshape(sorted_groups * SUBLANES, d)


def _final_kernel(d0_ref, d1_ref, h_ref, y_hbm, gate_ref, fg_ref, o_ref, ybuf, sem):
    i = pl.program_id(0)
    n = pl.num_programs(0)
    n_slots = ybuf.shape[0]
    groups = ybuf.shape[2]
    tm = groups * SUBLANES

    def issue_group(tile, slot, g):
        for u in range(SUBLANES):
            tk = tile * tm + g * SUBLANES + u
            for k, dref in enumerate((d0_ref, d1_ref)):
                pltpu.make_async_copy(y_hbm.at[pl.ds(dref[tk], 1)], ybuf.at[slot, k, g, pl.ds(u, 1)],
                                      sem.at[slot]).start()

    def combine_group(slot, g, issue):
        r0 = pl.multiple_of(g * SUBLANES, SUBLANES)
        gt = gate_ref[pl.ds(r0, SUBLANES), :]
        out = h_ref[pl.ds(r0, SUBLANES), :] + gt[:, 0:1] * ybuf[slot, 0, g] + gt[:, 1:2] * ybuf[slot, 1, g]
        issue()
        o_ref[pl.ds(r0, SUBLANES), :] = _rms(out, fg_ref[...])

    @pl.when(i == 0)
    def _():
        for tile in range(n_slots - 1):
            @pl.when(tile < n)
            def _():
                lax.fori_loop(0, groups, lambda g, c: (issue_group(tile, tile, g), c)[1], 0)

    slot = lax.rem(i, n_slots)
    pltpu.make_async_copy(ybuf.at[slot], ybuf.at[slot], sem.at[slot]).wait()
    ahead = i + (n_slots - 1)
    unroll = FINAL_UNROLL if groups % FINAL_UNROLL == 0 else 1

    @pl.when(ahead < n)
    def _():
        nslot = lax.rem(ahead, n_slots)
        lax.fori_loop(0, groups, lambda g, c: (combine_group(
            slot, g, lambda: issue_group(ahead, nslot, g)), c)[1], 0, unroll=unroll)

    @pl.when(ahead >= n)
    def _():
        lax.fori_loop(0, groups, lambda g, c: (combine_group(slot, g, lambda: None), c)[1], 0, unroll=unroll)


def _final_call(dest0, dest1, h, y, gates, fg, *, tm):
    t, d = h.shape
    assert tm % SUBLANES == 0
    row = lambda i, *_: (i, 0)
    grid_spec = pltpu.PrefetchScalarGridSpec(
        num_scalar_prefetch=2,
        grid=(t // tm,),
        in_specs=[pl.BlockSpec((tm, d), row), pl.BlockSpec(memory_space=pl.ANY),
                  pl.BlockSpec((tm, LANES), row), pl.BlockSpec(fg.shape, lambda i, *_: (0, 0))],
        out_specs=pl.BlockSpec((tm, d), row),
        scratch_shapes=[pltpu.VMEM((FINAL_SLOTS, 2, tm // SUBLANES, SUBLANES, d), F32),
                        pltpu.SemaphoreType.DMA((FINAL_SLOTS,))],
    )
    return pl.pallas_call(
        _final_kernel,
        grid_spec=grid_spec,
        out_shape=jax.ShapeDtypeStruct((t, d), F32),
        compiler_params=pltpu.CompilerParams(dimension_semantics=("arbitrary",), vmem_limit_bytes=VMEM_LIMIT),
        name="final",
    )(dest0, dest1, h, y, gates, fg)


def _wsplit_kernel(lat_ref, u_ref, kr_ref, lat_out, u_out, kr_out, *, n_lat, n_kr):
    j = pl.program_id(0)
    u_out[...] = jnp.transpose(u_ref[...]).astype(BF16)

    @pl.when(j < n_lat)
    def _():
        lat_out[...] = jnp.transpose(lat_ref[...]).astype(BF16)

    @pl.when(j == 0)
    def _():
        kr = jnp.transpose(kr_ref[...])
        lane = lax.broadcasted_iota(I32, kr.shape, 1)
        kr_out[...] = jnp.where(lane < n_kr, kr, 0.0).astype(BF16)


def _wsplit_call(w_t, *, o_kr, o_u):
    cols, d = w_t.shape
    assert o_kr % LANES == 0 and (cols - o_u) % LANES == 0 and o_u % SUBLANES == 0
    blk = 2 * LANES if o_kr % (2 * LANES) == 0 and (cols - o_u) % (2 * LANES) == 0 else LANES
    n_u = (cols - o_u) // blk
    n_lat = o_kr // blk
    assert 0 < o_u - o_kr <= LANES and n_lat <= n_u and o_kr + LANES <= cols
    lat_blk = lambda j: jnp.minimum(j, n_lat - 1)
    return pl.pallas_call(
        functools.partial(_wsplit_kernel, n_lat=n_lat, n_kr=o_u - o_kr),
        grid=(n_u,),
        in_specs=[pl.BlockSpec((blk, d), lambda j: (lat_blk(j), 0)),
                  pl.BlockSpec((pl.Element(blk), pl.Element(d)),
                               lambda j: (pl.multiple_of(o_u + j * blk, SUBLANES), 0)),
                  pl.BlockSpec((pl.Element(LANES), pl.Element(d)), lambda j: (o_kr, 0))],
        out_specs=[pl.BlockSpec((d, blk), lambda j: (0, lat_blk(j))), pl.BlockSpec((d, blk), lambda j: (0, j)),
                   pl.BlockSpec((d, LANES), lambda j: (0, 0))],
        out_shape=[jax.ShapeDtypeStruct((d, o_kr), BF16), jax.ShapeDtypeStruct((d, cols - o_u), BF16),
                   jax.ShapeDtypeStruct((d, LANES), BF16)],
        compiler_params=pltpu.CompilerParams(dimension_semantics=("arbitrary",), vmem_limit_bytes=VMEM_LIMIT),
        name="wsplit",
    )(w_t, w_t, w_t)


def _rope_tables(seq):
    pos = np.arange(seq, dtype=np.float64)
    inv_freq = ROPE_THETA ** (-np.arange(0, QK_ROPE_DIM, 2, dtype=np.float64) / QK_ROPE_DIM)
    ang = pos[:, None] * inv_freq[None, :]
    cos, sin = np.cos(ang).astype(np.float32), np.sin(ang).astype(np.float32)
    zero = np.zeros_like(sin)
    cos_t = np.concatenate([cos, cos, cos, cos], axis=1)
    s1_t = np.concatenate([zero, sin, zero, sin], axis=1)
    s2_t = np.concatenate([-sin, zero, -sin, zero], axis=1)
    return jnp.asarray(cos_t), jnp.asarray(s1_t), jnp.asarray(s2_t)


def kernel(x, ln1_g, w_in, b_glu, q_norm_g, w_uq, kv_norm_g, w_ukv, w_dw, b_dw, conv_ln_g, conv_ln_b,
           w_o, ln2_g, w_group, b_group, w_router, b_router, w_gate, w_up, w_down, final_g):
    batch, seq, d = x.shape
    assert ln1_g.shape[0] == 1, "single-layer trunk"
    t = batch * seq
    q_rank = q_norm_g.shape[1]
    kv_rank = kv_norm_g.shape[1]
    x2 = x.reshape(t, d)

    wi = w_in[0]
    o_kr = q_rank + kv_rank
    o_u = o_kr + QK_ROPE_DIM
    wlat, wu, wkr = _wsplit_call(jnp.transpose(wi), o_kr=o_kr, o_u=o_u)
    wuq = w_uq[0].reshape(q_rank, MLA_HEADS, QK_NOPE_DIM + QK_ROPE_DIM)
    wuq = jnp.concatenate([wuq[:, :, :QK_NOPE_DIM].reshape(q_rank, MLA_HEADS * QK_NOPE_DIM),
                           wuq[:, :, QK_NOPE_DIM:].reshape(q_rank, MLA_HEADS * QK_ROPE_DIM)], axis=1).astype(BF16)
    wukv = w_ukv[0]
    wo = w_o[0]
    wr = jnp.concatenate([w_router[0], w_group[0],
                          jnp.zeros((d, LANES - N_EXPERTS - N_EXPERT_GROUPS), F32)], axis=1)
    wr_hi = wr.astype(BF16)
    wr_lo = (wr - wr_hi.astype(F32)).astype(BF16)
    wr2 = jnp.concatenate([wr_hi, wr_lo], axis=1)
    br = jnp.concatenate([b_router[0], b_group[0],
                          jnp.zeros((LANES - N_EXPERTS - N_EXPERT_GROUPS,), F32)])[None, :]
    cos_t, s1_t, s2_t = _rope_tables(seq)

    tm = min(512, seq)
    q, k, v, c = _proj_call(x2, ln1_g, wlat, wu, wkr, b_glu, q_norm_g, kv_norm_g, wuq, wukv, cos_t, s1_t, s2_t,
                            seq=seq, tm=tm)
    attn, conv = _attn_conv_call(q, k, v, c, w_dw[0], b_dw, batch=batch, seq=seq, tq=min(256, seq))
    h, logits = _oproj_call(attn, conv, x2, wo, conv_ln_g, conv_ln_b, ln2_g, wr2, br, tm=tm)

    n_blocks = -(-(2 * t + N_EXPERTS * (MOE_ROWS - 1)) // MOE_ROWS)
    dest0, dest1, gates, meta = _route_call(logits, rows_per_block=MOE_ROWS, n_blocks=n_blocks)
    y = _expert_call(meta[:n_blocks, 0], meta[:n_blocks, 1], meta[0:1, 2], meta[:n_blocks, 3], meta[0:1, 4],
                     dest0, dest1, h, ln2_g, w_gate[0], w_up[0], w_down[0],
                     rows_per_block=MOE_ROWS, n_blocks=n_blocks)
    out = _final_call(dest0, dest1, h, y, gates, final_g[None, :], tm=tm)
    return out.reshape(batch, seq, d)
```

```python
import functools

import jax
import jax.numpy as jnp
import numpy as np
from jax import lax
from jax.experimental import pallas as pl
from jax.experimental.pallas import tpu as pltpu

F32 = jnp.float32
BF16 = jnp.bfloat16
I32 = jnp.int32

MLA_HEADS = 8
QK_NOPE_DIM = 128
QK_ROPE_DIM = 64
V_HEAD_DIM = 128
ROPE_THETA = 10000.0
N_EXPERT_GROUPS = 8
EXPERTS_PER_GROUP = 8
N_EXPERTS = N_EXPERT_GROUPS * EXPERTS_PER_GROUP
EPS = 1e-6
LOG2E = 1.4426950408889634

LANES = 128
SUBLANES = 8
HEAD_SLOT = 2 * LANES
ROPE_HALF = QK_ROPE_DIM // 2
VMEM_LIMIT = 56 * 1024 * 1024

MOE_ROWS = 256
FINAL_SLOTS = 3
FINAL_UNROLL = 8
ROUTE_UNROLL = 4
EXPERT_WEIGHT_SLOTS = 3
DMA_UNROLL = 8


def _rms(x, g):
    return x * lax.rsqrt(jnp.mean(x * x, axis=-1, keepdims=True) + EPS) * g


def _dot(a, b):
    return jnp.dot(a, b, preferred_element_type=F32)


def _whole(shape, single=False):
    mode = dict(pipeline_mode=pl.Buffered(1)) if single else {}
    return pl.BlockSpec(shape, lambda *_: (0,) * len(shape), **mode)


def _proj_kernel(x_ref, g1_ref, wlat_ref, wu_ref, wkr_ref, bglu_ref, qg_ref, kvg_ref, wuq_ref, wukv_ref,
                 cos_ref, s1_ref, s2_ref, q_out, k_out, v_out, c_out, *, q_rank, kv_rank, conv_ch, q_scale):
    xn = _rms(x_ref[...], g1_ref[...]).astype(BF16)
    cos = cos_ref[...]
    s1 = s1_ref[...]
    s2 = s2_ref[...]

    def rope(t):
        return t * cos + pltpu.roll(t, ROPE_HALF, 1) * s1 + pltpu.roll(t, LANES - ROPE_HALF, 1) * s2

    a = _dot(xn, wu_ref[:, :conv_ch]) + bglu_ref[:, :conv_ch]
    gate = _dot(xn, wu_ref[:, conv_ch:]) + bglu_ref[:, conv_ch:]
    c_out[...] = (a * jax.nn.sigmoid(gate)).astype(BF16)

    kr = rope(_dot(xn, wkr_ref[...])).astype(BF16)
    qn = _rms(_dot(xn, wlat_ref[:, :q_rank]), qg_ref[...]).astype(BF16)
    kvn = _rms(_dot(xn, wlat_ref[:, q_rank:q_rank + kv_rank]), kvg_ref[...]).astype(BF16)
    lane = lax.broadcasted_iota(I32, (x_ref.shape[0], LANES), 1)
    ones_blk = (lane == 0).astype(BF16)
    n_nope = MLA_HEADS * QK_NOPE_DIM
    q_nope = _dot(qn, wuq_ref[:, :n_nope])
    q_rope = _dot(qn, wuq_ref[:, n_nope:])
    for h in range(MLA_HEADS):
        c0 = h * HEAD_SLOT
        q_out[:, c0:c0 + LANES] = (q_nope[:, h * QK_NOPE_DIM:(h + 1) * QK_NOPE_DIM] * q_scale).astype(BF16)
        if h % 2 == 0:
            pair = rope(q_rope[:, (h // 2) * LANES:(h // 2 + 1) * LANES]) * q_scale
        half_pair = pair if h % 2 == 0 else pltpu.roll(pair, LANES - QK_ROPE_DIM, 1)
        q_out[:, c0 + LANES:c0 + HEAD_SLOT] = jnp.where(lane < QK_ROPE_DIM, half_pair, 0.0).astype(BF16)
        kvh = _dot(kvn, wukv_ref[:, c0:c0 + HEAD_SLOT].astype(BF16))
        k_out[:, c0:c0 + LANES] = kvh[:, :LANES].astype(BF16)
        k_out[:, c0 + LANES:c0 + HEAD_SLOT] = kr
        v_out[:, c0:c0 + LANES] = kvh[:, LANES:].astype(BF16)
        v_out[:, c0 + LANES:c0 + HEAD_SLOT] = ones_blk


def _proj_call(x2, g1, wlat, wu, wkr, bglu, qg, kvg, wuq, wukv, cos_t, s1_t, s2_t, *, seq, tm):
    t, d = x2.shape
    q_rank, kv_rank = qg.shape[1], kvg.shape[1]
    conv_ch = bglu.shape[1] // 2
    n_pos = seq // tm
    row = lambda i: (i, 0)
    pos = lambda i: (i % n_pos, 0)
    q_scale = float(QK_NOPE_DIM + QK_ROPE_DIM) ** -0.5 * LOG2E
    kern = functools.partial(_proj_kernel, q_rank=q_rank, kv_rank=kv_rank, conv_ch=conv_ch, q_scale=q_scale)
    slot_w = MLA_HEADS * HEAD_SLOT
    return pl.pallas_call(
        kern,
        grid=(t // tm,),
        in_specs=[
            pl.BlockSpec((tm, d), row), _whole(g1.shape), _whole(wlat.shape, True), _whole(wu.shape, True),
            _whole(wkr.shape, True), _whole(bglu.shape),
            _whole(qg.shape), _whole(kvg.shape), _whole(wuq.shape, True), _whole(wukv.shape, True),
            pl.BlockSpec((tm, LANES), pos), pl.BlockSpec((tm, LANES), pos), pl.BlockSpec((tm, LANES), pos),
        ],
        out_specs=[
            pl.BlockSpec((tm, slot_w), row), pl.BlockSpec((tm, slot_w), row),
            pl.BlockSpec((tm, slot_w), row), pl.BlockSpec((tm, conv_ch), row),
        ],
        out_shape=[
            jax.ShapeDtypeStruct((t, slot_w), BF16), jax.ShapeDtypeStruct((t, slot_w), BF16),
            jax.ShapeDtypeStruct((t, slot_w), BF16), jax.ShapeDtypeStruct((t, conv_ch), BF16),
        ],
        compiler_params=pltpu.CompilerParams(dimension_semantics=("arbitrary",), vmem_limit_bytes=VMEM_LIMIT),
        name="proj",
    )(x2, g1, wlat, wu, wkr, bglu, qg, kvg, wuq, wukv, cos_t, s1_t, s2_t)


CONV_PAD = 16
CONV_ROWS = 128


def _attn_conv_kernel(q_ref, k_ref, v_ref, c_ref, w_ref, b_ref, o_ref, y_ref, xp_ref, *, tq, width):
    s_len = q_ref.shape[0]
    half = width // 2
    rows = min(CONV_ROWS, s_len)
    win = rows + 2 * CONV_PAD
    zeros = jnp.zeros((CONV_PAD, LANES), F32)
    xp_ref[0:CONV_PAD, :] = zeros
    xp_ref[CONV_PAD + s_len:, :] = zeros
    xp_ref[CONV_PAD:CONV_PAD + s_len, :] = c_ref[...].astype(F32)

    def conv_chunk(ci):
        base = ci * rows
        xw = xp_ref[base:base + win, :]
        acc = jnp.zeros((rows, LANES), F32)
        for r in range(SUBLANES):
            shifted = xw if r == 0 else pltpu.roll(xw, win - r, 0)
            for a0 in range(0, 2 * CONV_PAD, SUBLANES):
                k = a0 + r - (CONV_PAD - half)
                if 0 <= k < width:
                    acc = acc + shifted[a0:a0 + rows, :] * w_ref[k:k + 1, :]
        y_ref[base:base + rows, :] = (acc + b_ref[...]).astype(BF16)

    k = k_ref[...]
    v = v_ref[...]
    n_q = s_len // tq
    n_chunks = s_len // rows
    for j in range(n_q):
        qs = slice(j * tq, (j + 1) * tq)
        s = lax.dot_general(q_ref[qs, :], k, (((1,), (1,)), ((), ())), preferred_element_type=F32)
        m = jnp.max(s, axis=-1, keepdims=True)
        p = jnp.exp2(s - m).astype(BF16)
        o = _dot(p, v)
        o_ref[qs, :] = (o[:, :V_HEAD_DIM] / o[:, V_HEAD_DIM:V_HEAD_DIM + 1]).astype(BF16)
        for ci in range(j * n_chunks // n_q, (j + 1) * n_chunks // n_q):
            conv_chunk(ci)


def _attn_conv_call(q, k, v, c, w_dw, b_dw, *, batch, seq, tq):
    t, ch = c.shape
    width = w_dw.shape[0]
    assert width // 2 <= CONV_PAD and ch == MLA_HEADS * LANES
    head = lambda b, h: (b, h)
    chan = lambda b, h: (0, h)
    return pl.pallas_call(
        functools.partial(_attn_conv_kernel, tq=tq, width=width),
        grid=(batch, MLA_HEADS),
        in_specs=[pl.BlockSpec((seq, HEAD_SLOT), head), pl.BlockSpec((seq, HEAD_SLOT), head),
                  pl.BlockSpec((seq, HEAD_SLOT), head), pl.BlockSpec((seq, LANES), head),
                  pl.BlockSpec((width, LANES), chan), pl.BlockSpec((1, LANES), chan)],
        out_specs=[pl.BlockSpec((seq, V_HEAD_DIM), head), pl.BlockSpec((seq, LANES), head)],
        out_shape=[jax.ShapeDtypeStruct((t, MLA_HEADS * V_HEAD_DIM), BF16), jax.ShapeDtypeStruct((t, ch), BF16)],
        scratch_shapes=[pltpu.VMEM((seq + 2 * CONV_PAD, LANES), F32)],
        compiler_params=pltpu.CompilerParams(
            dimension_semantics=("arbitrary", "arbitrary"), vmem_limit_bytes=VMEM_LIMIT),
        name="attn_conv",
    )(q, k, v, c, w_dw, b_dw)


def _oproj_kernel(a_ref, c_ref, x_ref, wo_ref, lg_ref, lb_ref, g2_ref, wr_ref, br_ref, h_out, lg_out):
    na = a_ref.shape[1]
    y = c_ref[...].astype(F32)
    yc = y - jnp.mean(y, axis=-1, keepdims=True)
    z = yc * lax.rsqrt(jnp.mean(yc * yc, axis=-1, keepdims=True) + EPS) * lg_ref[...] + lb_ref[...]
    act = (z * jax.nn.sigmoid(z)).astype(BF16)
    h = (x_ref[...] + _dot(a_ref[...], wo_ref[:na, :].astype(BF16))
         + _dot(act, wo_ref[na:, :].astype(BF16)))
    h_out[...] = h
    hn = _rms(h, g2_ref[...])
    hi = hn.astype(BF16)
    lo = (hn - hi.astype(F32)).astype(BF16)
    r = _dot(hi, wr_ref[...])
    lg_out[...] = r[:, :LANES] + r[:, LANES:] + _dot(lo, wr_ref[:, :LANES]) + br_ref[...]


def _oproj_call(attn, conv, x2, wo, ln_g, ln_b, g2, wr, br, *, tm):
    t, d = x2.shape
    row = lambda i: (i, 0)
    return pl.pallas_call(
        _oproj_kernel,
        grid=(t // tm,),
        in_specs=[pl.BlockSpec((tm, attn.shape[1]), row), pl.BlockSpec((tm, conv.shape[1]), row),
                  pl.BlockSpec((tm, d), row), _whole(wo.shape, True), _whole(ln_g.shape), _whole(ln_b.shape),
                  _whole(g2.shape), _whole(wr.shape, True), _whole(br.shape)],
        out_specs=[pl.BlockSpec((tm, d), row), pl.BlockSpec((tm, LANES), row)],
        out_shape=[jax.ShapeDtypeStruct((t, d), F32), jax.ShapeDtypeStruct((t, LANES), F32)],
        compiler_params=pltpu.CompilerParams(dimension_semantics=("arbitrary",), vmem_limit_bytes=VMEM_LIMIT),
        name="oproj",
    )(attn, conv, x2, wo, ln_g, ln_b, g2, wr, br)


def _route_tokens_on_lanes(lt):
    shape = (SUBLANES, LANES)
    row = lax.broadcasted_iota(I32, shape, 0)
    big = jnp.int32(1 << 20)
    neg = jnp.float32(-jnp.inf)

    def top(v):
        m = jnp.max(v, axis=0, keepdims=True)
        return m, jnp.min(jnp.where(v == m, row, big), axis=0, keepdims=True)

    lgrp = lt[N_EXPERTS:N_EXPERTS + N_EXPERT_GROUPS, :]
    gmax, gsel = top(lgrp)
    p_g = 1.0 / jnp.sum(jnp.exp(lgrp - gmax), axis=0, keepdims=True)
    le = jnp.zeros(shape, F32)
    for g in range(N_EXPERT_GROUPS):
        le = jnp.where(gsel == g, lt[g * EXPERTS_PER_GROUP:(g + 1) * EXPERTS_PER_GROUP, :], le)
    m1, i1 = top(le)
    m2, i2 = top(jnp.where(row == i1, neg, le))
    r = jnp.exp(m2 - m1)
    w1 = 1.0 / (1.0 + r)
    w2 = r / (1.0 + r)
    base = gsel * EXPERTS_PER_GROUP
    return base + i1, base + i2, p_g * w1, p_g * w2


def _lane_cumsum(v):
    lane = lax.broadcasted_iota(I32, v.shape, 1)
    sh = 1
    while sh < LANES:
        v = v + jnp.where(lane >= sh, pltpu.roll(v, sh, 1), 0)
        sh *= 2
    return v


def _route_kernel(lg_ref, d0_out, d1_out, gate_out, meta_out, e_ref, *, rows_per_block):
    n_chunks = d0_out.shape[0]
    shift = rows_per_block.bit_length() - 1
    sub_shift = SUBLANES.bit_length() - 1
    sq = (LANES, LANES)
    row = lax.broadcasted_iota(I32, sq, 0)
    row8 = lax.broadcasted_iota(I32, (SUBLANES, LANES), 0)

    def one_hots(e1, e2):
        return (row == e1).astype(F32), (row == e2).astype(F32)

    def count_step(i, cnt):
        base = pl.multiple_of(i * LANES, LANES)
        e1, e2, g1, g2 = _route_tokens_on_lanes(jnp.transpose(lg_ref[pl.ds(base, LANES), :]))
        e_ref[i] = jnp.where(row8 == 0, e1, jnp.where(row8 == 1, e2, 0))
        gate_out[pl.ds(base, LANES), :] = jnp.transpose(jnp.where(row == 0, g1, jnp.where(row == 1, g2, 0.0)))
        oh1, oh2 = one_hots(e1, e2)
        return cnt + jnp.sum(oh1 + oh2, axis=1, keepdims=True)

    unroll = ROUTE_UNROLL if n_chunks % ROUTE_UNROLL == 0 else 1
    counts_col = lax.fori_loop(0, n_chunks, count_step, jnp.zeros((LANES, 1), F32), unroll=unroll)
    counts = jnp.transpose(jnp.broadcast_to(counts_col, sq))[0:SUBLANES, :].astype(I32)
    padded = ((counts + (SUBLANES - 1)) >> sub_shift) << sub_shift
    pad_end = _lane_cumsum(padded)
    pad_start = pad_end - padded
    start_col = jnp.transpose(jnp.broadcast_to(pad_start[0:1, :].astype(F32), sq))[:, 0:1]

    tri = (row < lax.broadcasted_iota(I32, sq, 1)).astype(BF16)

    def dest_step(i, carry):
        er = e_ref[i]
        oh1, oh2 = one_hots(er[0:1, :], er[1:2, :])
        oh = oh1 + oh2
        pos = carry + _dot(oh.astype(BF16), tri)
        d0_out[i] = jnp.sum(oh1 * pos, axis=0, keepdims=True).astype(I32)
        d1_out[i] = jnp.sum(oh2 * pos, axis=0, keepdims=True).astype(I32)
        return carry + jnp.sum(oh, axis=1, keepdims=True)

    lax.fori_loop(0, n_chunks, dest_step, start_col, unroll=unroll)

    nbp = meta_out.shape[0]
    lane_b = lax.broadcasted_iota(I32, (nbp, LANES), 1)
    blk = lax.broadcasted_iota(I32, (nbp, LANES), 0)
    nblk = (counts + (rows_per_block - 1)) >> shift
    blk_end = _lane_cumsum(nblk)
    bcast = lambda v: jnp.broadcast_to(v[0:1, :], (nbp, LANES))
    be_end, be_start, cn, ps = bcast(blk_end), bcast(blk_end - nblk), bcast(counts), bcast(pad_start)
    is_e = lane_b < N_EXPERTS
    lsum = lambda v: jnp.sum(v, axis=-1, keepdims=True)
    n_used = jnp.max(be_end, axis=-1, keepdims=True)
    total = jnp.max(bcast(pad_end), axis=-1, keepdims=True)
    last_e = jnp.max(jnp.where(is_e & (cn > 0), lane_b, 0), axis=-1, keepdims=True)
    be = jnp.minimum(lsum(jnp.where(is_e & (be_end <= blk), 1, 0)), last_e)
    sel = lane_b == be
    first_row = (blk[:, 0:1] - lsum(jnp.where(sel, be_start, 0))) * rows_per_block
    used = blk[:, 0:1] < n_used
    nvalid = jnp.where(used, jnp.clip(lsum(jnp.where(sel, cn, 0)) - first_row, 0, rows_per_block), 0)
    row0 = jnp.where(used, lsum(jnp.where(sel, ps, 0)) + first_row, 0)
    cols = (be, nvalid, n_used, row0, total)
    meta = jnp.zeros((nbp, LANES), I32)
    for c, v in enumerate(cols):
        meta = jnp.where(lane_b == c, v, meta)
    meta_out[...] = meta


def _route_call(logits, *, rows_per_block, n_blocks):
    t = logits.shape[0]
    assert t % LANES == 0
    nbp = -(-n_blocks // SUBLANES) * SUBLANES
    dshape = (t // LANES, 1, LANES)
    d0, d1, gates, meta = pl.pallas_call(
        functools.partial(_route_kernel, rows_per_block=rows_per_block),
        in_specs=[_whole(logits.shape)],
        out_specs=[_whole(dshape), _whole(dshape), _whole((t, LANES)), _whole((nbp, LANES))],
        out_shape=[jax.ShapeDtypeStruct(dshape, I32), jax.ShapeDtypeStruct(dshape, I32),
                   jax.ShapeDtypeStruct((t, LANES), F32), jax.ShapeDtypeStruct((nbp, LANES), I32)],
        grid=(1,),
        scratch_shapes=[pltpu.VMEM((t // LANES, SUBLANES, LANES), I32)],
        compiler_params=pltpu.CompilerParams(dimension_semantics=("arbitrary",), vmem_limit_bytes=VMEM_LIMIT),
        name="route",
    )(logits)
    return d0.reshape(t), d1.reshape(t), gates, meta


def _pow2_chunks(limit):
    c = 1 << (limit.bit_length() - 1)
    while c >= 1:
        yield c
        c >>= 1


def _expert_kernel(be_ref, nv_ref, nused_ref, row0_ref, total_ref, d0_ref, d1_ref,
                   h_hbm, g2_ref, wg_hbm, wu_hbm, wd_hbm, y_hbm,
                   tok_ref, ord_ref, exp_ref, nexp_ref, xbuf, ybuf, zbuf, wg_buf, wu_buf, wd_buf,
                   gsem, ysem, zsem, wsem):
    b = pl.program_id(0)
    n_used = nused_ref[0]
    groups = xbuf.shape[1]
    rows = groups * SUBLANES
    d = xbuf.shape[3]
    n_tok = d0_ref.shape[0]
    sub_shift = SUBLANES.bit_length() - 1
    n_slots = wg_buf.shape[0]

    def y_copies(blk, slot):
        ng = lax.shift_right_logical(nv_ref[blk] + (SUBLANES - 1), sub_shift)
        g0 = lax.shift_right_logical(row0_ref[blk], sub_shift)
        out = []
        for c in _pow2_chunks(groups):
            off = ng & ~(2 * c - 1)
            copy = pltpu.make_async_copy(ybuf.at[slot, pl.ds(off, c)], y_hbm.at[pl.ds(g0 + off, c)], ysem.at[slot])
            out.append(((ng & c) != 0, copy))
        return out

    def start_y(blk, slot):
        for cond, copy in y_copies(blk, slot):
            pl.when(cond)(functools.partial(copy.start, priority=1))

    def wait_y(blk, slot):
        for cond, copy in y_copies(blk, slot):
            pl.when(cond)(copy.wait)

    def fill_tail():
        zbuf[...] = jnp.zeros(zbuf.shape, F32)
        first = lax.shift_right_logical(total_ref[0], sub_shift)
        copy = lambda g: pltpu.make_async_copy(zbuf, y_hbm.at[g], zsem.at[0])
        lax.fori_loop(first, y_hbm.shape[0], lambda g, c: (copy(g).start(), c)[1], 0)
        lax.fori_loop(first, y_hbm.shape[0], lambda g, c: (copy(g).wait(), c)[1], 0)

    def weight_copies(j):
        e = exp_ref[j]
        slot = lax.rem(j, n_slots)
        return [pltpu.make_async_copy(src.at[e], dst.at[slot], wsem.at[slot])
                for src, dst in ((wg_hbm, wg_buf), (wu_hbm, wu_buf), (wd_hbm, wd_buf))]

    def start_weights(j):
        for c in weight_copies(j):
            c.start(priority=1)

    def wait_weights(j):
        for c in weight_copies(j):
            c.wait()

    def gather_copy(blk, slot, g, u):
        tok = tok_ref[row0_ref[blk] + g * SUBLANES + u]
        return pltpu.make_async_copy(h_hbm.at[pl.ds(tok, 1)], xbuf.at[slot, g, pl.ds(u, 1)], gsem.at[slot])

    def start_gather(blk, slot):
        n = nv_ref[blk]

        def group(g, c):
            for u in range(SUBLANES):
                gather_copy(blk, slot, g, u).start()
            return c
        full = lax.shift_right_logical(n, sub_shift)
        lax.fori_loop(0, full, group, 0)
        for u in range(SUBLANES - 1):
            @pl.when(full * SUBLANES + u < n)
            def _():
                gather_copy(blk, slot, full, u).start()

    def wait_gather(blk, slot):
        n = nv_ref[blk]
        buf = xbuf.at[slot]
        for c in _pow2_chunks(rows):
            @pl.when((n & c) != 0)
            def _():
                if c >= SUBLANES:
                    part = buf.at[pl.ds(0, c // SUBLANES)]
                else:
                    part = buf.at[0, pl.ds(0, c)]
                pltpu.make_async_copy(part, part, gsem.at[slot]).wait()

    @pl.when(b == 0)
    def _():
        def scan(blk, j):
            e = be_ref[blk]
            is_new = jnp.logical_or(blk == 0, e != be_ref[jnp.maximum(blk - 1, 0)])
            j = j + is_new.astype(I32)
            ord_ref[blk] = j - 1

            @pl.when(is_new)
            def _():
                exp_ref[j - 1] = e
            return j
        n_exp = lax.fori_loop(0, n_used, scan, jnp.int32(0))
        nexp_ref[0] = n_exp
        for j in range(n_slots):
            @pl.when(j < n_exp)
            def _():
                start_weights(j)

        def inv(g, c):
            for u in range(DMA_UNROLL):
                tk = g * DMA_UNROLL + u
                tok_ref[d0_ref[tk]] = tk
                tok_ref[d1_ref[tk]] = tk
            return c
        lax.fori_loop(0, n_tok // DMA_UNROLL, inv, 0)
        xbuf[...] = jnp.zeros(xbuf.shape, F32)
        start_gather(0, 0)

    @pl.when(b < n_used)
    def _():
        slot = b & 1
        wait_gather(b, slot)

        @pl.when(b + 1 < n_used)
        def _():
            start_gather(b + 1, 1 - slot)

        j = ord_ref[b]

        @pl.when(jnp.logical_or(b == 0, ord_ref[jnp.maximum(b - 1, 0)] != j))
        def _():
            wait_weights(j)

            @pl.when(jnp.logical_and(j >= 1, j + (n_slots - 1) < nexp_ref[0]))
            def _():
                start_weights(j + (n_slots - 1))

        ws = lax.rem(j, n_slots)
        hn = _rms(xbuf[slot].reshape(rows, d), g2_ref[...]).astype(BF16)
        gate = _dot(hn, wg_buf[ws].astype(BF16))
        up = _dot(hn, wu_buf[ws].astype(BF16))
        hmid = (gate * jax.nn.sigmoid(gate) * up).astype(BF16)
        y = _dot(hmid, wd_buf[ws].astype(BF16))

        @pl.when(b >= 2)
        def _():
            wait_y(b - 2, slot)

        ybuf[slot] = y.reshape(groups, SUBLANES, d)
        start_y(b, slot)

        @pl.when(b == n_used - 1)
        def _():
            @pl.when(b >= 1)
            def _():
                wait_y(b - 1, 1 - slot)
            wait_y(b, slot)
            fill_tail()


def _expert_call(be, nvalid, n_used, row0, total, dest0, dest1, h, g2, w_gate, w_up, w_down, *,
                 rows_per_block, n_blocks):
    t, d = h.shape
    n_exp, _, f = w_gate.shape
    assert t % DMA_UNROLL == 0 and rows_per_block % SUBLANES == 0 and (2 * t) % SUBLANES == 0
    groups = rows_per_block // SUBLANES
    sorted_groups = 2 * t // SUBLANES + n_exp
    hbm = pl.BlockSpec(memory_space=pl.ANY)
    slots = EXPERT_WEIGHT_SLOTS
    grid_spec = pltpu.PrefetchScalarGridSpec(
        num_scalar_prefetch=7,
        grid=(n_blocks,),
        in_specs=[hbm, pl.BlockSpec(g2.shape, lambda b, *_: (0, 0)), hbm, hbm, hbm],
        out_specs=hbm,
        scratch_shapes=[
            pltpu.SMEM((sorted_groups * SUBLANES,), I32),
            pltpu.SMEM((n_blocks,), I32), pltpu.SMEM((n_exp,), I32), pltpu.SMEM((1,), I32),
            pltpu.VMEM((2, groups, SUBLANES, d), F32), pltpu.VMEM((2, groups, SUBLANES, d), F32),
            pltpu.VMEM((SUBLANES, d), F32),
            pltpu.VMEM((slots, d, f), F32), pltpu.VMEM((slots, d, f), F32), pltpu.VMEM((slots, f, d), F32),
            pltpu.SemaphoreType.DMA((2,)), pltpu.SemaphoreType.DMA((2,)), pltpu.SemaphoreType.DMA((1,)),
            pltpu.SemaphoreType.DMA((slots,)),
        ],
    )
    y = pl.pallas_call(
        _expert_kernel,
        grid_spec=grid_spec,
        out_shape=jax.ShapeDtypeStruct((sorted_groups, SUBLANES, d), F32),
        compiler_params=pltpu.CompilerParams(dimension_semantics=("arbitrary",), vmem_limit_bytes=VMEM_LIMIT),
        name="experts",
    )(be, nvalid, n_used, row0, total, dest0, dest1, h, g2, w_gate, w_up, w_down)
    return y.reshape(sorted_groups * SUBLANES, d)


def _final_kernel(d0_ref, d1_ref, h_ref, y_hbm, gate_ref, fg_ref, o_ref, ybuf, sem):
    i = pl.program_id(0)
    n = pl.num_programs(0)
    n_slots = ybuf.shape[0]
    groups = ybuf.shape[2]
    tm = groups * SUBLANES

    def issue_group(tile, slot, g):
        for u in range(SUBLANES):
            tk = tile * tm + g * SUBLANES + u
            for k, dref in enumerate((d0_ref, d1_ref)):
                pltpu.make_async_copy(y_hbm.at[pl.ds(dref[tk], 1)], ybuf.at[slot, k, g, pl.ds(u, 1)],
                                      sem.at[slot]).start(priority=1)

    def combine_group(slot, g, issue):
        r0 = pl.multiple_of(g * SUBLANES, SUBLANES)
        gt = gate_ref[pl.ds(r0, SUBLANES), :]
        out = h_ref[pl.ds(r0, SUBLANES), :] + gt[:, 0:1] * ybuf[slot, 0, g] + gt[:, 1:2] * ybuf[slot, 1, g]
        issue()
        o_ref[pl.ds(r0, SUBLANES), :] = _rms(out, fg_ref[...])

    @pl.when(i == 0)
    def _():
        for tile in range(n_slots - 1):
            @pl.when(tile < n)
            def _():
                lax.fori_loop(0, groups, lambda g, c: (issue_group(tile, tile, g), c)[1], 0)

    slot = lax.rem(i, n_slots)
    pltpu.make_async_copy(ybuf.at[slot], ybuf.at[slot], sem.at[slot]).wait()
    ahead = i + (n_slots - 1)
    unroll = FINAL_UNROLL if groups % FINAL_UNROLL == 0 else 1

    @pl.when(ahead < n)
    def _():
        nslot = lax.rem(ahead, n_slots)
        lax.fori_loop(0, groups, lambda g, c: (combine_group(
            slot, g, lambda: issue_group(ahead, nslot, g)), c)[1], 0, unroll=unroll)

    @pl.when(ahead >= n)
    def _():
        lax.fori_loop(0, groups, lambda g, c: (combine_group(slot, g, lambda: None), c)[1], 0, unroll=unroll)


def _final_call(dest0, dest1, h, y, gates, fg, *, tm):
    t, d = h.shape
    assert tm % SUBLANES == 0
    row = lambda i, *_: (i, 0)
    grid_spec = pltpu.PrefetchScalarGridSpec(
        num_scalar_prefetch=2,
        grid=(t // tm,),
        in_specs=[pl.BlockSpec((tm, d), row), pl.BlockSpec(memory_space=pl.ANY),
                  pl.BlockSpec((tm, LANES), row), pl.BlockSpec(fg.shape, lambda i, *_: (0, 0))],
        out_specs=pl.BlockSpec((tm, d), row),
        scratch_shapes=[pltpu.VMEM((FINAL_SLOTS, 2, tm // SUBLANES, SUBLANES, d), F32),
                        pltpu.SemaphoreType.DMA((FINAL_SLOTS,))],
    )
    return pl.pallas_call(
        _final_kernel,
        grid_spec=grid_spec,
        out_shape=jax.ShapeDtypeStruct((t, d), F32),
        compiler_params=pltpu.CompilerParams(dimension_semantics=("arbitrary",), vmem_limit_bytes=VMEM_LIMIT),
        name="final",
    )(dest0, dest1, h, y, gates, fg)


def _wsplit_kernel(lat_ref, u_ref, kr_ref, lat_out, u_out, kr_out, *, n_lat, n_kr):
    j = pl.program_id(0)
    u_out[...] = jnp.transpose(u_ref[...]).astype(BF16)

    @pl.when(j < n_lat)
    def _():
        lat_out[...] = jnp.transpose(lat_ref[...]).astype(BF16)

    @pl.when(j == 0)
    def _():
        kr = jnp.transpose(kr_ref[...])
        lane = lax.broadcasted_iota(I32, kr.shape, 1)
        kr_out[...] = jnp.where(lane < n_kr, kr, 0.0).astype(BF16)


def _wsplit_call(w_t, *, o_kr, o_u):
    cols, d = w_t.shape
    assert o_kr % LANES == 0 and (cols - o_u) % LANES == 0 and o_u % SUBLANES == 0
    blk = 2 * LANES if o_kr % (2 * LANES) == 0 and (cols - o_u) % (2 * LANES) == 0 else LANES
    n_u = (cols - o_u) // blk
    n_lat = o_kr // blk
    assert 0 < o_u - o_kr <= LANES and n_lat <= n_u and o_kr + LANES <= cols
    lat_blk = lambda j: jnp.minimum(j, n_lat - 1)
    return pl.pallas_call(
        functools.partial(_wsplit_kernel, n_lat=n_lat, n_kr=o_u - o_kr),
        grid=(n_u,),
        in_specs=[pl.BlockSpec((blk, d), lambda j: (lat_blk(j), 0)),
                  pl.BlockSpec((pl.Element(blk), pl.Element(d)),
                               lambda j: (pl.multiple_of(o_u + j * blk, SUBLANES), 0)),
                  pl.BlockSpec((pl.Element(LANES), pl.Element(d)), lambda j: (o_kr, 0))],
        out_specs=[pl.BlockSpec((d, blk), lambda j: (0, lat_blk(j))), pl.BlockSpec((d, blk), lambda j: (0, j)),
                   pl.BlockSpec((d, LANES), lambda j: (0, 0))],
        out_shape=[jax.ShapeDtypeStruct((d, o_kr), BF16), jax.ShapeDtypeStruct((d, cols - o_u), BF16),
                   jax.ShapeDtypeStruct((d, LANES), BF16)],
        compiler_params=pltpu.CompilerParams(dimension_semantics=("arbitrary",), vmem_limit_bytes=VMEM_LIMIT),
        name="wsplit",
    )(w_t, w_t, w_t)


def _rope_tables(seq):
    pos = np.arange(seq, dtype=np.float64)
    inv_freq = ROPE_THETA ** (-np.arange(0, QK_ROPE_DIM, 2, dtype=np.float64) / QK_ROPE_DIM)
    ang = pos[:, None] * inv_freq[None, :]
    cos, sin = np.cos(ang).astype(np.float32), np.sin(ang).astype(np.float32)
    zero = np.zeros_like(sin)
    cos_t = np.concatenate([cos, cos, cos, cos], axis=1)
    s1_t = np.concatenate([zero, sin, zero, sin], axis=1)
    s2_t = np.concatenate([-sin, zero, -sin, zero], axis=1)
    return jnp.asarray(cos_t), jnp.asarray(s1_t), jnp.asarray(s2_t)


def kernel(x, ln1_g, w_in, b_glu, q_norm_g, w_uq, kv_norm_g, w_ukv, w_dw, b_dw, conv_ln_g, conv_ln_b,
           w_o, ln2_g, w_group, b_group, w_router, b_router, w_gate, w_up, w_down, final_g):
    batch, seq, d = x.shape
    assert ln1_g.shape[0] == 1, "single-layer trunk"
    t = batch * seq
    q_rank = q_norm_g.shape[1]
    kv_rank = kv_norm_g.shape[1]
    x2 = x.reshape(t, d)

    wi = w_in[0]
    o_kr = q_rank + kv_rank
    o_u = o_kr + QK_ROPE_DIM
    wlat, wu, wkr = _wsplit_call(jnp.transpose(wi), o_kr=o_kr, o_u=o_u)
    wuq = w_uq[0].reshape(q_rank, MLA_HEADS, QK_NOPE_DIM + QK_ROPE_DIM)
    wuq = jnp.concatenate([wuq[:, :, :QK_NOPE_DIM].reshape(q_rank, MLA_HEADS * QK_NOPE_DIM),
                           wuq[:, :, QK_NOPE_DIM:].reshape(q_rank, MLA_HEADS * QK_ROPE_DIM)], axis=1).astype(BF16)
    wukv = w_ukv[0]
    wo = w_o[0]
    wr = jnp.concatenate([w_router[0], w_group[0],
                          jnp.zeros((d, LANES - N_EXPERTS - N_EXPERT_GROUPS), F32)], axis=1)
    wr_hi = wr.astype(BF16)
    wr_lo = (wr - wr_hi.astype(F32)).astype(BF16)
    wr2 = jnp.concatenate([wr_hi, wr_lo], axis=1)
    br = jnp.concatenate([b_router[0], b_group[0],
                          jnp.zeros((LANES - N_EXPERTS - N_EXPERT_GROUPS,), F32)])[None, :]
    cos_t, s1_t, s2_t = _rope_tables(seq)

    tm = min(512, seq)
    q, k, v, c = _proj_call(x2, ln1_g, wlat, wu, wkr, b_glu, q_norm_g, kv_norm_g, wuq, wukv, cos_t, s1_t, s2_t,
                            seq=seq, tm=tm)
    attn, conv = _attn_conv_call(q, k, v, c, w_dw[0], b_dw, batch=batch, seq=seq, tq=min(256, seq))
    h, logits = _oproj_call(attn, conv, x2, wo, conv_ln_g, conv_ln_b, ln2_g, wr2, br, tm=tm)

    n_blocks = -(-(2 * t + N_EXPERTS * (MOE_ROWS - 1)) // MOE_ROWS)
    dest0, dest1, gates, meta = _route_call(logits, rows_per_block=MOE_ROWS, n_blocks=n_blocks)
    y = _expert_call(meta[:n_blocks, 0], meta[:n_blocks, 1], meta[0:1, 2], meta[:n_blocks, 3], meta[0:1, 4],
                     dest0, dest1, h, ln2_g, w_gate[0], w_up[0], w_down[0],
                     rows_per_block=MOE_ROWS, n_blocks=n_blocks)
    out = _final_call(dest0, dest1, h, y, gates, final_g[None, :], tm=tm)
    return out.reshape(batch, seq, d)
```

```python
import functools

import jax
import jax.numpy as jnp
import numpy as np
from jax import lax
from jax.experimental import pallas as pl
from jax.experimental.pallas import tpu as pltpu

F32 = jnp.float32
BF16 = jnp.bfloat16
I32 = jnp.int32

MLA_HEADS = 8
QK_NOPE_DIM = 128
QK_ROPE_DIM = 64
V_HEAD_DIM = 128
ROPE_THETA = 10000.0
N_EXPERT_GROUPS = 8
EXPERTS_PER_GROUP = 8
N_EXPERTS = N_EXPERT_GROUPS * EXPERTS_PER_GROUP
EPS = 1e-6
LOG2E = 1.4426950408889634

LANES = 128
SUBLANES = 8
HEAD_SLOT = 2 * LANES
ROPE_HALF = QK_ROPE_DIM // 2
VMEM_LIMIT = 56 * 1024 * 1024

MOE_ROWS = 256
FINAL_SLOTS = 3
FINAL_UNROLL = 8
ROUTE_UNROLL = 4
EXPERT_WEIGHT_SLOTS = 3
DMA_UNROLL = 8


def _rms(x, g):
    return x * lax.rsqrt(jnp.mean(x * x, axis=-1, keepdims=True) + EPS) * g


def _dot(a, b):
    return jnp.dot(a, b, preferred_element_type=F32)


def _whole(shape, single=False):
    mode = dict(pipeline_mode=pl.Buffered(1)) if single else {}
    return pl.BlockSpec(shape, lambda *_: (0,) * len(shape), **mode)


def _proj_kernel(x_ref, g1_ref, wlat_ref, wu_ref, wkr_ref, bglu_ref, qg_ref, kvg_ref, wuq_ref, wukv_ref,
                 cos_ref, s1_ref, s2_ref, q_out, k_out, v_out, c_out, *, q_rank, kv_rank, conv_ch, q_scale):
    xn = _rms(x_ref[...], g1_ref[...]).astype(BF16)
    cos = cos_ref[...]
    s1 = s1_ref[...]
    s2 = s2_ref[...]

    def rope(t):
        return t * cos + pltpu.roll(t, ROPE_HALF, 1) * s1 + pltpu.roll(t, LANES - ROPE_HALF, 1) * s2

    a = _dot(xn, wu_ref[:, :conv_ch]) + bglu_ref[:, :conv_ch]
    gate = _dot(xn, wu_ref[:, conv_ch:]) + bglu_ref[:, conv_ch:]
    c_out[...] = (a * jax.nn.sigmoid(gate)).astype(BF16)

    kr = rope(_dot(xn, wkr_ref[...])).astype(BF16)
    qn = _rms(_dot(xn, wlat_ref[:, :q_rank]), qg_ref[...]).astype(BF16)
    kvn = _rms(_dot(xn, wlat_ref[:, q_rank:q_rank + kv_rank]), kvg_ref[...]).astype(BF16)
    lane = lax.broadcasted_iota(I32, (x_ref.shape[0], LANES), 1)
    ones_blk = (lane == 0).astype(BF16)
    n_nope = MLA_HEADS * QK_NOPE_DIM
    q_nope = _dot(qn, wuq_ref[:, :n_nope])
    q_rope = _dot(qn, wuq_ref[:, n_nope:])
    for h in range(MLA_HEADS):
        c0 = h * HEAD_SLOT
        q_out[:, c0:c0 + LANES] = (q_nope[:, h * QK_NOPE_DIM:(h + 1) * QK_NOPE_DIM] * q_scale).astype(BF16)
        if h % 2 == 0:
            pair = rope(q_rope[:, (h // 2) * LANES:(h // 2 + 1) * LANES]) * q_scale
        half_pair = pair if h % 2 == 0 else pltpu.roll(pair, LANES - QK_ROPE_DIM, 1)
        q_out[:, c0 + LANES:c0 + HEAD_SLOT] = jnp.where(lane < QK_ROPE_DIM, half_pair, 0.0).astype(BF16)
        kvh = _dot(kvn, wukv_ref[:, c0:c0 + HEAD_SLOT].astype(BF16))
        k_out[:, c0:c0 + LANES] = kvh[:, :LANES].astype(BF16)
        k_out[:, c0 + LANES:c0 + HEAD_SLOT] = kr
        v_out[:, c0:c0 + LANES] = kvh[:, LANES:].astype(BF16)
        v_out[:, c0 + LANES:c0 + HEAD_SLOT] = ones_blk


def _proj_call(x2, g1, wlat, wu, wkr, bglu, qg, kvg, wuq, wukv, cos_t, s1_t, s2_t, *, seq, tm):
    t, d = x2.shape
    q_rank, kv_rank = qg.shape[1], kvg.shape[1]
    conv_ch = bglu.shape[1] // 2
    n_pos = seq // tm
    row = lambda i: (i, 0)
    pos = lambda i: (i % n_pos, 0)
    q_scale = float(QK_NOPE_DIM + QK_ROPE_DIM) ** -0.5 * LOG2E
    kern = functools.partial(_proj_kernel, q_rank=q_rank, kv_rank=kv_rank, conv_ch=conv_ch, q_scale=q_scale)
    slot_w = MLA_HEADS * HEAD_SLOT
    return pl.pallas_call(
        kern,
        grid=(t // tm,),
        in_specs=[
            pl.BlockSpec((tm, d), row), _whole(g1.shape), _whole(wlat.shape, True), _whole(wu.shape, True),
            _whole(wkr.shape, True), _whole(bglu.shape),
            _whole(qg.shape), _whole(kvg.shape), _whole(wuq.shape, True), _whole(wukv.shape, True),
            pl.BlockSpec((tm, LANES), pos), pl.BlockSpec((tm, LANES), pos), pl.BlockSpec((tm, LANES), pos),
        ],
        out_specs=[
            pl.BlockSpec((tm, slot_w), row), pl.BlockSpec((tm, slot_w), row),
            pl.BlockSpec((tm, slot_w), row), pl.BlockSpec((tm, conv_ch), row),
        ],
        out_shape=[
            jax.ShapeDtypeStruct((t, slot_w), BF16), jax.ShapeDtypeStruct((t, slot_w), BF16),
            jax.ShapeDtypeStruct((t, slot_w), BF16), jax.ShapeDtypeStruct((t, conv_ch), BF16),
        ],
        compiler_params=pltpu.CompilerParams(dimension_semantics=("arbitrary",), vmem_limit_bytes=VMEM_LIMIT),
        name="proj",
    )(x2, g1, wlat, wu, wkr, bglu, qg, kvg, wuq, wukv, cos_t, s1_t, s2_t)


CONV_PAD = 16
CONV_ROWS = 128


def _attn_conv_kernel(q_ref, k_ref, v_ref, c_ref, w_ref, b_ref, o_ref, y_ref, xp_ref, *, tq, width):
    s_len = q_ref.shape[0]
    half = width // 2
    rows = min(CONV_ROWS, s_len)
    win = rows + 2 * CONV_PAD
    zeros = jnp.zeros((CONV_PAD, LANES), F32)
    xp_ref[0:CONV_PAD, :] = zeros
    xp_ref[CONV_PAD + s_len:, :] = zeros
    xp_ref[CONV_PAD:CONV_PAD + s_len, :] = c_ref[...].astype(F32)

    def conv_chunk(ci):
        base = ci * rows
        xw = xp_ref[base:base + win, :]
        acc = jnp.zeros((rows, LANES), F32)
        for r in range(SUBLANES):
            shifted = xw if r == 0 else pltpu.roll(xw, win - r, 0)
            for a0 in range(0, 2 * CONV_PAD, SUBLANES):
                k = a0 + r - (CONV_PAD - half)
                if 0 <= k < width:
                    acc = acc + shifted[a0:a0 + rows, :] * w_ref[k:k + 1, :]
        y_ref[base:base + rows, :] = (acc + b_ref[...]).astype(BF16)

    k = k_ref[...]
    v = v_ref[...]
    n_q = s_len // tq
    n_chunks = s_len // rows
    for j in range(n_q):
        qs = slice(j * tq, (j + 1) * tq)
        s = lax.dot_general(q_ref[qs, :], k, (((1,), (1,)), ((), ())), preferred_element_type=F32)
        m = jnp.max(s, axis=-1, keepdims=True)
        p = jnp.exp2(s - m).astype(BF16)
        o = _dot(p, v)
        o_ref[qs, :] = (o[:, :V_HEAD_DIM] / o[:, V_HEAD_DIM:V_HEAD_DIM + 1]).astype(BF16)
        for ci in range(j * n_chunks // n_q, (j + 1) * n_chunks // n_q):
            conv_chunk(ci)


def _attn_conv_call(q, k, v, c, w_dw, b_dw, *, batch, seq, tq):
    t, ch = c.shape
    width = w_dw.shape[0]
    assert width // 2 <= CONV_PAD and ch == MLA_HEADS * LANES
    head = lambda b, h: (b, h)
    chan = lambda b, h: (0, h)
    return pl.pallas_call(
        functools.partial(_attn_conv_kernel, tq=tq, width=width),
        grid=(batch, MLA_HEADS),
        in_specs=[pl.BlockSpec((seq, HEAD_SLOT), head), pl.BlockSpec((seq, HEAD_SLOT), head),
                  pl.BlockSpec((seq, HEAD_SLOT), head), pl.BlockSpec((seq, LANES), head),
                  pl.BlockSpec((width, LANES), chan), pl.BlockSpec((1, LANES), chan)],
        out_specs=[pl.BlockSpec((seq, V_HEAD_DIM), head), pl.BlockSpec((seq, LANES), head)],
        out_shape=[jax.ShapeDtypeStruct((t, MLA_HEADS * V_HEAD_DIM), BF16), jax.ShapeDtypeStruct((t, ch), BF16)],
        scratch_shapes=[pltpu.VMEM((seq + 2 * CONV_PAD, LANES), F32)],
        compiler_params=pltpu.CompilerParams(
            dimension_semantics=("arbitrary", "arbitrary"), vmem_limit_bytes=VMEM_LIMIT),
        name="attn_conv",
    )(q, k, v, c, w_dw, b_dw)


def _oproj_kernel(a_ref, c_ref, x_ref, wo_ref, lg_ref, lb_ref, g2_ref, wr_ref, br_ref, h_out, lg_out):
    na = a_ref.shape[1]
    y = c_ref[...].astype(F32)
    yc = y - jnp.mean(y, axis=-1, keepdims=True)
    z = yc * lax.rsqrt(jnp.mean(yc * yc, axis=-1, keepdims=True) + EPS) * lg_ref[...] + lb_ref[...]
    act = (z * jax.nn.sigmoid(z)).astype(BF16)
    h = (x_ref[...] + _dot(a_ref[...], wo_ref[:na, :].astype(BF16))
         + _dot(act, wo_ref[na:, :].astype(BF16)))
    h_out[...] = h
    hn = _rms(h, g2_ref[...])
    hi = hn.astype(BF16)
    lo = (hn - hi.astype(F32)).astype(BF16)
    r = _dot(hi, wr_ref[...])
    lg_out[...] = r[:, :LANES] + r[:, LANES:] + _dot(lo, wr_ref[:, :LANES]) + br_ref[...]


def _oproj_call(attn, conv, x2, wo, ln_g, ln_b, g2, wr, br, *, tm):
    t, d = x2.shape
    row = lambda i: (i, 0)
    return pl.pallas_call(
        _oproj_kernel,
        grid=(t // tm,),
        in_specs=[pl.BlockSpec((tm, attn.shape[1]), row), pl.BlockSpec((tm, conv.shape[1]), row),
                  pl.BlockSpec((tm, d), row), _whole(wo.shape, True), _whole(ln_g.shape), _whole(ln_b.shape),
                  _whole(g2.shape), _whole(wr.shape, True), _whole(br.shape)],
        out_specs=[pl.BlockSpec((tm, d), row), pl.BlockSpec((tm, LANES), row)],
        out_shape=[jax.ShapeDtypeStruct((t, d), F32), jax.ShapeDtypeStruct((t, LANES), F32)],
        compiler_params=pltpu.CompilerParams(dimension_semantics=("arbitrary",), vmem_limit_bytes=VMEM_LIMIT),
        name="oproj",
    )(attn, conv, x2, wo, ln_g, ln_b, g2, wr, br)


def _route_tokens_on_lanes(lt):
    shape = (SUBLANES, LANES)
    row = lax.broadcasted_iota(I32, shape, 0)
    big = jnp.int32(1 << 20)
    neg = jnp.float32(-jnp.inf)

    def top(v):
        m = jnp.max(v, axis=0, keepdims=True)
        return m, jnp.min(jnp.where(v == m, row, big), axis=0, keepdims=True)

    lgrp = lt[N_EXPERTS:N_EXPERTS + N_EXPERT_GROUPS, :]
    gmax, gsel = top(lgrp)
    p_g = 1.0 / jnp.sum(jnp.exp(lgrp - gmax), axis=0, keepdims=True)
    le = jnp.zeros(shape, F32)
    for g in range(N_EXPERT_GROUPS):
        le = jnp.where(gsel == g, lt[g * EXPERTS_PER_GROUP:(g + 1) * EXPERTS_PER_GROUP, :], le)
    m1, i1 = top(le)
    m2, i2 = top(jnp.where(row == i1, neg, le))
    r = jnp.exp(m2 - m1)
    w1 = 1.0 / (1.0 + r)
    w2 = r / (1.0 + r)
    base = gsel * EXPERTS_PER_GROUP
    return base + i1, base + i2, p_g * w1, p_g * w2


def _lane_cumsum(v):
    lane = lax.broadcasted_iota(I32, v.shape, 1)
    sh = 1
    while sh < LANES:
        v = v + jnp.where(lane >= sh, pltpu.roll(v, sh, 1), 0)
        sh *= 2
    return v


def _route_kernel(lg_ref, d0_out, d1_out, gate_out, meta_out, e_ref, *, rows_per_block):
    n_chunks = d0_out.shape[0]
    shift = rows_per_block.bit_length() - 1
    sub_shift = SUBLANES.bit_length() - 1
    sq = (LANES, LANES)
    row = lax.broadcasted_iota(I32, sq, 0)
    row8 = lax.broadcasted_iota(I32, (SUBLANES, LANES), 0)

    def one_hots(e1, e2):
        return (row == e1).astype(F32), (row == e2).astype(F32)

    def count_step(i, cnt):
        base = pl.multiple_of(i * LANES, LANES)
        e1, e2, g1, g2 = _route_tokens_on_lanes(jnp.transpose(lg_ref[pl.ds(base, LANES), :]))
        e_ref[i] = jnp.where(row8 == 0, e1, jnp.where(row8 == 1, e2, 0))
        gate_out[pl.ds(base, LANES), :] = jnp.transpose(jnp.where(row == 0, g1, jnp.where(row == 1, g2, 0.0)))
        oh1, oh2 = one_hots(e1, e2)
        return cnt + jnp.sum(oh1 + oh2, axis=1, keepdims=True)

    unroll = ROUTE_UNROLL if n_chunks % ROUTE_UNROLL == 0 else 1
    counts_col = lax.fori_loop(0, n_chunks, count_step, jnp.zeros((LANES, 1), F32), unroll=unroll)
    counts = jnp.transpose(jnp.broadcast_to(counts_col, sq))[0:SUBLANES, :].astype(I32)
    padded = ((counts + (SUBLANES - 1)) >> sub_shift) << sub_shift
    pad_end = _lane_cumsum(padded)
    pad_start = pad_end - padded
    start_col = jnp.transpose(jnp.broadcast_to(pad_start[0:1, :].astype(F32), sq))[:, 0:1]

    tri = (row < lax.broadcasted_iota(I32, sq, 1)).astype(BF16)

    def dest_step(i, carry):
        er = e_ref[i]
        oh1, oh2 = one_hots(er[0:1, :], er[1:2, :])
        oh = oh1 + oh2
        pos = carry + _dot(oh.astype(BF16), tri)
        d0_out[i] = jnp.sum(oh1 * pos, axis=0, keepdims=True).astype(I32)
        d1_out[i] = jnp.sum(oh2 * pos, axis=0, keepdims=True).astype(I32)
        return carry + jnp.sum(oh, axis=1, keepdims=True)

    lax.fori_loop(0, n_chunks, dest_step, start_col, unroll=unroll)

    nbp = meta_out.shape[0]
    lane_b = lax.broadcasted_iota(I32, (nbp, LANES), 1)
    blk = lax.broadcasted_iota(I32, (nbp, LANES), 0)
    nblk = (counts + (rows_per_block - 1)) >> shift
    blk_end = _lane_cumsum(nblk)
    bcast = lambda v: jnp.broadcast_to(v[0:1, :], (nbp, LANES))
    be_end, be_start, cn, ps = bcast(blk_end), bcast(blk_end - nblk), bcast(counts), bcast(pad_start)
    is_e = lane_b < N_EXPERTS
    lsum = lambda v: jnp.sum(v, axis=-1, keepdims=True)
    n_used = jnp.max(be_end, axis=-1, keepdims=True)
    total = jnp.max(bcast(pad_end), axis=-1, keepdims=True)
    last_e = jnp.max(jnp.where(is_e & (cn > 0), lane_b, 0), axis=-1, keepdims=True)
    be = jnp.minimum(lsum(jnp.where(is_e & (be_end <= blk), 1, 0)), last_e)
    sel = lane_b == be
    first_row = (blk[:, 0:1] - lsum(jnp.where(sel, be_start, 0))) * rows_per_block
    used = blk[:, 0:1] < n_used
    nvalid = jnp.where(used, jnp.clip(lsum(jnp.where(sel, cn, 0)) - first_row, 0, rows_per_block), 0)
    row0 = jnp.where(used, lsum(jnp.where(sel, ps, 0)) + first_row, 0)
    cols = (be, nvalid, n_used, row0, total)
    meta = jnp.zeros((nbp, LANES), I32)
    for c, v in enumerate(cols):
        meta = jnp.where(lane_b == c, v, meta)
    meta_out[...] = meta


def _route_call(logits, *, rows_per_block, n_blocks):
    t = logits.shape[0]
    assert t % LANES == 0
    nbp = -(-n_blocks // SUBLANES) * SUBLANES
    dshape = (t // LANES, 1, LANES)
    d0, d1, gates, meta = pl.pallas_call(
        functools.partial(_route_kernel, rows_per_block=rows_per_block),
        in_specs=[_whole(logits.shape)],
        out_specs=[_whole(dshape), _whole(dshape), _whole((t, LANES)), _whole((nbp, LANES))],
        out_shape=[jax.ShapeDtypeStruct(dshape, I32), jax.ShapeDtypeStruct(dshape, I32),
                   jax.ShapeDtypeStruct((t, LANES), F32), jax.ShapeDtypeStruct((nbp, LANES), I32)],
        grid=(1,),
        scratch_shapes=[pltpu.VMEM((t // LANES, SUBLANES, LANES), I32)],
        compiler_params=pltpu.CompilerParams(dimension_semantics=("arbitrary",), vmem_limit_bytes=VMEM_LIMIT),
        name="route",
    )(logits)
    return d0.reshape(t), d1.reshape(t), gates, meta


def _pow2_chunks(limit):
    c = 1 << (limit.bit_length() - 1)
    while c >= 1:
        yield c
        c >>= 1


def _expert_kernel(be_ref, nv_ref, nused_ref, row0_ref, total_ref, d0_ref, d1_ref,
                   h_hbm, g2_ref, wg_hbm, wu_hbm, wd_hbm, y_hbm,
                   tok_ref, ord_ref, exp_ref, nexp_ref, xbuf, ybuf, zbuf, wg_buf, wu_buf, wd_buf,
                   gsem, ysem, zsem, wsem):
    n_used = nused_ref[0]
    groups = xbuf.shape[1]
    rows = groups * SUBLANES
    d = xbuf.shape[3]
    n_tok = d0_ref.shape[0]
    sub_shift = SUBLANES.bit_length() - 1
    n_slots = wg_buf.shape[0]

    def y_copies(blk, slot):
        ng = lax.shift_right_logical(nv_ref[blk] + (SUBLANES - 1), sub_shift)
        g0 = lax.shift_right_logical(row0_ref[blk], sub_shift)
        out = []
        for c in _pow2_chunks(groups):
            off = ng & ~(2 * c - 1)
            copy = pltpu.make_async_copy(ybuf.at[slot, pl.ds(off, c)], y_hbm.at[pl.ds(g0 + off, c)], ysem.at[slot])
            out.append(((ng & c) != 0, copy))
        return out

    def start_y(blk, slot):
        for cond, copy in y_copies(blk, slot):
            pl.when(cond)(copy.start)

    def wait_y(blk, slot):
        for cond, copy in y_copies(blk, slot):
            pl.when(cond)(copy.wait)

    def fill_tail():
        zbuf[...] = jnp.zeros(zbuf.shape, F32)
        first = lax.shift_right_logical(total_ref[0], sub_shift)
        copy = lambda g: pltpu.make_async_copy(zbuf, y_hbm.at[g], zsem.at[0])
        lax.fori_loop(first, y_hbm.shape[0], lambda g, c: (copy(g).start(), c)[1], 0)
        lax.fori_loop(first, y_hbm.shape[0], lambda g, c: (copy(g).wait(), c)[1], 0)

    def weight_copies(j):
        e = exp_ref[j]
        slot = lax.rem(j, n_slots)
        return [pltpu.make_async_copy(src.at[e], dst.at[slot], wsem.at[slot])
                for src, dst in ((wg_hbm, wg_buf), (wu_hbm, wu_buf), (wd_hbm, wd_buf))]

    def start_weights(j):
        for c in weight_copies(j):
            c.start(priority=1)

    def wait_weights(j):
        for c in weight_copies(j):
            c.wait()

    def gather_copy(blk, slot, g, u):
        tok = tok_ref[row0_ref[blk] + g * SUBLANES + u]
        return pltpu.make_async_copy(h_hbm.at[pl.ds(tok, 1)], xbuf.at[slot, g, pl.ds(u, 1)], gsem.at[slot])

    def start_gather(blk, slot):
        n = nv_ref[blk]

        def group(g, c):
            for u in range(SUBLANES):
                gather_copy(blk, slot, g, u).start()
            return c
        full = lax.shift_right_logical(n, sub_shift)
        lax.fori_loop(0, full, group, 0)
        for u in range(SUBLANES - 1):
            @pl.when(full * SUBLANES + u < n)
            def _():
                gather_copy(blk, slot, full, u).start()

    def wait_gather(blk, slot):
        n = nv_ref[blk]
        buf = xbuf.at[slot]
        for c in _pow2_chunks(rows):
            @pl.when((n & c) != 0)
            def _():
                if c >= SUBLANES:
                    part = buf.at[pl.ds(0, c // SUBLANES)]
                else:
                    part = buf.at[0, pl.ds(0, c)]
                pltpu.make_async_copy(part, part, gsem.at[slot]).wait()

    def scan(blk, j):
        e = be_ref[blk]
        is_new = jnp.logical_or(blk == 0, e != be_ref[jnp.maximum(blk - 1, 0)])
        j = j + is_new.astype(I32)
        ord_ref[blk] = j - 1

        @pl.when(is_new)
        def _():
            exp_ref[j - 1] = e
        return j
    n_exp = lax.fori_loop(0, n_used, scan, jnp.int32(0))
    nexp_ref[0] = n_exp
    for j in range(n_slots):
        @pl.when(j < n_exp)
        def _():
            start_weights(j)

    def inv(g, c):
        for u in range(DMA_UNROLL):
            tk = g * DMA_UNROLL + u
            tok_ref[d0_ref[tk]] = tk
            tok_ref[d1_ref[tk]] = tk
        return c
    lax.fori_loop(0, n_tok // DMA_UNROLL, inv, 0)
    xbuf[...] = jnp.zeros(xbuf.shape, F32)
    start_gather(0, 0)

    def block(b, carry):
        slot = b & 1
        wait_gather(b, slot)

        @pl.when(b + 1 < n_used)
        def _():
            start_gather(b + 1, 1 - slot)

        j = ord_ref[b]

        @pl.when(jnp.logical_or(b == 0, ord_ref[jnp.maximum(b - 1, 0)] != j))
        def _():
            wait_weights(j)

            @pl.when(jnp.logical_and(j >= 1, j + (n_slots - 1) < nexp_ref[0]))
            def _():
                start_weights(j + (n_slots - 1))

        ws = lax.rem(j, n_slots)
        hn = _rms(xbuf[slot].reshape(rows, d), g2_ref[...]).astype(BF16)
        gate = _dot(hn, wg_buf[ws].astype(BF16))
        up = _dot(hn, wu_buf[ws].astype(BF16))
        hmid = (gate * jax.nn.sigmoid(gate) * up).astype(BF16)
        y = _dot(hmid, wd_buf[ws].astype(BF16))

        @pl.when(b >= 2)
        def _():
            wait_y(b - 2, slot)

        ybuf[slot] = y.reshape(groups, SUBLANES, d)
        start_y(b, slot)
        return carry

    lax.fori_loop(0, n_used, block, 0)
    last = n_used - 1

    @pl.when(last >= 1)
    def _():
        wait_y(last - 1, 1 - (last & 1))
    wait_y(last, last & 1)
    fill_tail()


def _expert_call(be, nvalid, n_used, row0, total, dest0, dest1, h, g2, w_gate, w_up, w_down, *,
                 rows_per_block, n_blocks):
    t, d = h.shape
    n_exp, _, f = w_gate.shape
    assert t % DMA_UNROLL == 0 and rows_per_block % SUBLANES == 0 and (2 * t) % SUBLANES == 0
    groups = rows_per_block // SUBLANES
    sorted_groups = 2 * t // SUBLANES + n_exp
    hbm = pl.BlockSpec(memory_space=pl.ANY)
    slots = EXPERT_WEIGHT_SLOTS
    grid_spec = pltpu.PrefetchScalarGridSpec(
        num_scalar_prefetch=7,
        grid=(1,),
        in_specs=[hbm, pl.BlockSpec(g2.shape, lambda b, *_: (0, 0)), hbm, hbm, hbm],
        out_specs=hbm,
        scratch_shapes=[
            pltpu.SMEM((sorted_groups * SUBLANES,), I32),
            pltpu.SMEM((n_blocks,), I32), pltpu.SMEM((n_exp,), I32), pltpu.SMEM((1,), I32),
            pltpu.VMEM((2, groups, SUBLANES, d), F32), pltpu.VMEM((2, groups, SUBLANES, d), F32),
            pltpu.VMEM((SUBLANES, d), F32),
            pltpu.VMEM((slots, d, f), F32), pltpu.VMEM((slots, d, f), F32), pltpu.VMEM((slots, f, d), F32),
            pltpu.SemaphoreType.DMA((2,)), pltpu.SemaphoreType.DMA((2,)), pltpu.SemaphoreType.DMA((1,)),
            pltpu.SemaphoreType.DMA((slots,)),
        ],
    )
    y = pl.pallas_call(
        _expert_kernel,
        grid_spec=grid_spec,
        out_shape=jax.ShapeDtypeStruct((sorted_groups, SUBLANES, d), F32),
        compiler_params=pltpu.CompilerParams(dimension_semantics=("arbitrary",), vmem_limit_bytes=VMEM_LIMIT),
        name="experts",
    )(be, nvalid, n_used, row0, total, dest0, dest1, h, g2, w_gate, w_up, w_down)
    return y.reshape(sorted_groups * SUBLANES, d)


def _final_kernel(d0_ref, d1_ref, h_ref, y_hbm, gate_ref, fg_ref, o_ref, ybuf, sem):
    i = pl.program_id(0)
    n = pl.num_programs(0)
    n_slots = ybuf.shape[0]
    groups = ybuf.shape[2]
    tm = groups * SUBLANES

    def issue_group(tile, slot, g):
        for u in range(SUBLANES):
            tk = tile * tm + g * SUBLANES + u
            for k, dref in enumerate((d0_ref, d1_ref)):
                pltpu.make_async_copy(y_hbm.at[pl.ds(dref[tk], 1)], ybuf.at[slot, k, g, pl.ds(u, 1)],
                                      sem.at[slot]).start()

    def combine_group(slot, g, issue):
        r0 = pl.multiple_of(g * SUBLANES, SUBLANES)
        gt = gate_ref[pl.ds(r0, SUBLANES), :]
        out = h_ref[pl.ds(r0, SUBLANES), :] + gt[:, 0:1] * ybuf[slot, 0, g] + gt[:, 1:2] * ybuf[slot, 1, g]
        issue()
        o_ref[pl.ds(r0, SUBLANES), :] = _rms(out, fg_ref[...])

    @pl.when(i == 0)
    def _():
        for tile in range(n_slots - 1):
            @pl.when(tile < n)
            def _():
                lax.fori_loop(0, groups, lambda g, c: (issue_group(tile, tile, g), c)[1], 0)

    slot = lax.rem(i, n_slots)
    pltpu.make_async_copy(ybuf.at[slot], ybuf.at[slot], sem.at[slot]).wait()
    ahead = i + (n_slots - 1)
    unroll = FINAL_UNROLL if groups % FINAL_UNROLL == 0 else 1

    @pl.when(ahead < n)
    def _():
        nslot = lax.rem(ahead, n_slots)
        lax.fori_loop(0, groups, lambda g, c: (combine_group(
            slot, g, lambda: issue_group(ahead, nslot, g)), c)[1], 0, unroll=unroll)

    @pl.when(ahead >= n)
    def _():
        lax.fori_loop(0, groups, lambda g, c: (combine_group(slot, g, lambda: None), c)[1], 0, unroll=unroll)


def _final_call(dest0, dest1, h, y, gates, fg, *, tm):
    t, d = h.shape
    assert tm % SUBLANES == 0
    row = lambda i, *_: (i, 0)
    grid_spec = pltpu.PrefetchScalarGridSpec(
        num_scalar_prefetch=2,
        grid=(t // tm,),
        in_specs=[pl.BlockSpec((tm, d), row), pl.BlockSpec(memory_space=pl.ANY),
                  pl.BlockSpec((tm, LANES), row), pl.BlockSpec(fg.shape, lambda i, *_: (0, 0))],
        out_specs=pl.BlockSpec((tm, d), row),
        scratch_shapes=[pltpu.VMEM((FINAL_SLOTS, 2, tm // SUBLANES, SUBLANES, d), F32),
                        pltpu.SemaphoreType.DMA((FINAL_SLOTS,))],
    )
    return pl.pallas_call(
        _final_kernel,
        grid_spec=grid_spec,
        out_shape=jax.ShapeDtypeStruct((t, d), F32),
        compiler_params=pltpu.CompilerParams(dimension_semantics=("arbitrary",), vmem_limit_bytes=VMEM_LIMIT),
        name="final",
    )(dest0, dest1, h, y, gates, fg)


def _wsplit_kernel(lat_ref, u_ref, kr_ref, lat_out, u_out, kr_out, *, n_lat, n_kr):
    j = pl.program_id(0)
    u_out[...] = jnp.transpose(u_ref[...]).astype(BF16)

    @pl.when(j < n_lat)
    def _():
        lat_out[...] = jnp.transpose(lat_ref[...]).astype(BF16)

    @pl.when(j == 0)
    def _():
        kr = jnp.transpose(kr_ref[...])
        lane = lax.broadcasted_iota(I32, kr.shape, 1)
        kr_out[...] = jnp.where(lane < n_kr, kr, 0.0).astype(BF16)


def _wsplit_call(w_t, *, o_kr, o_u):
    cols, d = w_t.shape
    assert o_kr % LANES == 0 and (cols - o_u) % LANES == 0 and o_u % SUBLANES == 0
    blk = 2 * LANES if o_kr % (2 * LANES) == 0 and (cols - o_u) % (2 * LANES) == 0 else LANES
    n_u = (cols - o_u) // blk
    n_lat = o_kr // blk
    assert 0 < o_u - o_kr <= LANES and n_lat <= n_u and o_kr + LANES <= cols
    lat_blk = lambda j: jnp.minimum(j, n_lat - 1)
    return pl.pallas_call(
        functools.partial(_wsplit_kernel, n_lat=n_lat, n_kr=o_u - o_kr),
        grid=(n_u,),
        in_specs=[pl.BlockSpec((blk, d), lambda j: (lat_blk(j), 0)),
                  pl.BlockSpec((pl.Element(blk), pl.Element(d)),
                               lambda j: (pl.multiple_of(o_u + j * blk, SUBLANES), 0)),
                  pl.BlockSpec((pl.Element(LANES), pl.Element(d)), lambda j: (o_kr, 0))],
        out_specs=[pl.BlockSpec((d, blk), lambda j: (0, lat_blk(j))), pl.BlockSpec((d, blk), lambda j: (0, j)),
                   pl.BlockSpec((d, LANES), lambda j: (0, 0))],
        out_shape=[jax.ShapeDtypeStruct((d, o_kr), BF16), jax.ShapeDtypeStruct((d, cols - o_u), BF16),
                   jax.ShapeDtypeStruct((d, LANES), BF16)],
        compiler_params=pltpu.CompilerParams(dimension_semantics=("arbitrary",), vmem_limit_bytes=VMEM_LIMIT),
        name="wsplit",
    )(w_t, w_t, w_t)


def _rope_tables(seq):
    pos = np.arange(seq, dtype=np.float64)
    inv_freq = ROPE_THETA ** (-np.arange(0, QK_ROPE_DIM, 2, dtype=np.float64) / QK_ROPE_DIM)
    ang = pos[:, None] * inv_freq[None, :]
    cos, sin = np.cos(ang).astype(np.float32), np.sin(ang).astype(np.float32)
    zero = np.zeros_like(sin)
    cos_t = np.concatenate([cos, cos, cos, cos], axis=1)
    s1_t = np.concatenate([zero, sin, zero, sin], axis=1)
    s2_t = np.concatenate([-sin, zero, -sin, zero], axis=1)
    return jnp.asarray(cos_t), jnp.asarray(s1_t), jnp.asarray(s2_t)


def kernel(x, ln1_g, w_in, b_glu, q_norm_g, w_uq, kv_norm_g, w_ukv, w_dw, b_dw, conv_ln_g, conv_ln_b,
           w_o, ln2_g, w_group, b_group, w_router, b_router, w_gate, w_up, w_down, final_g):
    batch, seq, d = x.shape
    assert ln1_g.shape[0] == 1, "single-layer trunk"
    t = batch * seq
    q_rank = q_norm_g.shape[1]
    kv_rank = kv_norm_g.shape[1]
    x2 = x.reshape(t, d)

    wi = w_in[0]
    o_kr = q_rank + kv_rank
    o_u = o_kr + QK_ROPE_DIM
    wlat, wu, wkr = _wsplit_call(jnp.transpose(wi), o_kr=o_kr, o_u=o_u)
    wuq = w_uq[0].reshape(q_rank, MLA_HEADS, QK_NOPE_DIM + QK_ROPE_DIM)
    wuq = jnp.concatenate([wuq[:, :, :QK_NOPE_DIM].reshape(q_rank, MLA_HEADS * QK_NOPE_DIM),
                           wuq[:, :, QK_NOPE_DIM:].reshape(q_rank, MLA_HEADS * QK_ROPE_DIM)], axis=1).astype(BF16)
    wukv = w_ukv[0]
    wo = w_o[0]
    wr = jnp.concatenate([w_router[0], w_group[0],
                          jnp.zeros((d, LANES - N_EXPERTS - N_EXPERT_GROUPS), F32)], axis=1)
    wr_hi = wr.astype(BF16)
    wr_lo = (wr - wr_hi.astype(F32)).astype(BF16)
    wr2 = jnp.concatenate([wr_hi, wr_lo], axis=1)
    br = jnp.concatenate([b_router[0], b_group[0],
                          jnp.zeros((LANES - N_EXPERTS - N_EXPERT_GROUPS,), F32)])[None, :]
    cos_t, s1_t, s2_t = _rope_tables(seq)

    tm = min(512, seq)
    q, k, v, c = _proj_call(x2, ln1_g, wlat, wu, wkr, b_glu, q_norm_g, kv_norm_g, wuq, wukv, cos_t, s1_t, s2_t,
                            seq=seq, tm=tm)
    attn, conv = _attn_conv_call(q, k, v, c, w_dw[0], b_dw, batch=batch, seq=seq, tq=min(256, seq))
    h, logits = _oproj_call(attn, conv, x2, wo, conv_ln_g, conv_ln_b, ln2_g, wr2, br, tm=tm)

    n_blocks = -(-(2 * t + N_EXPERTS * (MOE_ROWS - 1)) // MOE_ROWS)
    dest0, dest1, gates, meta = _route_call(logits, rows_per_block=MOE_ROWS, n_blocks=n_blocks)
    y = _expert_call(meta[:n_blocks, 0], meta[:n_blocks, 1], meta[0:1, 2], meta[:n_blocks, 3], meta[0:1, 4],
                     dest0, dest1, h, ln2_g, w_gate[0], w_up[0], w_down[0],
                     rows_per_block=MOE_ROWS, n_blocks=n_blocks)
    out = _final_call(dest0, dest1, h, y, gates, final_g[None, :], tm=tm)
    return out.reshape(batch, seq, d)
```

```python
import functools

import jax
import jax.numpy as jnp
import numpy as np
from jax import lax
from jax.experimental import pallas as pl
from jax.experimental.pallas import tpu as pltpu

F32 = jnp.float32
BF16 = jnp.bfloat16
I32 = jnp.int32

MLA_HEADS = 8
QK_NOPE_DIM = 128
QK_ROPE_DIM = 64
V_HEAD_DIM = 128
ROPE_THETA = 10000.0
N_EXPERT_GROUPS = 8
EXPERTS_PER_GROUP = 8
N_EXPERTS = N_EXPERT_GROUPS * EXPERTS_PER_GROUP
EPS = 1e-6
LOG2E = 1.4426950408889634

LANES = 128
SUBLANES = 8
HEAD_SLOT = 2 * LANES
ROPE_HALF = QK_ROPE_DIM // 2
VMEM_LIMIT = 56 * 1024 * 1024

MOE_ROWS = 256
FINAL_SLOTS = 3
FINAL_UNROLL = 8
ROUTE_UNROLL = 4
EXPERT_WEIGHT_SLOTS = 3
DMA_UNROLL = 8


def _rms(x, g):
    return x * lax.rsqrt(jnp.mean(x * x, axis=-1, keepdims=True) + EPS) * g


def _dot(a, b):
    return jnp.dot(a, b, preferred_element_type=F32)


def _whole(shape, single=False):
    mode = dict(pipeline_mode=pl.Buffered(1)) if single else {}
    return pl.BlockSpec(shape, lambda *_: (0,) * len(shape), **mode)


def _proj_kernel(x_ref, g1_ref, wlat_ref, wu_ref, wkr_ref, bglu_ref, qg_ref, kvg_ref, wuq_ref, wukv_ref,
                 cos_ref, s1_ref, s2_ref, q_out, k_out, v_out, c_out, *, q_rank, kv_rank, conv_ch, q_scale):
    xn = _rms(x_ref[...], g1_ref[...]).astype(BF16)
    cos = cos_ref[...]
    s1 = s1_ref[...]
    s2 = s2_ref[...]

    def rope(t):
        return t * cos + pltpu.roll(t, ROPE_HALF, 1) * s1 + pltpu.roll(t, LANES - ROPE_HALF, 1) * s2

    a = _dot(xn, wu_ref[:, :conv_ch]) + bglu_ref[:, :conv_ch]
    gate = _dot(xn, wu_ref[:, conv_ch:]) + bglu_ref[:, conv_ch:]
    c_out[...] = (a * jax.nn.sigmoid(gate)).astype(BF16)

    kr = rope(_dot(xn, wkr_ref[...])).astype(BF16)
    qn = _rms(_dot(xn, wlat_ref[:, :q_rank]), qg_ref[...]).astype(BF16)
    kvn = _rms(_dot(xn, wlat_ref[:, q_rank:q_rank + kv_rank]), kvg_ref[...]).astype(BF16)
    lane = lax.broadcasted_iota(I32, (x_ref.shape[0], LANES), 1)
    ones_blk = (lane == 0).astype(BF16)
    n_nope = MLA_HEADS * QK_NOPE_DIM
    q_nope = _dot(qn, wuq_ref[:, :n_nope])
    q_rope = _dot(qn, wuq_ref[:, n_nope:])
    for h in range(MLA_HEADS):
        c0 = h * HEAD_SLOT
        q_out[:, c0:c0 + LANES] = (q_nope[:, h * QK_NOPE_DIM:(h + 1) * QK_NOPE_DIM] * q_scale).astype(BF16)
        if h % 2 == 0:
            pair = rope(q_rope[:, (h // 2) * LANES:(h // 2 + 1) * LANES]) * q_scale
        half_pair = pair if h % 2 == 0 else pltpu.roll(pair, LANES - QK_ROPE_DIM, 1)
        q_out[:, c0 + LANES:c0 + HEAD_SLOT] = jnp.where(lane < QK_ROPE_DIM, half_pair, 0.0).astype(BF16)
        kvh = _dot(kvn, wukv_ref[:, c0:c0 + HEAD_SLOT].astype(BF16))
        k_out[:, c0:c0 + LANES] = kvh[:, :LANES].astype(BF16)
        k_out[:, c0 + LANES:c0 + HEAD_SLOT] = kr
        v_out[:, c0:c0 + LANES] = kvh[:, LANES:].astype(BF16)
        v_out[:, c0 + LANES:c0 + HEAD_SLOT] = ones_blk


def _proj_call(x2, g1, wlat, wu, wkr, bglu, qg, kvg, wuq, wukv, cos_t, s1_t, s2_t, *, seq, tm):
    t, d = x2.shape
    q_rank, kv_rank = qg.shape[1], kvg.shape[1]
    conv_ch = bglu.shape[1] // 2
    n_pos = seq // tm
    row = lambda i: (i, 0)
    pos = lambda i: (i % n_pos, 0)
    q_scale = float(QK_NOPE_DIM + QK_ROPE_DIM) ** -0.5 * LOG2E
    kern = functools.partial(_proj_kernel, q_rank=q_rank, kv_rank=kv_rank, conv_ch=conv_ch, q_scale=q_scale)
    slot_w = MLA_HEADS * HEAD_SLOT
    return pl.pallas_call(
        kern,
        grid=(t // tm,),
        in_specs=[
            pl.BlockSpec((tm, d), row), _whole(g1.shape), _whole(wlat.shape, True), _whole(wu.shape, True),
            _whole(wkr.shape, True), _whole(bglu.shape),
            _whole(qg.shape), _whole(kvg.shape), _whole(wuq.shape, True), _whole(wukv.shape, True),
            pl.BlockSpec((tm, LANES), pos), pl.BlockSpec((tm, LANES), pos), pl.BlockSpec((tm, LANES), pos),
        ],
        out_specs=[
            pl.BlockSpec((tm, slot_w), row), pl.BlockSpec((tm, slot_w), row),
            pl.BlockSpec((tm, slot_w), row), pl.BlockSpec((tm, conv_ch), row),
        ],
        out_shape=[
            jax.ShapeDtypeStruct((t, slot_w), BF16), jax.ShapeDtypeStruct((t, slot_w), BF16),
            jax.ShapeDtypeStruct((t, slot_w), BF16), jax.ShapeDtypeStruct((t, conv_ch), BF16),
        ],
        compiler_params=pltpu.CompilerParams(dimension_semantics=("arbitrary",), vmem_limit_bytes=VMEM_LIMIT),
        name="proj",
    )(x2, g1, wlat, wu, wkr, bglu, qg, kvg, wuq, wukv, cos_t, s1_t, s2_t)


CONV_PAD = 16
CONV_ROWS = 128


def _attn_conv_kernel(q_ref, k_ref, v_ref, c_ref, w_ref, b_ref, o_ref, y_ref, xp_ref, *, tq, width):
    s_len = q_ref.shape[0]
    half = width // 2
    rows = min(CONV_ROWS, s_len)
    win = rows + 2 * CONV_PAD
    zeros = jnp.zeros((CONV_PAD, LANES), F32)
    xp_ref[0:CONV_PAD, :] = zeros
    xp_ref[CONV_PAD + s_len:, :] = zeros
    xp_ref[CONV_PAD:CONV_PAD + s_len, :] = c_ref[...].astype(F32)

    def conv_chunk(ci):
        base = ci * rows
        xw = xp_ref[base:base + win, :]
        acc = jnp.zeros((rows, LANES), F32)
        for r in range(SUBLANES):
            shifted = xw if r == 0 else pltpu.roll(xw, win - r, 0)
            for a0 in range(0, 2 * CONV_PAD, SUBLANES):
                k = a0 + r - (CONV_PAD - half)
                if 0 <= k < width:
                    acc = acc + shifted[a0:a0 + rows, :] * w_ref[k:k + 1, :]
        y_ref[base:base + rows, :] = (acc + b_ref[...]).astype(BF16)

    k = k_ref[...]
    v = v_ref[...]
    n_q = s_len // tq
    n_chunks = s_len // rows
    for j in range(n_q):
        qs = slice(j * tq, (j + 1) * tq)
        s = lax.dot_general(q_ref[qs, :], k, (((1,), (1,)), ((), ())), preferred_element_type=F32)
        m = jnp.max(s, axis=-1, keepdims=True)
        p = jnp.exp2(s - m).astype(BF16)
        o = _dot(p, v)
        o_ref[qs, :] = (o[:, :V_HEAD_DIM] / o[:, V_HEAD_DIM:V_HEAD_DIM + 1]).astype(BF16)
        for ci in range(j * n_chunks // n_q, (j + 1) * n_chunks // n_q):
            conv_chunk(ci)


def _attn_conv_call(q, k, v, c, w_dw, b_dw, *, batch, seq, tq):
    t, ch = c.shape
    width = w_dw.shape[0]
    assert width // 2 <= CONV_PAD and ch == MLA_HEADS * LANES
    head = lambda b, h: (b, h)
    chan = lambda b, h: (0, h)
    return pl.pallas_call(
        functools.partial(_attn_conv_kernel, tq=tq, width=width),
        grid=(batch, MLA_HEADS),
        in_specs=[pl.BlockSpec((seq, HEAD_SLOT), head), pl.BlockSpec((seq, HEAD_SLOT), head),
                  pl.BlockSpec((seq, HEAD_SLOT), head), pl.BlockSpec((seq, LANES), head),
                  pl.BlockSpec((width, LANES), chan), pl.BlockSpec((1, LANES), chan)],
        out_specs=[pl.BlockSpec((seq, V_HEAD_DIM), head), pl.BlockSpec((seq, LANES), head)],
        out_shape=[jax.ShapeDtypeStruct((t, MLA_HEADS * V_HEAD_DIM), BF16), jax.ShapeDtypeStruct((t, ch), BF16)],
        scratch_shapes=[pltpu.VMEM((seq + 2 * CONV_PAD, LANES), F32)],
        compiler_params=pltpu.CompilerParams(
            dimension_semantics=("arbitrary", "arbitrary"), vmem_limit_bytes=VMEM_LIMIT),
        name="attn_conv",
    )(q, k, v, c, w_dw, b_dw)


def _oproj_kernel(a_ref, c_ref, x_ref, wo_ref, lg_ref, lb_ref, g2_ref, wr_ref, br_ref, h_out, lg_out):
    na = a_ref.shape[1]
    y = c_ref[...].astype(F32)
    yc = y - jnp.mean(y, axis=-1, keepdims=True)
    z = yc * lax.rsqrt(jnp.mean(yc * yc, axis=-1, keepdims=True) + EPS) * lg_ref[...] + lb_ref[...]
    act = (z * jax.nn.sigmoid(z)).astype(BF16)
    h = (x_ref[...] + _dot(a_ref[...], wo_ref[:na, :].astype(BF16))
         + _dot(act, wo_ref[na:, :].astype(BF16)))
    h_out[...] = h
    hn = _rms(h, g2_ref[...])
    hi = hn.astype(BF16)
    lo = (hn - hi.astype(F32)).astype(BF16)
    r = _dot(hi, wr_ref[...])
    lg_out[...] = r[:, :LANES] + r[:, LANES:] + _dot(lo, wr_ref[:, :LANES]) + br_ref[...]


def _oproj_call(attn, conv, x2, wo, ln_g, ln_b, g2, wr, br, *, tm):
    t, d = x2.shape
    row = lambda i: (i, 0)
    return pl.pallas_call(
        _oproj_kernel,
        grid=(t // tm,),
        in_specs=[pl.BlockSpec((tm, attn.shape[1]), row), pl.BlockSpec((tm, conv.shape[1]), row),
                  pl.BlockSpec((tm, d), row), _whole(wo.shape, True), _whole(ln_g.shape), _whole(ln_b.shape),
                  _whole(g2.shape), _whole(wr.shape, True), _whole(br.shape)],
        out_specs=[pl.BlockSpec((tm, d), row), pl.BlockSpec((tm, LANES), row)],
        out_shape=[jax.ShapeDtypeStruct((t, d), F32), jax.ShapeDtypeStruct((t, LANES), F32)],
        compiler_params=pltpu.CompilerParams(dimension_semantics=("arbitrary",), vmem_limit_bytes=VMEM_LIMIT),
        name="oproj",
    )(attn, conv, x2, wo, ln_g, ln_b, g2, wr, br)


def _route_tokens_on_lanes(lt):
    shape = (SUBLANES, LANES)
    row = lax.broadcasted_iota(I32, shape, 0)
    big = jnp.int32(1 << 20)
    neg = jnp.float32(-jnp.inf)

    def top(v):
        m = jnp.max(v, axis=0, keepdims=True)
        return m, jnp.min(jnp.where(v == m, row, big), axis=0, keepdims=True)

    lgrp = lt[N_EXPERTS:N_EXPERTS + N_EXPERT_GROUPS, :]
    gmax, gsel = top(lgrp)
    p_g = 1.0 / jnp.sum(jnp.exp(lgrp - gmax), axis=0, keepdims=True)
    le = jnp.zeros(shape, F32)
    for g in range(N_EXPERT_GROUPS):
        le = jnp.where(gsel == g, lt[g * EXPERTS_PER_GROUP:(g + 1) * EXPERTS_PER_GROUP, :], le)
    m1, i1 = top(le)
    m2, i2 = top(jnp.where(row == i1, neg, le))
    r = jnp.exp(m2 - m1)
    w1 = 1.0 / (1.0 + r)
    w2 = r / (1.0 + r)
    base = gsel * EXPERTS_PER_GROUP
    return base + i1, base + i2, p_g * w1, p_g * w2


def _lane_cumsum(v):
    lane = lax.broadcasted_iota(I32, v.shape, 1)
    sh = 1
    while sh < LANES:
        v = v + jnp.where(lane >= sh, pltpu.roll(v, sh, 1), 0)
        sh *= 2
    return v


def _route_kernel(lg_ref, d0_out, d1_out, gate_out, meta_out, e_ref, *, rows_per_block):
    n_chunks = d0_out.shape[0]
    shift = rows_per_block.bit_length() - 1
    sub_shift = SUBLANES.bit_length() - 1
    sq = (LANES, LANES)
    row = lax.broadcasted_iota(I32, sq, 0)
    row8 = lax.broadcasted_iota(I32, (SUBLANES, LANES), 0)

    def one_hots(e1, e2):
        return (row == e1).astype(F32), (row == e2).astype(F32)

    def count_step(i, cnt):
        base = pl.multiple_of(i * LANES, LANES)
        e1, e2, g1, g2 = _route_tokens_on_lanes(jnp.transpose(lg_ref[pl.ds(base, LANES), :]))
        e_ref[i] = jnp.where(row8 == 0, e1, jnp.where(row8 == 1, e2, 0))
        gate_out[pl.ds(base, LANES), :] = jnp.transpose(jnp.where(row == 0, g1, jnp.where(row == 1, g2, 0.0)))
        oh1, oh2 = one_hots(e1, e2)
        return cnt + jnp.sum(oh1 + oh2, axis=1, keepdims=True)

    unroll = ROUTE_UNROLL if n_chunks % ROUTE_UNROLL == 0 else 1
    counts_col = lax.fori_loop(0, n_chunks, count_step, jnp.zeros((LANES, 1), F32), unroll=unroll)
    counts = jnp.transpose(jnp.broadcast_to(counts_col, sq))[0:SUBLANES, :].astype(I32)
    padded = ((counts + (SUBLANES - 1)) >> sub_shift) << sub_shift
    pad_end = _lane_cumsum(padded)
    pad_start = pad_end - padded
    start_col = jnp.transpose(jnp.broadcast_to(pad_start[0:1, :].astype(F32), sq))[:, 0:1]

    tri = (row < lax.broadcasted_iota(I32, sq, 1)).astype(BF16)

    def dest_step(i, carry):
        er = e_ref[i]
        oh1, oh2 = one_hots(er[0:1, :], er[1:2, :])
        oh = oh1 + oh2
        pos = carry + _dot(oh.astype(BF16), tri)
        d0_out[i] = jnp.sum(oh1 * pos, axis=0, keepdims=True).astype(I32)
        d1_out[i] = jnp.sum(oh2 * pos, axis=0, keepdims=True).astype(I32)
        return carry + jnp.sum(oh, axis=1, keepdims=True)

    lax.fori_loop(0, n_chunks, dest_step, start_col, unroll=unroll)

    nbp = meta_out.shape[0]
    lane_b = lax.broadcasted_iota(I32, (nbp, LANES), 1)
    blk = lax.broadcasted_iota(I32, (nbp, LANES), 0)
    nblk = (counts + (rows_per_block - 1)) >> shift
    blk_end = _lane_cumsum(nblk)
    bcast = lambda v: jnp.broadcast_to(v[0:1, :], (nbp, LANES))
    be_end, be_start, cn, ps = bcast(blk_end), bcast(blk_end - nblk), bcast(counts), bcast(pad_start)
    is_e = lane_b < N_EXPERTS
    lsum = lambda v: jnp.sum(v, axis=-1, keepdims=True)
    n_used = jnp.max(be_end, axis=-1, keepdims=True)
    total = jnp.max(bcast(pad_end), axis=-1, keepdims=True)
    last_e = jnp.max(jnp.where(is_e & (cn > 0), lane_b, 0), axis=-1, keepdims=True)
    be = jnp.minimum(lsum(jnp.where(is_e & (be_end <= blk), 1, 0)), last_e)
    sel = lane_b == be
    first_row = (blk[:, 0:1] - lsum(jnp.where(sel, be_start, 0))) * rows_per_block
    used = blk[:, 0:1] < n_used
    nvalid = jnp.where(used, jnp.clip(lsum(jnp.where(sel, cn, 0)) - first_row, 0, rows_per_block), 0)
    row0 = jnp.where(used, lsum(jnp.where(sel, ps, 0)) + first_row, 0)
    cols = (be, nvalid, n_used, row0, total)
    meta = jnp.zeros((nbp, LANES), I32)
    for c, v in enumerate(cols):
        meta = jnp.where(lane_b == c, v, meta)
    meta_out[...] = meta


def _route_call(logits, *, rows_per_block, n_blocks):
    t = logits.shape[0]
    assert t % LANES == 0
    nbp = -(-n_blocks // SUBLANES) * SUBLANES
    dshape = (t // LANES, 1, LANES)
    d0, d1, gates, meta = pl.pallas_call(
        functools.partial(_route_kernel, rows_per_block=rows_per_block),
        in_specs=[_whole(logits.shape)],
        out_specs=[_whole(dshape), _whole(dshape), _whole((t, LANES)), _whole((nbp, LANES))],
        out_shape=[jax.ShapeDtypeStruct(dshape, I32), jax.ShapeDtypeStruct(dshape, I32),
                   jax.ShapeDtypeStruct((t, LANES), F32), jax.ShapeDtypeStruct((nbp, LANES), I32)],
        grid=(1,),
        scratch_shapes=[pltpu.VMEM((t // LANES, SUBLANES, LANES), I32)],
        compiler_params=pltpu.CompilerParams(dimension_semantics=("arbitrary",), vmem_limit_bytes=VMEM_LIMIT),
        name="route",
    )(logits)
    return d0.reshape(t), d1.reshape(t), gates, meta


def _pow2_chunks(limit):
    c = 1 << (limit.bit_length() - 1)
    while c >= 1:
        yield c
        c >>= 1


def _expert_kernel(be_ref, nv_ref, nused_ref, row0_ref, total_ref, d0_ref, d1_ref,
                   h_hbm, g2_ref, wg_hbm, wu_hbm, wd_hbm, y_hbm,
                   tok_ref, ord_ref, exp_ref, nexp_ref, xbuf, ybuf, zbuf, wg_buf, wu_buf, wd_buf,
                   gsem, ysem, zsem, wsem):
    n_used = nused_ref[0]
    groups = xbuf.shape[1]
    rows = groups * SUBLANES
    d = xbuf.shape[3]
    n_tok = d0_ref.shape[0]
    sub_shift = SUBLANES.bit_length() - 1
    n_slots = wg_buf.shape[0]

    def y_copies(blk, slot):
        ng = lax.shift_right_logical(nv_ref[blk] + (SUBLANES - 1), sub_shift)
        g0 = lax.shift_right_logical(row0_ref[blk], sub_shift)
        out = []
        for c in _pow2_chunks(groups):
            off = ng & ~(2 * c - 1)
            copy = pltpu.make_async_copy(ybuf.at[slot, pl.ds(off, c)], y_hbm.at[pl.ds(g0 + off, c)], ysem.at[slot])
            out.append(((ng & c) != 0, copy))
        return out

    def start_y(blk, slot):
        for cond, copy in y_copies(blk, slot):
            pl.when(cond)(copy.start)

    def wait_y(blk, slot):
        for cond, copy in y_copies(blk, slot):
            pl.when(cond)(copy.wait)

    def fill_tail():
        zbuf[...] = jnp.zeros(zbuf.shape, F32)
        first = lax.shift_right_logical(total_ref[0], sub_shift)
        copy = lambda g: pltpu.make_async_copy(zbuf, y_hbm.at[g], zsem.at[0])
        lax.fori_loop(first, y_hbm.shape[0], lambda g, c: (copy(g).start(), c)[1], 0)
        lax.fori_loop(first, y_hbm.shape[0], lambda g, c: (copy(g).wait(), c)[1], 0)

    def weight_copies(j):
        e = exp_ref[j]
        slot = lax.rem(j, n_slots)
        return [pltpu.make_async_copy(src.at[e], dst.at[slot], wsem.at[slot])
                for src, dst in ((wg_hbm, wg_buf), (wu_hbm, wu_buf), (wd_hbm, wd_buf))]

    def start_weights(j):
        for c in weight_copies(j):
            c.start(priority=1)

    def wait_weights(j):
        for c in weight_copies(j):
            c.wait()

    def gather_copy(blk, slot, g, u):
        tok = tok_ref[row0_ref[blk] + g * SUBLANES + u]
        return pltpu.make_async_copy(h_hbm.at[pl.ds(tok, 1)], xbuf.at[slot, g, pl.ds(u, 1)], gsem.at[slot])

    def start_gather(blk, slot):
        n = nv_ref[blk]

        def group(g, c):
            for u in range(SUBLANES):
                gather_copy(blk, slot, g, u).start(priority=u % 2)
            return c
        full = lax.shift_right_logical(n, sub_shift)
        lax.fori_loop(0, full, group, 0)
        for u in range(SUBLANES - 1):
            @pl.when(full * SUBLANES + u < n)
            def _():
                gather_copy(blk, slot, full, u).start(priority=u % 2)

    def wait_gather(blk, slot):
        n = nv_ref[blk]
        buf = xbuf.at[slot]
        for c in _pow2_chunks(rows):
            @pl.when((n & c) != 0)
            def _():
                if c >= SUBLANES:
                    part = buf.at[pl.ds(0, c // SUBLANES)]
                else:
                    part = buf.at[0, pl.ds(0, c)]
                pltpu.make_async_copy(part, part, gsem.at[slot]).wait()

    def scan(blk, j):
        e = be_ref[blk]
        is_new = jnp.logical_or(blk == 0, e != be_ref[jnp.maximum(blk - 1, 0)])
        j = j + is_new.astype(I32)
        ord_ref[blk] = j - 1

        @pl.when(is_new)
        def _():
            exp_ref[j - 1] = e
        return j
    n_exp = lax.fori_loop(0, n_used, scan, jnp.int32(0))
    nexp_ref[0] = n_exp
    for j in range(n_slots):
        @pl.when(j < n_exp)
        def _():
            start_weights(j)

    def inv(g, c):
        for u in range(DMA_UNROLL):
            tk = g * DMA_UNROLL + u
            tok_ref[d0_ref[tk]] = tk
            tok_ref[d1_ref[tk]] = tk
        return c
    lax.fori_loop(0, n_tok // DMA_UNROLL, inv, 0)
    xbuf[...] = jnp.zeros(xbuf.shape, F32)
    start_gather(0, 0)

    def block(b, carry):
        slot = b & 1
        wait_gather(b, slot)

        @pl.when(b + 1 < n_used)
        def _():
            start_gather(b + 1, 1 - slot)

        j = ord_ref[b]

        @pl.when(jnp.logical_or(b == 0, ord_ref[jnp.maximum(b - 1, 0)] != j))
        def _():
            wait_weights(j)

            @pl.when(jnp.logical_and(j >= 1, j + (n_slots - 1) < nexp_ref[0]))
            def _():
                start_weights(j + (n_slots - 1))

        ws = lax.rem(j, n_slots)
        hn = _rms(xbuf[slot].reshape(rows, d), g2_ref[...]).astype(BF16)
        gate = _dot(hn, wg_buf[ws].astype(BF16))
        up = _dot(hn, wu_buf[ws].astype(BF16))
        hmid = (gate * jax.nn.sigmoid(gate) * up).astype(BF16)
        y = _dot(hmid, wd_buf[ws].astype(BF16))

        @pl.when(b >= 2)
        def _():
            wait_y(b - 2, slot)

        ybuf[slot] = y.reshape(groups, SUBLANES, d)
        start_y(b, slot)
        return carry

    lax.fori_loop(0, n_used, block, 0)
    last = n_used - 1

    @pl.when(last >= 1)
    def _():
        wait_y(last - 1, 1 - (last & 1))
    wait_y(last, last & 1)
    fill_tail()


def _expert_call(be, nvalid, n_used, row0, total, dest0, dest1, h, g2, w_gate, w_up, w_down, *,
                 rows_per_block, n_blocks):
    t, d = h.shape
    n_exp, _, f = w_gate.shape
    assert t % DMA_UNROLL == 0 and rows_per_block % SUBLANES == 0 and (2 * t) % SUBLANES == 0
    groups = rows_per_block // SUBLANES
    sorted_groups = 2 * t // SUBLANES + n_exp
    hbm = pl.BlockSpec(memory_space=pl.ANY)
    slots = EXPERT_WEIGHT_SLOTS
    grid_spec = pltpu.PrefetchScalarGridSpec(
        num_scalar_prefetch=7,
        grid=(1,),
        in_specs=[hbm, pl.BlockSpec(g2.shape, lambda b, *_: (0, 0)), hbm, hbm, hbm],
        out_specs=hbm,
        scratch_shapes=[
            pltpu.SMEM((sorted_groups * SUBLANES,), I32),
            pltpu.SMEM((n_blocks,), I32), pltpu.SMEM((n_exp,), I32), pltpu.SMEM((1,), I32),
            pltpu.VMEM((2, groups, SUBLANES, d), F32), pltpu.VMEM((2, groups, SUBLANES, d), F32),
            pltpu.VMEM((SUBLANES, d), F32),
            pltpu.VMEM((slots, d, f), F32), pltpu.VMEM((slots, d, f), F32), pltpu.VMEM((slots, f, d), F32),
            pltpu.SemaphoreType.DMA((2,)), pltpu.SemaphoreType.DMA((2,)), pltpu.SemaphoreType.DMA((1,)),
            pltpu.SemaphoreType.DMA((slots,)),
        ],
    )
    y = pl.pallas_call(
        _expert_kernel,
        grid_spec=grid_spec,
        out_shape=jax.ShapeDtypeStruct((sorted_groups, SUBLANES, d), F32),
        compiler_params=pltpu.CompilerParams(dimension_semantics=("arbitrary",), vmem_limit_bytes=VMEM_LIMIT),
        name="experts",
    )(be, nvalid, n_used, row0, total, dest0, dest1, h, g2, w_gate, w_up, w_down)
    return y.reshape(sorted_groups * SUBLANES, d)


def _final_kernel(d0_ref, d1_ref, h_ref, y_hbm, gate_ref, fg_ref, o_ref, ybuf, sem):
    i = pl.program_id(0)
    n = pl.num_programs(0)
    n_slots = ybuf.shape[0]
    groups = ybuf.shape[2]
    tm = groups * SUBLANES

    def issue_group(tile, slot, g):
        for u in range(SUBLANES):
            tk = tile * tm + g * SUBLANES + u
            for k, dref in enumerate((d0_ref, d1_ref)):
                pltpu.make_async_copy(y_hbm.at[pl.ds(dref[tk], 1)], ybuf.at[slot, k, g, pl.ds(u, 1)],
                                      sem.at[slot]).start()

    def combine_group(slot, g, issue):
        r0 = pl.multiple_of(g * SUBLANES, SUBLANES)
        gt = gate_ref[pl.ds(r0, SUBLANES), :]
        out = h_ref[pl.ds(r0, SUBLANES), :] + gt[:, 0:1] * ybuf[slot, 0, g] + gt[:, 1:2] * ybuf[slot, 1, g]
        issue()
        o_ref[pl.ds(r0, SUBLANES), :] = _rms(out, fg_ref[...])

    @pl.when(i == 0)
    def _():
        for tile in range(n_slots - 1):
            @pl.when(tile < n)
            def _():
                lax.fori_loop(0, groups, lambda g, c: (issue_group(tile, tile, g), c)[1], 0)

    slot = lax.rem(i, n_slots)
    pltpu.make_async_copy(ybuf.at[slot], ybuf.at[slot], sem.at[slot]).wait()
    ahead = i + (n_slots - 1)
    unroll = FINAL_UNROLL if groups % FINAL_UNROLL == 0 else 1

    @pl.when(ahead < n)
    def _():
        nslot = lax.rem(ahead, n_slots)
        lax.fori_loop(0, groups, lambda g, c: (combine_group(
            slot, g, lambda: issue_group(ahead, nslot, g)), c)[1], 0, unroll=unroll)

    @pl.when(ahead >= n)
    def _():
        lax.fori_loop(0, groups, lambda g, c: (combine_group(slot, g, lambda: None), c)[1], 0, unroll=unroll)


def _final_call(dest0, dest1, h, y, gates, fg, *, tm):
    t, d = h.shape
    assert tm % SUBLANES == 0
    row = lambda i, *_: (i, 0)
    grid_spec = pltpu.PrefetchScalarGridSpec(
        num_scalar_prefetch=2,
        grid=(t // tm,),
        in_specs=[pl.BlockSpec((tm, d), row), pl.BlockSpec(memory_space=pl.ANY),
                  pl.BlockSpec((tm, LANES), row), pl.BlockSpec(fg.shape, lambda i, *_: (0, 0))],
        out_specs=pl.BlockSpec((tm, d), row),
        scratch_shapes=[pltpu.VMEM((FINAL_SLOTS, 2, tm // SUBLANES, SUBLANES, d), F32),
                        pltpu.SemaphoreType.DMA((FINAL_SLOTS,))],
    )
    return pl.pallas_call(
        _final_kernel,
        grid_spec=grid_spec,
        out_shape=jax.ShapeDtypeStruct((t, d), F32),
        compiler_params=pltpu.CompilerParams(dimension_semantics=("arbitrary",), vmem_limit_bytes=VMEM_LIMIT),
        name="final",
    )(dest0, dest1, h, y, gates, fg)


def _wsplit_kernel(lat_ref, u_ref, kr_ref, lat_out, u_out, kr_out, *, n_lat, n_kr):
    j = pl.program_id(0)
    u_out[...] = jnp.transpose(u_ref[...]).astype(BF16)

    @pl.when(j < n_lat)
    def _():
        lat_out[...] = jnp.transpose(lat_ref[...]).astype(BF16)

    @pl.when(j == 0)
    def _():
        kr = jnp.transpose(kr_ref[...])
        lane = lax.broadcasted_iota(I32, kr.shape, 1)
        kr_out[...] = jnp.where(lane < n_kr, kr, 0.0).astype(BF16)


def _wsplit_call(w_t, *, o_kr, o_u):
    cols, d = w_t.shape
    assert o_kr % LANES == 0 and (cols - o_u) % LANES == 0 and o_u % SUBLANES == 0
    blk = 2 * LANES if o_kr % (2 * LANES) == 0 and (cols - o_u) % (2 * LANES) == 0 else LANES
    n_u = (cols - o_u) // blk
    n_lat = o_kr // blk
    assert 0 < o_u - o_kr <= LANES and n_lat <= n_u and o_kr + LANES <= cols
    lat_blk = lambda j: jnp.minimum(j, n_lat - 1)
    return pl.pallas_call(
        functools.partial(_wsplit_kernel, n_lat=n_lat, n_kr=o_u - o_kr),
        grid=(n_u,),
        in_specs=[pl.BlockSpec((blk, d), lambda j: (lat_blk(j), 0)),
                  pl.BlockSpec((pl.Element(blk), pl.Element(d)),
                               lambda j: (pl.multiple_of(o_u + j * blk, SUBLANES), 0)),
                  pl.BlockSpec((pl.Element(LANES), pl.Element(d)), lambda j: (o_kr, 0))],
        out_specs=[pl.BlockSpec((d, blk), lambda j: (0, lat_blk(j))), pl.BlockSpec((d, blk), lambda j: (0, j)),
                   pl.BlockSpec((d, LANES), lambda j: (0, 0))],
        out_shape=[jax.ShapeDtypeStruct((d, o_kr), BF16), jax.ShapeDtypeStruct((d, cols - o_u), BF16),
                   jax.ShapeDtypeStruct((d, LANES), BF16)],
        compiler_params=pltpu.CompilerParams(dimension_semantics=("arbitrary",), vmem_limit_bytes=VMEM_LIMIT),
        name="wsplit",
    )(w_t, w_t, w_t)


def _rope_tables(seq):
    pos = np.arange(seq, dtype=np.float64)
    inv_freq = ROPE_THETA ** (-np.arange(0, QK_ROPE_DIM, 2, dtype=np.float64) / QK_ROPE_DIM)
    ang = pos[:, None] * inv_freq[None, :]
    cos, sin = np.cos(ang).astype(np.float32), np.sin(ang).astype(np.float32)
    zero = np.zeros_like(sin)
    cos_t = np.concatenate([cos, cos, cos, cos], axis=1)
    s1_t = np.concatenate([zero, sin, zero, sin], axis=1)
    s2_t = np.concatenate([-sin, zero, -sin, zero], axis=1)
    return jnp.asarray(cos_t), jnp.asarray(s1_t), jnp.asarray(s2_t)


def kernel(x, ln1_g, w_in, b_glu, q_norm_g, w_uq, kv_norm_g, w_ukv, w_dw, b_dw, conv_ln_g, conv_ln_b,
           w_o, ln2_g, w_group, b_group, w_router, b_router, w_gate, w_up, w_down, final_g):
    batch, seq, d = x.shape
    assert ln1_g.shape[0] == 1, "single-layer trunk"
    t = batch * seq
    q_rank = q_norm_g.shape[1]
    kv_rank = kv_norm_g.shape[1]
    x2 = x.reshape(t, d)

    wi = w_in[0]
    o_kr = q_rank + kv_rank
    o_u = o_kr + QK_ROPE_DIM
    wlat, wu, wkr = _wsplit_call(jnp.transpose(wi), o_kr=o_kr, o_u=o_u)
    wuq = w_uq[0].reshape(q_rank, MLA_HEADS, QK_NOPE_DIM + QK_ROPE_DIM)
    wuq = jnp.concatenate([wuq[:, :, :QK_NOPE_DIM].reshape(q_rank, MLA_HEADS * QK_NOPE_DIM),
                           wuq[:, :, QK_NOPE_DIM:].reshape(q_rank, MLA_HEADS * QK_ROPE_DIM)], axis=1).astype(BF16)
    wukv = w_ukv[0]
    wo = w_o[0]
    wr = jnp.concatenate([w_router[0], w_group[0],
                          jnp.zeros((d, LANES - N_EXPERTS - N_EXPERT_GROUPS), F32)], axis=1)
    wr_hi = wr.astype(BF16)
    wr_lo = (wr - wr_hi.astype(F32)).astype(BF16)
    wr2 = jnp.concatenate([wr_hi, wr_lo], axis=1)
    br = jnp.concatenate([b_router[0], b_group[0],
                          jnp.zeros((LANES - N_EXPERTS - N_EXPERT_GROUPS,), F32)])[None, :]
    cos_t, s1_t, s2_t = _rope_tables(seq)

    tm = min(512, seq)
    q, k, v, c = _proj_call(x2, ln1_g, wlat, wu, wkr, b_glu, q_norm_g, kv_norm_g, wuq, wukv, cos_t, s1_t, s2_t,
                            seq=seq, tm=tm)
    attn, conv = _attn_conv_call(q, k, v, c, w_dw[0], b_dw, batch=batch, seq=seq, tq=min(256, seq))
    h, logits = _oproj_call(attn, conv, x2, wo, conv_ln_g, conv_ln_b, ln2_g, wr2, br, tm=tm)

    n_blocks = -(-(2 * t + N_EXPERTS * (MOE_ROWS - 1)) // MOE_ROWS)
    dest0, dest1, gates, meta = _route_call(logits, rows_per_block=MOE_ROWS, n_blocks=n_blocks)
    y = _expert_call(meta[:n_blocks, 0], meta[:n_blocks, 1], meta[0:1, 2], meta[:n_blocks, 3], meta[0:1, 4],
                     dest0, dest1, h, ln2_g, w_gate[0], w_up[0], w_down[0],
                     rows_per_block=MOE_ROWS, n_blocks=n_blocks)
    out = _final_call(dest0, dest1, h, y, gates, final_g[None, :], tm=tm)
    return out.reshape(batch, seq, d)
```

```python
import functools

import jax
import jax.numpy as jnp
import numpy as np
from jax import lax
from jax.experimental import pallas as pl
from jax.experimental.pallas import tpu as pltpu

F32 = jnp.float32
BF16 = jnp.bfloat16
I32 = jnp.int32

MLA_HEADS = 8
QK_NOPE_DIM = 128
QK_ROPE_DIM = 64
V_HEAD_DIM = 128
ROPE_THETA = 10000.0
N_EXPERT_GROUPS = 8
EXPERTS_PER_GROUP = 8
N_EXPERTS = N_EXPERT_GROUPS * EXPERTS_PER_GROUP
EPS = 1e-6
LOG2E = 1.4426950408889634

LANES = 128
SUBLANES = 8
HEAD_SLOT = 2 * LANES
ROPE_HALF = QK_ROPE_DIM // 2
VMEM_LIMIT = 56 * 1024 * 1024

MOE_ROWS = 256
FINAL_SLOTS = 3
FINAL_UNROLL = 8
ROUTE_UNROLL = 4
EXPERT_ROW_SLOTS = 3
EXPERT_WEIGHT_SLOTS = 3
DMA_UNROLL = 8


def _rms(x, g):
    return x * lax.rsqrt(jnp.mean(x * x, axis=-1, keepdims=True) + EPS) * g


def _dot(a, b):
    return jnp.dot(a, b, preferred_element_type=F32)


def _whole(shape, single=False):
    mode = dict(pipeline_mode=pl.Buffered(1)) if single else {}
    return pl.BlockSpec(shape, lambda *_: (0,) * len(shape), **mode)


def _proj_kernel(x_ref, g1_ref, wlat_ref, wu_ref, wkr_ref, bglu_ref, qg_ref, kvg_ref, wuq_ref, wukv_ref,
                 cos_ref, s1_ref, s2_ref, q_out, k_out, v_out, c_out, *, q_rank, kv_rank, conv_ch, q_scale):
    xn = _rms(x_ref[...], g1_ref[...]).astype(BF16)
    cos = cos_ref[...]
    s1 = s1_ref[...]
    s2 = s2_ref[...]

    def rope(t):
        return t * cos + pltpu.roll(t, ROPE_HALF, 1) * s1 + pltpu.roll(t, LANES - ROPE_HALF, 1) * s2

    a = _dot(xn, wu_ref[:, :conv_ch]) + bglu_ref[:, :conv_ch]
    gate = _dot(xn, wu_ref[:, conv_ch:]) + bglu_ref[:, conv_ch:]
    c_out[...] = (a * jax.nn.sigmoid(gate)).astype(BF16)

    kr = rope(_dot(xn, wkr_ref[...])).astype(BF16)
    qn = _rms(_dot(xn, wlat_ref[:, :q_rank]), qg_ref[...]).astype(BF16)
    kvn = _rms(_dot(xn, wlat_ref[:, q_rank:q_rank + kv_rank]), kvg_ref[...]).astype(BF16)
    lane = lax.broadcasted_iota(I32, (x_ref.shape[0], LANES), 1)
    ones_blk = (lane == 0).astype(BF16)
    n_nope = MLA_HEADS * QK_NOPE_DIM
    q_nope = _dot(qn, wuq_ref[:, :n_nope])
    q_rope = _dot(qn, wuq_ref[:, n_nope:])
    for h in range(MLA_HEADS):
        c0 = h * HEAD_SLOT
        q_out[:, c0:c0 + LANES] = (q_nope[:, h * QK_NOPE_DIM:(h + 1) * QK_NOPE_DIM] * q_scale).astype(BF16)
        if h % 2 == 0:
            pair = rope(q_rope[:, (h // 2) * LANES:(h // 2 + 1) * LANES]) * q_scale
        half_pair = pair if h % 2 == 0 else pltpu.roll(pair, LANES - QK_ROPE_DIM, 1)
        q_out[:, c0 + LANES:c0 + HEAD_SLOT] = jnp.where(lane < QK_ROPE_DIM, half_pair, 0.0).astype(BF16)
        kvh = _dot(kvn, wukv_ref[:, c0:c0 + HEAD_SLOT].astype(BF16))
        k_out[:, c0:c0 + LANES] = kvh[:, :LANES].astype(BF16)
        k_out[:, c0 + LANES:c0 + HEAD_SLOT] = kr
        v_out[:, c0:c0 + LANES] = kvh[:, LANES:].astype(BF16)
        v_out[:, c0 + LANES:c0 + HEAD_SLOT] = ones_blk


def _proj_call(x2, g1, wlat, wu, wkr, bglu, qg, kvg, wuq, wukv, cos_t, s1_t, s2_t, *, seq, tm):
    t, d = x2.shape
    q_rank, kv_rank = qg.shape[1], kvg.shape[1]
    conv_ch = bglu.shape[1] // 2
    n_pos = seq // tm
    row = lambda i: (i, 0)
    pos = lambda i: (i % n_pos, 0)
    q_scale = float(QK_NOPE_DIM + QK_ROPE_DIM) ** -0.5 * LOG2E
    kern = functools.partial(_proj_kernel, q_rank=q_rank, kv_rank=kv_rank, conv_ch=conv_ch, q_scale=q_scale)
    slot_w = MLA_HEADS * HEAD_SLOT
    return pl.pallas_call(
        kern,
        grid=(t // tm,),
        in_specs=[
            pl.BlockSpec((tm, d), row), _whole(g1.shape), _whole(wlat.shape, True), _whole(wu.shape, True),
            _whole(wkr.shape, True), _whole(bglu.shape),
            _whole(qg.shape), _whole(kvg.shape), _whole(wuq.shape, True), _whole(wukv.shape, True),
            pl.BlockSpec((tm, LANES), pos), pl.BlockSpec((tm, LANES), pos), pl.BlockSpec((tm, LANES), pos),
        ],
        out_specs=[
            pl.BlockSpec((tm, slot_w), row), pl.BlockSpec((tm, slot_w), row),
            pl.BlockSpec((tm, slot_w), row), pl.BlockSpec((tm, conv_ch), row),
        ],
        out_shape=[
            jax.ShapeDtypeStruct((t, slot_w), BF16), jax.ShapeDtypeStruct((t, slot_w), BF16),
            jax.ShapeDtypeStruct((t, slot_w), BF16), jax.ShapeDtypeStruct((t, conv_ch), BF16),
        ],
        compiler_params=pltpu.CompilerParams(dimension_semantics=("arbitrary",), vmem_limit_bytes=VMEM_LIMIT),
        name="proj",
    )(x2, g1, wlat, wu, wkr, bglu, qg, kvg, wuq, wukv, cos_t, s1_t, s2_t)


CONV_PAD = 16
CONV_ROWS = 128


def _attn_conv_kernel(q_ref, k_ref, v_ref, c_ref, w_ref, b_ref, o_ref, y_ref, xp_ref, *, tq, width):
    s_len = q_ref.shape[0]
    half = width // 2
    rows = min(CONV_ROWS, s_len)
    win = rows + 2 * CONV_PAD
    zeros = jnp.zeros((CONV_PAD, LANES), F32)
    xp_ref[0:CONV_PAD, :] = zeros
    xp_ref[CONV_PAD + s_len:, :] = zeros
    xp_ref[CONV_PAD:CONV_PAD + s_len, :] = c_ref[...].astype(F32)

    def conv_chunk(ci):
        base = ci * rows
        xw = xp_ref[base:base + win, :]
        acc = jnp.zeros((rows, LANES), F32)
        for r in range(SUBLANES):
            shifted = xw if r == 0 else pltpu.roll(xw, win - r, 0)
            for a0 in range(0, 2 * CONV_PAD, SUBLANES):
                k = a0 + r - (CONV_PAD - half)
                if 0 <= k < width:
                    acc = acc + shifted[a0:a0 + rows, :] * w_ref[k:k + 1, :]
        y_ref[base:base + rows, :] = (acc + b_ref[...]).astype(BF16)

    k = k_ref[...]
    v = v_ref[...]
    n_q = s_len // tq
    n_chunks = s_len // rows
    for j in range(n_q):
        qs = slice(j * tq, (j + 1) * tq)
        s = lax.dot_general(q_ref[qs, :], k, (((1,), (1,)), ((), ())), preferred_element_type=F32)
        m = jnp.max(s, axis=-1, keepdims=True)
        p = jnp.exp2(s - m).astype(BF16)
        o = _dot(p, v)
        o_ref[qs, :] = (o[:, :V_HEAD_DIM] / o[:, V_HEAD_DIM:V_HEAD_DIM + 1]).astype(BF16)
        for ci in range(j * n_chunks // n_q, (j + 1) * n_chunks // n_q):
            conv_chunk(ci)


def _attn_conv_call(q, k, v, c, w_dw, b_dw, *, batch, seq, tq):
    t, ch = c.shape
    width = w_dw.shape[0]
    assert width // 2 <= CONV_PAD and ch == MLA_HEADS * LANES
    head = lambda b, h: (b, h)
    chan = lambda b, h: (0, h)
    return pl.pallas_call(
        functools.partial(_attn_conv_kernel, tq=tq, width=width),
        grid=(batch, MLA_HEADS),
        in_specs=[pl.BlockSpec((seq, HEAD_SLOT), head), pl.BlockSpec((seq, HEAD_SLOT), head),
                  pl.BlockSpec((seq, HEAD_SLOT), head), pl.BlockSpec((seq, LANES), head),
                  pl.BlockSpec((width, LANES), chan), pl.BlockSpec((1, LANES), chan)],
        out_specs=[pl.BlockSpec((seq, V_HEAD_DIM), head), pl.BlockSpec((seq, LANES), head)],
        out_shape=[jax.ShapeDtypeStruct((t, MLA_HEADS * V_HEAD_DIM), BF16), jax.ShapeDtypeStruct((t, ch), BF16)],
        scratch_shapes=[pltpu.VMEM((seq + 2 * CONV_PAD, LANES), F32)],
        compiler_params=pltpu.CompilerParams(
            dimension_semantics=("arbitrary", "arbitrary"), vmem_limit_bytes=VMEM_LIMIT),
        name="attn_conv",
    )(q, k, v, c, w_dw, b_dw)


def _oproj_kernel(a_ref, c_ref, x_ref, wo_ref, lg_ref, lb_ref, g2_ref, wr_ref, br_ref, h_out, lg_out):
    na = a_ref.shape[1]
    y = c_ref[...].astype(F32)
    yc = y - jnp.mean(y, axis=-1, keepdims=True)
    z = yc * lax.rsqrt(jnp.mean(yc * yc, axis=-1, keepdims=True) + EPS) * lg_ref[...] + lb_ref[...]
    act = (z * jax.nn.sigmoid(z)).astype(BF16)
    h = (x_ref[...] + _dot(a_ref[...], wo_ref[:na, :].astype(BF16))
         + _dot(act, wo_ref[na:, :].astype(BF16)))
    h_out[...] = h
    hn = _rms(h, g2_ref[...])
    hi = hn.astype(BF16)
    lo = (hn - hi.astype(F32)).astype(BF16)
    r = _dot(hi, wr_ref[...])
    lg_out[...] = r[:, :LANES] + r[:, LANES:] + _dot(lo, wr_ref[:, :LANES]) + br_ref[...]


def _oproj_call(attn, conv, x2, wo, ln_g, ln_b, g2, wr, br, *, tm):
    t, d = x2.shape
    row = lambda i: (i, 0)
    return pl.pallas_call(
        _oproj_kernel,
        grid=(t // tm,),
        in_specs=[pl.BlockSpec((tm, attn.shape[1]), row), pl.BlockSpec((tm, conv.shape[1]), row),
                  pl.BlockSpec((tm, d), row), _whole(wo.shape, True), _whole(ln_g.shape), _whole(ln_b.shape),
                  _whole(g2.shape), _whole(wr.shape, True), _whole(br.shape)],
        out_specs=[pl.BlockSpec((tm, d), row), pl.BlockSpec((tm, LANES), row)],
        out_shape=[jax.ShapeDtypeStruct((t, d), F32), jax.ShapeDtypeStruct((t, LANES), F32)],
        compiler_params=pltpu.CompilerParams(dimension_semantics=("arbitrary",), vmem_limit_bytes=VMEM_LIMIT),
        name="oproj",
    )(attn, conv, x2, wo, ln_g, ln_b, g2, wr, br)


def _route_tokens_on_lanes(lt):
    shape = (SUBLANES, LANES)
    row = lax.broadcasted_iota(I32, shape, 0)
    big = jnp.int32(1 << 20)
    neg = jnp.float32(-jnp.inf)

    def top(v):
        m = jnp.max(v, axis=0, keepdims=True)
        return m, jnp.min(jnp.where(v == m, row, big), axis=0, keepdims=True)

    lgrp = lt[N_EXPERTS:N_EXPERTS + N_EXPERT_GROUPS, :]
    gmax, gsel = top(lgrp)
    p_g = 1.0 / jnp.sum(jnp.exp(lgrp - gmax), axis=0, keepdims=True)
    le = jnp.zeros(shape, F32)
    for g in range(N_EXPERT_GROUPS):
        le = jnp.where(gsel == g, lt[g * EXPERTS_PER_GROUP:(g + 1) * EXPERTS_PER_GROUP, :], le)
    m1, i1 = top(le)
    m2, i2 = top(jnp.where(row == i1, neg, le))
    r = jnp.exp(m2 - m1)
    w1 = 1.0 / (1.0 + r)
    w2 = r / (1.0 + r)
    base = gsel * EXPERTS_PER_GROUP
    return base + i1, base + i2, p_g * w1, p_g * w2


def _lane_cumsum(v):
    lane = lax.broadcasted_iota(I32, v.shape, 1)
    sh = 1
    while sh < LANES:
        v = v + jnp.where(lane >= sh, pltpu.roll(v, sh, 1), 0)
        sh *= 2
    return v


def _route_kernel(lg_ref, d0_out, d1_out, gate_out, meta_out, e_ref, *, rows_per_block):
    n_chunks = d0_out.shape[0]
    shift = rows_per_block.bit_length() - 1
    sub_shift = SUBLANES.bit_length() - 1
    sq = (LANES, LANES)
    row = lax.broadcasted_iota(I32, sq, 0)
    row8 = lax.broadcasted_iota(I32, (SUBLANES, LANES), 0)

    def one_hots(e1, e2):
        return (row == e1).astype(F32), (row == e2).astype(F32)

    def count_step(i, cnt):
        base = pl.multiple_of(i * LANES, LANES)
        e1, e2, g1, g2 = _route_tokens_on_lanes(jnp.transpose(lg_ref[pl.ds(base, LANES), :]))
        e_ref[i] = jnp.where(row8 == 0, e1, jnp.where(row8 == 1, e2, 0))
        gate_out[pl.ds(base, LANES), :] = jnp.transpose(jnp.where(row == 0, g1, jnp.where(row == 1, g2, 0.0)))
        oh1, oh2 = one_hots(e1, e2)
        return cnt + jnp.sum(oh1 + oh2, axis=1, keepdims=True)

    unroll = ROUTE_UNROLL if n_chunks % ROUTE_UNROLL == 0 else 1
    counts_col = lax.fori_loop(0, n_chunks, count_step, jnp.zeros((LANES, 1), F32), unroll=unroll)
    counts = jnp.transpose(jnp.broadcast_to(counts_col, sq))[0:SUBLANES, :].astype(I32)
    padded = ((counts + (SUBLANES - 1)) >> sub_shift) << sub_shift
    pad_end = _lane_cumsum(padded)
    pad_start = pad_end - padded
    start_col = jnp.transpose(jnp.broadcast_to(pad_start[0:1, :].astype(F32), sq))[:, 0:1]

    tri = (row < lax.broadcasted_iota(I32, sq, 1)).astype(BF16)

    def dest_step(i, carry):
        er = e_ref[i]
        oh1, oh2 = one_hots(er[0:1, :], er[1:2, :])
        oh = oh1 + oh2
        pos = carry + _dot(oh.astype(BF16), tri)
        d0_out[i] = jnp.sum(oh1 * pos, axis=0, keepdims=True).astype(I32)
        d1_out[i] = jnp.sum(oh2 * pos, axis=0, keepdims=True).astype(I32)
        return carry + jnp.sum(oh, axis=1, keepdims=True)

    lax.fori_loop(0, n_chunks, dest_step, start_col, unroll=unroll)

    nbp = meta_out.shape[0]
    lane_b = lax.broadcasted_iota(I32, (nbp, LANES), 1)
    blk = lax.broadcasted_iota(I32, (nbp, LANES), 0)
    nblk = (counts + (rows_per_block - 1)) >> shift
    blk_end = _lane_cumsum(nblk)
    bcast = lambda v: jnp.broadcast_to(v[0:1, :], (nbp, LANES))
    be_end, be_start, cn, ps = bcast(blk_end), bcast(blk_end - nblk), bcast(counts), bcast(pad_start)
    is_e = lane_b < N_EXPERTS
    lsum = lambda v: jnp.sum(v, axis=-1, keepdims=True)
    n_used = jnp.max(be_end, axis=-1, keepdims=True)
    total = jnp.max(bcast(pad_end), axis=-1, keepdims=True)
    last_e = jnp.max(jnp.where(is_e & (cn > 0), lane_b, 0), axis=-1, keepdims=True)
    be = jnp.minimum(lsum(jnp.where(is_e & (be_end <= blk), 1, 0)), last_e)
    sel = lane_b == be
    first_row = (blk[:, 0:1] - lsum(jnp.where(sel, be_start, 0))) * rows_per_block
    used = blk[:, 0:1] < n_used
    nvalid = jnp.where(used, jnp.clip(lsum(jnp.where(sel, cn, 0)) - first_row, 0, rows_per_block), 0)
    row0 = jnp.where(used, lsum(jnp.where(sel, ps, 0)) + first_row, 0)
    cols = (be, nvalid, n_used, row0, total)
    meta = jnp.zeros((nbp, LANES), I32)
    for c, v in enumerate(cols):
        meta = jnp.where(lane_b == c, v, meta)
    meta_out[...] = meta


def _route_call(logits, *, rows_per_block, n_blocks):
    t = logits.shape[0]
    assert t % LANES == 0
    nbp = -(-n_blocks // SUBLANES) * SUBLANES
    dshape = (t // LANES, 1, LANES)
    d0, d1, gates, meta = pl.pallas_call(
        functools.partial(_route_kernel, rows_per_block=rows_per_block),
        in_specs=[_whole(logits.shape)],
        out_specs=[_whole(dshape), _whole(dshape), _whole((t, LANES)), _whole((nbp, LANES))],
        out_shape=[jax.ShapeDtypeStruct(dshape, I32), jax.ShapeDtypeStruct(dshape, I32),
                   jax.ShapeDtypeStruct((t, LANES), F32), jax.ShapeDtypeStruct((nbp, LANES), I32)],
        grid=(1,),
        scratch_shapes=[pltpu.VMEM((t // LANES, SUBLANES, LANES), I32)],
        compiler_params=pltpu.CompilerParams(dimension_semantics=("arbitrary",), vmem_limit_bytes=VMEM_LIMIT),
        name="route",
    )(logits)
    return d0.reshape(t), d1.reshape(t), gates, meta


def _pow2_chunks(limit):
    c = 1 << (limit.bit_length() - 1)
    while c >= 1:
        yield c
        c >>= 1


def _expert_kernel(be_ref, nv_ref, nused_ref, row0_ref, total_ref, d0_ref, d1_ref,
                   h_hbm, g2_ref, wg_hbm, wu_hbm, wd_hbm, y_hbm,
                   tok_ref, ord_ref, exp_ref, nexp_ref, xbuf, ybuf, zbuf, wg_buf, wu_buf, wd_buf,
                   gsem, ysem, zsem, wsem):
    n_used = nused_ref[0]
    groups = xbuf.shape[1]
    rows = groups * SUBLANES
    d = xbuf.shape[3]
    n_tok = d0_ref.shape[0]
    sub_shift = SUBLANES.bit_length() - 1
    n_slots = wg_buf.shape[0]

    def y_copies(blk, slot):
        ng = lax.shift_right_logical(nv_ref[blk] + (SUBLANES - 1), sub_shift)
        g0 = lax.shift_right_logical(row0_ref[blk], sub_shift)
        out = []
        for c in _pow2_chunks(groups):
            off = ng & ~(2 * c - 1)
            copy = pltpu.make_async_copy(ybuf.at[slot, pl.ds(off, c)], y_hbm.at[pl.ds(g0 + off, c)], ysem.at[slot])
            out.append(((ng & c) != 0, copy))
        return out

    def start_y(blk, slot):
        for cond, copy in y_copies(blk, slot):
            pl.when(cond)(copy.start)

    def wait_y(blk, slot):
        for cond, copy in y_copies(blk, slot):
            pl.when(cond)(copy.wait)

    def fill_tail():
        zbuf[...] = jnp.zeros(zbuf.shape, F32)
        first = lax.shift_right_logical(total_ref[0], sub_shift)
        copy = lambda g: pltpu.make_async_copy(zbuf, y_hbm.at[g], zsem.at[0])
        lax.fori_loop(first, y_hbm.shape[0], lambda g, c: (copy(g).start(), c)[1], 0)
        lax.fori_loop(first, y_hbm.shape[0], lambda g, c: (copy(g).wait(), c)[1], 0)

    def weight_copies(j):
        e = exp_ref[j]
        slot = lax.rem(j, n_slots)
        return [pltpu.make_async_copy(src.at[e], dst.at[slot], wsem.at[slot])
                for src, dst in ((wg_hbm, wg_buf), (wu_hbm, wu_buf), (wd_hbm, wd_buf))]

    def start_weights(j):
        for c in weight_copies(j):
            c.start(priority=1)

    def wait_weights(j):
        for c in weight_copies(j):
            c.wait()

    def gather_copy(blk, slot, g, u):
        tok = tok_ref[row0_ref[blk] + g * SUBLANES + u]
        return pltpu.make_async_copy(h_hbm.at[pl.ds(tok, 1)], xbuf.at[slot, g, pl.ds(u, 1)], gsem.at[slot])

    def start_gather(blk, slot):
        n = nv_ref[blk]

        def group(g, c):
            for u in range(SUBLANES):
                gather_copy(blk, slot, g, u).start()
            return c
        full = lax.shift_right_logical(n, sub_shift)
        lax.fori_loop(0, full, group, 0)
        for u in range(SUBLANES - 1):
            @pl.when(full * SUBLANES + u < n)
            def _():
                gather_copy(blk, slot, full, u).start()

    def wait_gather(blk, slot):
        n = nv_ref[blk]
        buf = xbuf.at[slot]
        for c in _pow2_chunks(rows):
            @pl.when((n & c) != 0)
            def _():
                if c >= SUBLANES:
                    part = buf.at[pl.ds(0, c // SUBLANES)]
                else:
                    part = buf.at[0, pl.ds(0, c)]
                pltpu.make_async_copy(part, part, gsem.at[slot]).wait()

    def scan(blk, j):
        e = be_ref[blk]
        is_new = jnp.logical_or(blk == 0, e != be_ref[jnp.maximum(blk - 1, 0)])
        j = j + is_new.astype(I32)
        ord_ref[blk] = j - 1

        @pl.when(is_new)
        def _():
            exp_ref[j - 1] = e
        return j
    n_exp = lax.fori_loop(0, n_used, scan, jnp.int32(0))
    nexp_ref[0] = n_exp
    for j in range(n_slots):
        @pl.when(j < n_exp)
        def _():
            start_weights(j)

    def inv(g, c):
        for u in range(DMA_UNROLL):
            tk = g * DMA_UNROLL + u
            tok_ref[d0_ref[tk]] = tk
            tok_ref[d1_ref[tk]] = tk
        return c
    lax.fori_loop(0, n_tok // DMA_UNROLL, inv, 0)
    xbuf[...] = jnp.zeros(xbuf.shape, F32)
    x_slots = xbuf.shape[0]
    for blk in range(x_slots - 1):
        @pl.when(blk < n_used)
        def _():
            start_gather(blk, blk)

    def block(b, carry):
        slot = b & 1
        xs = lax.rem(b, x_slots)
        wait_gather(b, xs)
        ahead = b + (x_slots - 1)

        @pl.when(ahead < n_used)
        def _():
            start_gather(ahead, lax.rem(ahead, x_slots))

        j = ord_ref[b]

        @pl.when(jnp.logical_or(b == 0, ord_ref[jnp.maximum(b - 1, 0)] != j))
        def _():
            wait_weights(j)

            @pl.when(jnp.logical_and(j >= 1, j + (n_slots - 1) < nexp_ref[0]))
            def _():
                start_weights(j + (n_slots - 1))

        ws = lax.rem(j, n_slots)
        hn = _rms(xbuf[xs].reshape(rows, d), g2_ref[...]).astype(BF16)
        gate = _dot(hn, wg_buf[ws].astype(BF16))
        up = _dot(hn, wu_buf[ws].astype(BF16))
        hmid = (gate * jax.nn.sigmoid(gate) * up).astype(BF16)
        y = _dot(hmid, wd_buf[ws].astype(BF16))

        @pl.when(b >= 2)
        def _():
            wait_y(b - 2, slot)

        ybuf[slot] = y.reshape(groups, SUBLANES, d)
        start_y(b, slot)
        return carry

    lax.fori_loop(0, n_used, block, 0)
    last = n_used - 1

    @pl.when(last >= 1)
    def _():
        wait_y(last - 1, 1 - (last & 1))
    wait_y(last, last & 1)
    fill_tail()


def _expert_call(be, nvalid, n_used, row0, total, dest0, dest1, h, g2, w_gate, w_up, w_down, *,
                 rows_per_block, n_blocks):
    t, d = h.shape
    n_exp, _, f = w_gate.shape
    assert t % DMA_UNROLL == 0 and rows_per_block % SUBLANES == 0 and (2 * t) % SUBLANES == 0
    groups = rows_per_block // SUBLANES
    sorted_groups = 2 * t // SUBLANES + n_exp
    hbm = pl.BlockSpec(memory_space=pl.ANY)
    slots = EXPERT_WEIGHT_SLOTS
    grid_spec = pltpu.PrefetchScalarGridSpec(
        num_scalar_prefetch=7,
        grid=(1,),
        in_specs=[hbm, pl.BlockSpec(g2.shape, lambda b, *_: (0, 0)), hbm, hbm, hbm],
        out_specs=hbm,
        scratch_shapes=[
            pltpu.SMEM((sorted_groups * SUBLANES,), I32),
            pltpu.SMEM((n_blocks,), I32), pltpu.SMEM((n_exp,), I32), pltpu.SMEM((1,), I32),
            pltpu.VMEM((EXPERT_ROW_SLOTS, groups, SUBLANES, d), F32), pltpu.VMEM((2, groups, SUBLANES, d), F32),
            pltpu.VMEM((SUBLANES, d), F32),
            pltpu.VMEM((slots, d, f), F32), pltpu.VMEM((slots, d, f), F32), pltpu.VMEM((slots, f, d), F32),
            pltpu.SemaphoreType.DMA((EXPERT_ROW_SLOTS,)), pltpu.SemaphoreType.DMA((2,)), pltpu.SemaphoreType.DMA((1,)),
            pltpu.SemaphoreType.DMA((slots,)),
        ],
    )
    y = pl.pallas_call(
        _expert_kernel,
        grid_spec=grid_spec,
        out_shape=jax.ShapeDtypeStruct((sorted_groups, SUBLANES, d), F32),
        compiler_params=pltpu.CompilerParams(dimension_semantics=("arbitrary",), vmem_limit_bytes=VMEM_LIMIT),
        name="experts",
    )(be, nvalid, n_used, row0, total, dest0, dest1, h, g2, w_gate, w_up, w_down)
    return y.reshape(sorted_groups * SUBLANES, d)


def _final_kernel(d0_ref, d1_ref, h_ref, y_hbm, gate_ref, fg_ref, o_ref, ybuf, sem):
    i = pl.program_id(0)
    n = pl.num_programs(0)
    n_slots = ybuf.shape[0]
    groups = ybuf.shape[2]
    tm = groups * SUBLANES

    def issue_group(tile, slot, g):
        for u in range(SUBLANES):
            tk = tile * tm + g * SUBLANES + u
            for k, dref in enumerate((d0_ref, d1_ref)):
                pltpu.make_async_copy(y_hbm.at[pl.ds(dref[tk], 1)], ybuf.at[slot, k, g, pl.ds(u, 1)],
                                      sem.at[slot]).start()

    def combine_group(slot, g, issue):
        r0 = pl.multiple_of(g * SUBLANES, SUBLANES)
        gt = gate_ref[pl.ds(r0, SUBLANES), :]
        out = h_ref[pl.ds(r0, SUBLANES), :] + gt[:, 0:1] * ybuf[slot, 0, g] + gt[:, 1:2] * ybuf[slot, 1, g]
        issue()
        o_ref[pl.ds(r0, SUBLANES), :] = _rms(out, fg_ref[...])

    @pl.when(i == 0)
    def _():
        for tile in range(n_slots - 1):
            @pl.when(tile < n)
            def _():
                lax.fori_loop(0, groups, lambda g, c: (issue_group(tile, tile, g), c)[1], 0)

    slot = lax.rem(i, n_slots)
    pltpu.make_async_copy(ybuf.at[slot], ybuf.at[slot], sem.at[slot]).wait()
    ahead = i + (n_slots - 1)
    unroll = FINAL_UNROLL if groups % FINAL_UNROLL == 0 else 1

    @pl.when(ahead < n)
    def _():
        nslot = lax.rem(ahead, n_slots)
        lax.fori_loop(0, groups, lambda g, c: (combine_group(
            slot, g, lambda: issue_group(ahead, nslot, g)), c)[1], 0, unroll=unroll)

    @pl.when(ahead >= n)
    def _():
        lax.fori_loop(0, groups, lambda g, c: (combine_group(slot, g, lambda: None), c)[1], 0, unroll=unroll)


def _final_call(dest0, dest1, h, y, gates, fg, *, tm):
    t, d = h.shape
    assert tm % SUBLANES == 0
    row = lambda i, *_: (i, 0)
    grid_spec = pltpu.PrefetchScalarGridSpec(
        num_scalar_prefetch=2,
        grid=(t // tm,),
        in_specs=[pl.BlockSpec((tm, d), row), pl.BlockSpec(memory_space=pl.ANY),
                  pl.BlockSpec((tm, LANES), row), pl.BlockSpec(fg.shape, lambda i, *_: (0, 0))],
        out_specs=pl.BlockSpec((tm, d), row),
        scratch_shapes=[pltpu.VMEM((FINAL_SLOTS, 2, tm // SUBLANES, SUBLANES, d), F32),
                        pltpu.SemaphoreType.DMA((FINAL_SLOTS,))],
    )
    return pl.pallas_call(
        _final_kernel,
        grid_spec=grid_spec,
        out_shape=jax.ShapeDtypeStruct((t, d), F32),
        compiler_params=pltpu.CompilerParams(dimension_semantics=("arbitrary",), vmem_limit_bytes=VMEM_LIMIT),
        name="final",
    )(dest0, dest1, h, y, gates, fg)


def _wsplit_kernel(lat_ref, u_ref, kr_ref, lat_out, u_out, kr_out, *, n_lat, n_kr):
    j = pl.program_id(0)
    u_out[...] = jnp.transpose(u_ref[...]).astype(BF16)

    @pl.when(j < n_lat)
    def _():
        lat_out[...] = jnp.transpose(lat_ref[...]).astype(BF16)

    @pl.when(j == 0)
    def _():
        kr = jnp.transpose(kr_ref[...])
        lane = lax.broadcasted_iota(I32, kr.shape, 1)
        kr_out[...] = jnp.where(lane < n_kr, kr, 0.0).astype(BF16)


def _wsplit_call(w_t, *, o_kr, o_u):
    cols, d = w_t.shape
    assert o_kr % LANES == 0 and (cols - o_u) % LANES == 0 and o_u % SUBLANES == 0
    blk = 2 * LANES if o_kr % (2 * LANES) == 0 and (cols - o_u) % (2 * LANES) == 0 else LANES
    n_u = (cols - o_u) // blk
    n_lat = o_kr // blk
    assert 0 < o_u - o_kr <= LANES and n_lat <= n_u and o_kr + LANES <= cols
    lat_blk = lambda j: jnp.minimum(j, n_lat - 1)
    return pl.pallas_call(
        functools.partial(_wsplit_kernel, n_lat=n_lat, n_kr=o_u - o_kr),
        grid=(n_u,),
        in_specs=[pl.BlockSpec((blk, d), lambda j: (lat_blk(j), 0)),
                  pl.BlockSpec((pl.Element(blk), pl.Element(d)),
                               lambda j: (pl.multiple_of(o_u + j * blk, SUBLANES), 0)),
                  pl.BlockSpec((pl.Element(LANES), pl.Element(d)), lambda j: (o_kr, 0))],
        out_specs=[pl.BlockSpec((d, blk), lambda j: (0, lat_blk(j))), pl.BlockSpec((d, blk), lambda j: (0, j)),
                   pl.BlockSpec((d, LANES), lambda j: (0, 0))],
        out_shape=[jax.ShapeDtypeStruct((d, o_kr), BF16), jax.ShapeDtypeStruct((d, cols - o_u), BF16),
                   jax.ShapeDtypeStruct((d, LANES), BF16)],
        compiler_params=pltpu.CompilerParams(dimension_semantics=("arbitrary",), vmem_limit_bytes=VMEM_LIMIT),
        name="wsplit",
    )(w_t, w_t, w_t)


def _rope_tables(seq):
    pos = np.arange(seq, dtype=np.float64)
    inv_freq = ROPE_THETA ** (-np.arange(0, QK_ROPE_DIM, 2, dtype=np.float64) / QK_ROPE_DIM)
    ang = pos[:, None] * inv_freq[None, :]
    cos, sin = np.cos(ang).astype(np.float32), np.sin(ang).astype(np.float32)
    zero = np.zeros_like(sin)
    cos_t = np.concatenate([cos, cos, cos, cos], axis=1)
    s1_t = np.concatenate([zero, sin, zero, sin], axis=1)
    s2_t = np.concatenate([-sin, zero, -sin, zero], axis=1)
    return jnp.asarray(cos_t), jnp.asarray(s1_t), jnp.asarray(s2_t)


def kernel(x, ln1_g, w_in, b_glu, q_norm_g, w_uq, kv_norm_g, w_ukv, w_dw, b_dw, conv_ln_g, conv_ln_b,
           w_o, ln2_g, w_group, b_group, w_router, b_router, w_gate, w_up, w_down, final_g):
    batch, seq, d = x.shape
    assert ln1_g.shape[0] == 1, "single-layer trunk"
    t = batch * seq
    q_rank = q_norm_g.shape[1]
    kv_rank = kv_norm_g.shape[1]
    x2 = x.reshape(t, d)

    wi = w_in[0]
    o_kr = q_rank + kv_rank
    o_u = o_kr + QK_ROPE_DIM
    wlat, wu, wkr = _wsplit_call(jnp.transpose(wi), o_kr=o_kr, o_u=o_u)
    wuq = w_uq[0].reshape(q_rank, MLA_HEADS, QK_NOPE_DIM + QK_ROPE_DIM)
    wuq = jnp.concatenate([wuq[:, :, :QK_NOPE_DIM].reshape(q_rank, MLA_HEADS * QK_NOPE_DIM),
                           wuq[:, :, QK_NOPE_DIM:].reshape(q_rank, MLA_HEADS * QK_ROPE_DIM)], axis=1).astype(BF16)
    wukv = w_ukv[0]
    wo = w_o[0]
    wr = jnp.concatenate([w_router[0], w_group[0],
                          jnp.zeros((d, LANES - N_EXPERTS - N_EXPERT_GROUPS), F32)], axis=1)
    wr_hi = wr.astype(BF16)
    wr_lo = (wr - wr_hi.astype(F32)).astype(BF16)
    wr2 = jnp.concatenate([wr_hi, wr_lo], axis=1)
    br = jnp.concatenate([b_router[0], b_group[0],
                          jnp.zeros((LANES - N_EXPERTS - N_EXPERT_GROUPS,), F32)])[None, :]
    cos_t, s1_t, s2_t = _rope_tables(seq)

    tm = min(512, seq)
    q, k, v, c = _proj_call(x2, ln1_g, wlat, wu, wkr, b_glu, q_norm_g, kv_norm_g, wuq, wukv, cos_t, s1_t, s2_t,
                            seq=seq, tm=tm)
    attn, conv = _attn_conv_call(q, k, v, c, w_dw[0], b_dw, batch=batch, seq=seq, tq=min(256, seq))
    h, logits = _oproj_call(attn, conv, x2, wo, conv_ln_g, conv_ln_b, ln2_g, wr2, br, tm=tm)

    n_blocks = -(-(2 * t + N_EXPERTS * (MOE_ROWS - 1)) // MOE_ROWS)
    dest0, dest1, gates, meta = _route_call(logits, rows_per_block=MOE_ROWS, n_blocks=n_blocks)
    y = _expert_call(meta[:n_blocks, 0], meta[:n_blocks, 1], meta[0:1, 2], meta[:n_blocks, 3], meta[0:1, 4],
                     dest0, dest1, h, ln2_g, w_gate[0], w_up[0], w_down[0],
                     rows_per_block=MOE_ROWS, n_blocks=n_blocks)
    out = _final_call(dest0, dest1, h, y, gates, final_g[None, :], tm=tm)
    return out.reshape(batch, seq, d)
```

```python
import functools

import jax
import jax.numpy as jnp
import numpy as np
from jax import lax
from jax.experimental import pallas as pl
from jax.experimental.pallas import tpu as pltpu

F32 = jnp.float32
BF16 = jnp.bfloat16
I32 = jnp.int32

MLA_HEADS = 8
QK_NOPE_DIM = 128
QK_ROPE_DIM = 64
V_HEAD_DIM = 128
ROPE_THETA = 10000.0
N_EXPERT_GROUPS = 8
EXPERTS_PER_GROUP = 8
N_EXPERTS = N_EXPERT_GROUPS * EXPERTS_PER_GROUP
EPS = 1e-6
LOG2E = 1.4426950408889634

LANES = 128
SUBLANES = 8
HEAD_SLOT = 2 * LANES
ROPE_HALF = QK_ROPE_DIM // 2
VMEM_LIMIT = 56 * 1024 * 1024

MOE_ROWS = 256
FINAL_SLOTS = 3
FINAL_UNROLL = 8
ROUTE_UNROLL = 4
EXPERT_ROW_SLOTS = 4
EXPERT_WEIGHT_SLOTS = 3
DMA_UNROLL = 8


def _rms(x, g):
    return x * lax.rsqrt(jnp.mean(x * x, axis=-1, keepdims=True) + EPS) * g


def _dot(a, b):
    return jnp.dot(a, b, preferred_element_type=F32)


def _whole(shape, single=False):
    mode = dict(pipeline_mode=pl.Buffered(1)) if single else {}
    return pl.BlockSpec(shape, lambda *_: (0,) * len(shape), **mode)


def _proj_kernel(x_ref, g1_ref, wlat_ref, wu_ref, wkr_ref, bglu_ref, qg_ref, kvg_ref, wuq_ref, wukv_ref,
                 cos_ref, s1_ref, s2_ref, q_out, k_out, v_out, c_out, *, q_rank, kv_rank, conv_ch, q_scale):
    xn = _rms(x_ref[...], g1_ref[...]).astype(BF16)
    cos = cos_ref[...]
    s1 = s1_ref[...]
    s2 = s2_ref[...]

    def rope(t):
        return t * cos + pltpu.roll(t, ROPE_HALF, 1) * s1 + pltpu.roll(t, LANES - ROPE_HALF, 1) * s2

    a = _dot(xn, wu_ref[:, :conv_ch]) + bglu_ref[:, :conv_ch]
    gate = _dot(xn, wu_ref[:, conv_ch:]) + bglu_ref[:, conv_ch:]
    c_out[...] = (a * jax.nn.sigmoid(gate)).astype(BF16)

    kr = rope(_dot(xn, wkr_ref[...])).astype(BF16)
    qn = _rms(_dot(xn, wlat_ref[:, :q_rank]), qg_ref[...]).astype(BF16)
    kvn = _rms(_dot(xn, wlat_ref[:, q_rank:q_rank + kv_rank]), kvg_ref[...]).astype(BF16)
    lane = lax.broadcasted_iota(I32, (x_ref.shape[0], LANES), 1)
    ones_blk = (lane == 0).astype(BF16)
    n_nope = MLA_HEADS * QK_NOPE_DIM
    q_nope = _dot(qn, wuq_ref[:, :n_nope])
    q_rope = _dot(qn, wuq_ref[:, n_nope:])
    for h in range(MLA_HEADS):
        c0 = h * HEAD_SLOT
        q_out[:, c0:c0 + LANES] = (q_nope[:, h * QK_NOPE_DIM:(h + 1) * QK_NOPE_DIM] * q_scale).astype(BF16)
        if h % 2 == 0:
            pair = rope(q_rope[:, (h // 2) * LANES:(h // 2 + 1) * LANES]) * q_scale
        half_pair = pair if h % 2 == 0 else pltpu.roll(pair, LANES - QK_ROPE_DIM, 1)
        q_out[:, c0 + LANES:c0 + HEAD_SLOT] = jnp.where(lane < QK_ROPE_DIM, half_pair, 0.0).astype(BF16)
        kvh = _dot(kvn, wukv_ref[:, c0:c0 + HEAD_SLOT].astype(BF16))
        k_out[:, c0:c0 + LANES] = kvh[:, :LANES].astype(BF16)
        k_out[:, c0 + LANES:c0 + HEAD_SLOT] = kr
        v_out[:, c0:c0 + LANES] = kvh[:, LANES:].astype(BF16)
        v_out[:, c0 + LANES:c0 + HEAD_SLOT] = ones_blk


def _proj_call(x2, g1, wlat, wu, wkr, bglu, qg, kvg, wuq, wukv, cos_t, s1_t, s2_t, *, seq, tm):
    t, d = x2.shape
    q_rank, kv_rank = qg.shape[1], kvg.shape[1]
    conv_ch = bglu.shape[1] // 2
    n_pos = seq // tm
    row = lambda i: (i, 0)
    pos = lambda i: (i % n_pos, 0)
    q_scale = float(QK_NOPE_DIM + QK_ROPE_DIM) ** -0.5 * LOG2E
    kern = functools.partial(_proj_kernel, q_rank=q_rank, kv_rank=kv_rank, conv_ch=conv_ch, q_scale=q_scale)
    slot_w = MLA_HEADS * HEAD_SLOT
    return pl.pallas_call(
        kern,
        grid=(t // tm,),
        in_specs=[
            pl.BlockSpec((tm, d), row), _whole(g1.shape), _whole(wlat.shape, True), _whole(wu.shape, True),
            _whole(wkr.shape, True), _whole(bglu.shape),
            _whole(qg.shape), _whole(kvg.shape), _whole(wuq.shape, True), _whole(wukv.shape, True),
            pl.BlockSpec((tm, LANES), pos), pl.BlockSpec((tm, LANES), pos), pl.BlockSpec((tm, LANES), pos),
        ],
        out_specs=[
            pl.BlockSpec((tm, slot_w), row), pl.BlockSpec((tm, slot_w), row),
            pl.BlockSpec((tm, slot_w), row), pl.BlockSpec((tm, conv_ch), row),
        ],
        out_shape=[
            jax.ShapeDtypeStruct((t, slot_w), BF16), jax.ShapeDtypeStruct((t, slot_w), BF16),
            jax.ShapeDtypeStruct((t, slot_w), BF16), jax.ShapeDtypeStruct((t, conv_ch), BF16),
        ],
        compiler_params=pltpu.CompilerParams(dimension_semantics=("arbitrary",), vmem_limit_bytes=VMEM_LIMIT),
        name="proj",
    )(x2, g1, wlat, wu, wkr, bglu, qg, kvg, wuq, wukv, cos_t, s1_t, s2_t)


CONV_PAD = 16
CONV_ROWS = 128


def _attn_conv_kernel(q_ref, k_ref, v_ref, c_ref, w_ref, b_ref, o_ref, y_ref, xp_ref, *, tq, width):
    s_len = q_ref.shape[0]
    half = width // 2
    rows = min(CONV_ROWS, s_len)
    win = rows + 2 * CONV_PAD
    zeros = jnp.zeros((CONV_PAD, LANES), F32)
    xp_ref[0:CONV_PAD, :] = zeros
    xp_ref[CONV_PAD + s_len:, :] = zeros
    xp_ref[CONV_PAD:CONV_PAD + s_len, :] = c_ref[...].astype(F32)

    def conv_chunk(ci):
        base = ci * rows
        xw = xp_ref[base:base + win, :]
        acc = jnp.zeros((rows, LANES), F32)
        for r in range(SUBLANES):
            shifted = xw if r == 0 else pltpu.roll(xw, win - r, 0)
            for a0 in range(0, 2 * CONV_PAD, SUBLANES):
                k = a0 + r - (CONV_PAD - half)
                if 0 <= k < width:
                    acc = acc + shifted[a0:a0 + rows, :] * w_ref[k:k + 1, :]
        y_ref[base:base + rows, :] = (acc + b_ref[...]).astype(BF16)

    k = k_ref[...]
    v = v_ref[...]
    n_q = s_len // tq
    n_chunks = s_len // rows
    for j in range(n_q):
        qs = slice(j * tq, (j + 1) * tq)
        s = lax.dot_general(q_ref[qs, :], k, (((1,), (1,)), ((), ())), preferred_element_type=F32)
        m = jnp.max(s, axis=-1, keepdims=True)
        p = jnp.exp2(s - m).astype(BF16)
        o = _dot(p, v)
        o_ref[qs, :] = (o[:, :V_HEAD_DIM] / o[:, V_HEAD_DIM:V_HEAD_DIM + 1]).astype(BF16)
        for ci in range(j * n_chunks // n_q, (j + 1) * n_chunks // n_q):
            conv_chunk(ci)


def _attn_conv_call(q, k, v, c, w_dw, b_dw, *, batch, seq, tq):
    t, ch = c.shape
    width = w_dw.shape[0]
    assert width // 2 <= CONV_PAD and ch == MLA_HEADS * LANES
    head = lambda b, h: (b, h)
    chan = lambda b, h: (0, h)
    return pl.pallas_call(
        functools.partial(_attn_conv_kernel, tq=tq, width=width),
        grid=(batch, MLA_HEADS),
        in_specs=[pl.BlockSpec((seq, HEAD_SLOT), head), pl.BlockSpec((seq, HEAD_SLOT), head),
                  pl.BlockSpec((seq, HEAD_SLOT), head), pl.BlockSpec((seq, LANES), head),
                  pl.BlockSpec((width, LANES), chan), pl.BlockSpec((1, LANES), chan)],
        out_specs=[pl.BlockSpec((seq, V_HEAD_DIM), head), pl.BlockSpec((seq, LANES), head)],
        out_shape=[jax.ShapeDtypeStruct((t, MLA_HEADS * V_HEAD_DIM), BF16), jax.ShapeDtypeStruct((t, ch), BF16)],
        scratch_shapes=[pltpu.VMEM((seq + 2 * CONV_PAD, LANES), F32)],
        compiler_params=pltpu.CompilerParams(
            dimension_semantics=("arbitrary", "arbitrary"), vmem_limit_bytes=VMEM_LIMIT),
        name="attn_conv",
    )(q, k, v, c, w_dw, b_dw)


def _oproj_kernel(a_ref, c_ref, x_ref, wo_ref, lg_ref, lb_ref, g2_ref, wr_ref, br_ref, h_out, lg_out):
    na = a_ref.shape[1]
    y = c_ref[...].astype(F32)
    yc = y - jnp.mean(y, axis=-1, keepdims=True)
    z = yc * lax.rsqrt(jnp.mean(yc * yc, axis=-1, keepdims=True) + EPS) * lg_ref[...] + lb_ref[...]
    act = (z * jax.nn.sigmoid(z)).astype(BF16)
    h = (x_ref[...] + _dot(a_ref[...], wo_ref[:na, :].astype(BF16))
         + _dot(act, wo_ref[na:, :].astype(BF16)))
    h_out[...] = h
    hn = _rms(h, g2_ref[...])
    hi = hn.astype(BF16)
    lo = (hn - hi.astype(F32)).astype(BF16)
    r = _dot(hi, wr_ref[...])
    lg_out[...] = r[:, :LANES] + r[:, LANES:] + _dot(lo, wr_ref[:, :LANES]) + br_ref[...]


def _oproj_call(attn, conv, x2, wo, ln_g, ln_b, g2, wr, br, *, tm):
    t, d = x2.shape
    row = lambda i: (i, 0)
    return pl.pallas_call(
        _oproj_kernel,
        grid=(t // tm,),
        in_specs=[pl.BlockSpec((tm, attn.shape[1]), row), pl.BlockSpec((tm, conv.shape[1]), row),
                  pl.BlockSpec((tm, d), row), _whole(wo.shape, True), _whole(ln_g.shape), _whole(ln_b.shape),
                  _whole(g2.shape), _whole(wr.shape, True), _whole(br.shape)],
        out_specs=[pl.BlockSpec((tm, d), row), pl.BlockSpec((tm, LANES), row)],
        out_shape=[jax.ShapeDtypeStruct((t, d), F32), jax.ShapeDtypeStruct((t, LANES), F32)],
        compiler_params=pltpu.CompilerParams(dimension_semantics=("arbitrary",), vmem_limit_bytes=VMEM_LIMIT),
        name="oproj",
    )(attn, conv, x2, wo, ln_g, ln_b, g2, wr, br)


def _route_tokens_on_lanes(lt):
    shape = (SUBLANES, LANES)
    row = lax.broadcasted_iota(I32, shape, 0)
    big = jnp.int32(1 << 20)
    neg = jnp.float32(-jnp.inf)

    def top(v):
        m = jnp.max(v, axis=0, keepdims=True)
        return m, jnp.min(jnp.where(v == m, row, big), axis=0, keepdims=True)

    lgrp = lt[N_EXPERTS:N_EXPERTS + N_EXPERT_GROUPS, :]
    gmax, gsel = top(lgrp)
    p_g = 1.0 / jnp.sum(jnp.exp(lgrp - gmax), axis=0, keepdims=True)
    le = jnp.zeros(shape, F32)
    for g in range(N_EXPERT_GROUPS):
        le = jnp.where(gsel == g, lt[g * EXPERTS_PER_GROUP:(g + 1) * EXPERTS_PER_GROUP, :], le)
    m1, i1 = top(le)
    m2, i2 = top(jnp.where(row == i1, neg, le))
    r = jnp.exp(m2 - m1)
    w1 = 1.0 / (1.0 + r)
    w2 = r / (1.0 + r)
    base = gsel * EXPERTS_PER_GROUP
    return base + i1, base + i2, p_g * w1, p_g * w2


def _lane_cumsum(v):
    lane = lax.broadcasted_iota(I32, v.shape, 1)
    sh = 1
    while sh < LANES:
        v = v + jnp.where(lane >= sh, pltpu.roll(v, sh, 1), 0)
        sh *= 2
    return v


def _route_kernel(lg_ref, d0_out, d1_out, gate_out, meta_out, e_ref, *, rows_per_block):
    n_chunks = d0_out.shape[0]
    shift = rows_per_block.bit_length() - 1
    sub_shift = SUBLANES.bit_length() - 1
    sq = (LANES, LANES)
    row = lax.broadcasted_iota(I32, sq, 0)
    row8 = lax.broadcasted_iota(I32, (SUBLANES, LANES), 0)

    def one_hots(e1, e2):
        return (row == e1).astype(F32), (row == e2).astype(F32)

    def count_step(i, cnt):
        base = pl.multiple_of(i * LANES, LANES)
        e1, e2, g1, g2 = _route_tokens_on_lanes(jnp.transpose(lg_ref[pl.ds(base, LANES), :]))
        e_ref[i] = jnp.where(row8 == 0, e1, jnp.where(row8 == 1, e2, 0))
        gate_out[pl.ds(base, LANES), :] = jnp.transpose(jnp.where(row == 0, g1, jnp.where(row == 1, g2, 0.0)))
        oh1, oh2 = one_hots(e1, e2)
        return cnt + jnp.sum(oh1 + oh2, axis=1, keepdims=True)

    unroll = ROUTE_UNROLL if n_chunks % ROUTE_UNROLL == 0 else 1
    counts_col = lax.fori_loop(0, n_chunks, count_step, jnp.zeros((LANES, 1), F32), unroll=unroll)
    counts = jnp.transpose(jnp.broadcast_to(counts_col, sq))[0:SUBLANES, :].astype(I32)
    padded = ((counts + (SUBLANES - 1)) >> sub_shift) << sub_shift
    pad_end = _lane_cumsum(padded)
    pad_start = pad_end - padded
    start_col = jnp.transpose(jnp.broadcast_to(pad_start[0:1, :].astype(F32), sq))[:, 0:1]

    tri = (row < lax.broadcasted_iota(I32, sq, 1)).astype(BF16)

    def dest_step(i, carry):
        er = e_ref[i]
        oh1, oh2 = one_hots(er[0:1, :], er[1:2, :])
        oh = oh1 + oh2
        pos = carry + _dot(oh.astype(BF16), tri)
        d0_out[i] = jnp.sum(oh1 * pos, axis=0, keepdims=True).astype(I32)
        d1_out[i] = jnp.sum(oh2 * pos, axis=0, keepdims=True).astype(I32)
        return carry + jnp.sum(oh, axis=1, keepdims=True)

    lax.fori_loop(0, n_chunks, dest_step, start_col, unroll=unroll)

    nbp = meta_out.shape[0]
    lane_b = lax.broadcasted_iota(I32, (nbp, LANES), 1)
    blk = lax.broadcasted_iota(I32, (nbp, LANES), 0)
    nblk = (counts + (rows_per_block - 1)) >> shift
    blk_end = _lane_cumsum(nblk)
    bcast = lambda v: jnp.broadcast_to(v[0:1, :], (nbp, LANES))
    be_end, be_start, cn, ps = bcast(blk_end), bcast(blk_end - nblk), bcast(counts), bcast(pad_start)
    is_e = lane_b < N_EXPERTS
    lsum = lambda v: jnp.sum(v, axis=-1, keepdims=True)
    n_used = jnp.max(be_end, axis=-1, keepdims=True)
    total = jnp.max(bcast(pad_end), axis=-1, keepdims=True)
    last_e = jnp.max(jnp.where(is_e & (cn > 0), lane_b, 0), axis=-1, keepdims=True)
    be = jnp.minimum(lsum(jnp.where(is_e & (be_end <= blk), 1, 0)), last_e)
    sel = lane_b == be
    first_row = (blk[:, 0:1] - lsum(jnp.where(sel, be_start, 0))) * rows_per_block
    used = blk[:, 0:1] < n_used
    nvalid = jnp.where(used, jnp.clip(lsum(jnp.where(sel, cn, 0)) - first_row, 0, rows_per_block), 0)
    row0 = jnp.where(used, lsum(jnp.where(sel, ps, 0)) + first_row, 0)
    cols = (be, nvalid, n_used, row0, total)
    meta = jnp.zeros((nbp, LANES), I32)
    for c, v in enumerate(cols):
        meta = jnp.where(lane_b == c, v, meta)
    meta_out[...] = meta


def _route_call(logits, *, rows_per_block, n_blocks):
    t = logits.shape[0]
    assert t % LANES == 0
    nbp = -(-n_blocks // SUBLANES) * SUBLANES
    dshape = (t // LANES, 1, LANES)
    d0, d1, gates, meta = pl.pallas_call(
        functools.partial(_route_kernel, rows_per_block=rows_per_block),
        in_specs=[_whole(logits.shape)],
        out_specs=[_whole(dshape), _whole(dshape), _whole((t, LANES)), _whole((nbp, LANES))],
        out_shape=[jax.ShapeDtypeStruct(dshape, I32), jax.ShapeDtypeStruct(dshape, I32),
                   jax.ShapeDtypeStruct((t, LANES), F32), jax.ShapeDtypeStruct((nbp, LANES), I32)],
        grid=(1,),
        scratch_shapes=[pltpu.VMEM((t // LANES, SUBLANES, LANES), I32)],
        compiler_params=pltpu.CompilerParams(dimension_semantics=("arbitrary",), vmem_limit_bytes=VMEM_LIMIT),
        name="route",
    )(logits)
    return d0.reshape(t), d1.reshape(t), gates, meta


def _pow2_chunks(limit):
    c = 1 << (limit.bit_length() - 1)
    while c >= 1:
        yield c
        c >>= 1


def _expert_kernel(be_ref, nv_ref, nused_ref, row0_ref, total_ref, d0_ref, d1_ref,
                   h_hbm, g2_ref, wg_hbm, wu_hbm, wd_hbm, y_hbm,
                   tok_ref, ord_ref, exp_ref, nexp_ref, xbuf, ybuf, zbuf, wg_buf, wu_buf, wd_buf,
                   gsem, ysem, zsem, wsem):
    n_used = nused_ref[0]
    groups = xbuf.shape[1]
    rows = groups * SUBLANES
    d = xbuf.shape[3]
    n_tok = d0_ref.shape[0]
    sub_shift = SUBLANES.bit_length() - 1
    n_slots = wg_buf.shape[0]

    def y_copies(blk, slot):
        ng = lax.shift_right_logical(nv_ref[blk] + (SUBLANES - 1), sub_shift)
        g0 = lax.shift_right_logical(row0_ref[blk], sub_shift)
        out = []
        for c in _pow2_chunks(groups):
            off = ng & ~(2 * c - 1)
            copy = pltpu.make_async_copy(ybuf.at[slot, pl.ds(off, c)], y_hbm.at[pl.ds(g0 + off, c)], ysem.at[slot])
            out.append(((ng & c) != 0, copy))
        return out

    def start_y(blk, slot):
        for cond, copy in y_copies(blk, slot):
            pl.when(cond)(copy.start)

    def wait_y(blk, slot):
        for cond, copy in y_copies(blk, slot):
            pl.when(cond)(copy.wait)

    def fill_tail():
        zbuf[...] = jnp.zeros(zbuf.shape, F32)
        first = lax.shift_right_logical(total_ref[0], sub_shift)
        copy = lambda g: pltpu.make_async_copy(zbuf, y_hbm.at[g], zsem.at[0])
        lax.fori_loop(first, y_hbm.shape[0], lambda g, c: (copy(g).start(), c)[1], 0)
        lax.fori_loop(first, y_hbm.shape[0], lambda g, c: (copy(g).wait(), c)[1], 0)

    def weight_copies(j):
        e = exp_ref[j]
        slot = lax.rem(j, n_slots)
        return [pltpu.make_async_copy(src.at[e], dst.at[slot], wsem.at[slot])
                for src, dst in ((wg_hbm, wg_buf), (wu_hbm, wu_buf), (wd_hbm, wd_buf))]

    def start_weights(j):
        for c in weight_copies(j):
            c.start(priority=1)

    def wait_weights(j):
        for c in weight_copies(j):
            c.wait()

    def gather_copy(blk, slot, g, u):
        tok = tok_ref[row0_ref[blk] + g * SUBLANES + u]
        return pltpu.make_async_copy(h_hbm.at[pl.ds(tok, 1)], xbuf.at[slot, g, pl.ds(u, 1)], gsem.at[slot])

    def start_gather(blk, slot):
        n = nv_ref[blk]

        def group(g, c):
            for u in range(SUBLANES):
                gather_copy(blk, slot, g, u).start()
            return c
        full = lax.shift_right_logical(n, sub_shift)
        lax.fori_loop(0, full, group, 0)
        for u in range(SUBLANES - 1):
            @pl.when(full * SUBLANES + u < n)
            def _():
                gather_copy(blk, slot, full, u).start()

    def wait_gather(blk, slot):
        n = nv_ref[blk]
        buf = xbuf.at[slot]
        for c in _pow2_chunks(rows):
            @pl.when((n & c) != 0)
            def _():
                if c >= SUBLANES:
                    part = buf.at[pl.ds(0, c // SUBLANES)]
                else:
                    part = buf.at[0, pl.ds(0, c)]
                pltpu.make_async_copy(part, part, gsem.at[slot]).wait()

    def scan(blk, j):
        e = be_ref[blk]
        is_new = jnp.logical_or(blk == 0, e != be_ref[jnp.maximum(blk - 1, 0)])
        j = j + is_new.astype(I32)
        ord_ref[blk] = j - 1

        @pl.when(is_new)
        def _():
            exp_ref[j - 1] = e
        return j
    n_exp = lax.fori_loop(0, n_used, scan, jnp.int32(0))
    nexp_ref[0] = n_exp
    for j in range(n_slots):
        @pl.when(j < n_exp)
        def _():
            start_weights(j)

    def inv(g, c):
        for u in range(DMA_UNROLL):
            tk = g * DMA_UNROLL + u
            tok_ref[d0_ref[tk]] = tk
            tok_ref[d1_ref[tk]] = tk
        return c
    lax.fori_loop(0, n_tok // DMA_UNROLL, inv, 0)
    xbuf[...] = jnp.zeros(xbuf.shape, F32)
    x_slots = xbuf.shape[0]
    for blk in range(x_slots - 1):
        @pl.when(blk < n_used)
        def _():
            start_gather(blk, blk)

    def block(b, carry):
        slot = b & 1
        xs = lax.rem(b, x_slots)
        wait_gather(b, xs)
        ahead = b + (x_slots - 1)

        @pl.when(ahead < n_used)
        def _():
            start_gather(ahead, lax.rem(ahead, x_slots))

        j = ord_ref[b]

        @pl.when(jnp.logical_or(b == 0, ord_ref[jnp.maximum(b - 1, 0)] != j))
        def _():
            wait_weights(j)

            @pl.when(jnp.logical_and(j >= 1, j + (n_slots - 1) < nexp_ref[0]))
            def _():
                start_weights(j + (n_slots - 1))

        ws = lax.rem(j, n_slots)
        hn = _rms(xbuf[xs].reshape(rows, d), g2_ref[...]).astype(BF16)
        gate = _dot(hn, wg_buf[ws].astype(BF16))
        up = _dot(hn, wu_buf[ws].astype(BF16))
        hmid = (gate * jax.nn.sigmoid(gate) * up).astype(BF16)
        y = _dot(hmid, wd_buf[ws].astype(BF16))

        @pl.when(b >= 2)
        def _():
            wait_y(b - 2, slot)

        ybuf[slot] = y.reshape(groups, SUBLANES, d)
        start_y(b, slot)
        return carry

    lax.fori_loop(0, n_used, block, 0)
    last = n_used - 1

    @pl.when(last >= 1)
    def _():
        wait_y(last - 1, 1 - (last & 1))
    wait_y(last, last & 1)
    fill_tail()


def _expert_call(be, nvalid, n_used, row0, total, dest0, dest1, h, g2, w_gate, w_up, w_down, *,
                 rows_per_block, n_blocks):
    t, d = h.shape
    n_exp, _, f = w_gate.shape
    assert t % DMA_UNROLL == 0 and rows_per_block % SUBLANES == 0 and (2 * t) % SUBLANES == 0
    groups = rows_per_block // SUBLANES
    sorted_groups = 2 * t // SUBLANES + n_exp
    hbm = pl.BlockSpec(memory_space=pl.ANY)
    slots = EXPERT_WEIGHT_SLOTS
    grid_spec = pltpu.PrefetchScalarGridSpec(
        num_scalar_prefetch=7,
        grid=(1,),
        in_specs=[hbm, pl.BlockSpec(g2.shape, lambda b, *_: (0, 0)), hbm, hbm, hbm],
        out_specs=hbm,
        scratch_shapes=[
            pltpu.SMEM((sorted_groups * SUBLANES,), I32),
            pltpu.SMEM((n_blocks,), I32), pltpu.SMEM((n_exp,), I32), pltpu.SMEM((1,), I32),
            pltpu.VMEM((EXPERT_ROW_SLOTS, groups, SUBLANES, d), F32), pltpu.VMEM((2, groups, SUBLANES, d), F32),
            pltpu.VMEM((SUBLANES, d), F32),
            pltpu.VMEM((slots, d, f), F32), pltpu.VMEM((slots, d, f), F32), pltpu.VMEM((slots, f, d), F32),
            pltpu.SemaphoreType.DMA((EXPERT_ROW_SLOTS,)), pltpu.SemaphoreType.DMA((2,)), pltpu.SemaphoreType.DMA((1,)),
            pltpu.SemaphoreType.DMA((slots,)),
        ],
    )
    y = pl.pallas_call(
        _expert_kernel,
        grid_spec=grid_spec,
        out_shape=jax.ShapeDtypeStruct((sorted_groups, SUBLANES, d), F32),
        compiler_params=pltpu.CompilerParams(dimension_semantics=("arbitrary",), vmem_limit_bytes=VMEM_LIMIT),
        name="experts",
    )(be, nvalid, n_used, row0, total, dest0, dest1, h, g2, w_gate, w_up, w_down)
    return y.reshape(sorted_groups * SUBLANES, d)


def _final_kernel(d0_ref, d1_ref, h_ref, y_hbm, gate_ref, fg_ref, o_ref, ybuf, sem):
    i = pl.program_id(0)
    n = pl.num_programs(0)
    n_slots = ybuf.shape[0]
    groups = ybuf.shape[2]
    tm = groups * SUBLANES

    def issue_group(tile, slot, g):
        for u in range(SUBLANES):
            tk = tile * tm + g * SUBLANES + u
            for k, dref in enumerate((d0_ref, d1_ref)):
                pltpu.make_async_copy(y_hbm.at[pl.ds(dref[tk], 1)], ybuf.at[slot, k, g, pl.ds(u, 1)],
                                      sem.at[slot]).start()

    def combine_group(slot, g, issue):
        r0 = pl.multiple_of(g * SUBLANES, SUBLANES)
        gt = gate_ref[pl.ds(r0, SUBLANES), :]
        out = h_ref[pl.ds(r0, SUBLANES), :] + gt[:, 0:1] * ybuf[slot, 0, g] + gt[:, 1:2] * ybuf[slot, 1, g]
        issue()
        o_ref[pl.ds(r0, SUBLANES), :] = _rms(out, fg_ref[...])

    @pl.when(i == 0)
    def _():
        for tile in range(n_slots - 1):
            @pl.when(tile < n)
            def _():
                lax.fori_loop(0, groups, lambda g, c: (issue_group(tile, tile, g), c)[1], 0)

    slot = lax.rem(i, n_slots)
    pltpu.make_async_copy(ybuf.at[slot], ybuf.at[slot], sem.at[slot]).wait()
    ahead = i + (n_slots - 1)
    unroll = FINAL_UNROLL if groups % FINAL_UNROLL == 0 else 1

    @pl.when(ahead < n)
    def _():
        nslot = lax.rem(ahead, n_slots)
        lax.fori_loop(0, groups, lambda g, c: (combine_group(
            slot, g, lambda: issue_group(ahead, nslot, g)), c)[1], 0, unroll=unroll)

    @pl.when(ahead >= n)
    def _():
        lax.fori_loop(0, groups, lambda g, c: (combine_group(slot, g, lambda: None), c)[1], 0, unroll=unroll)


def _final_call(dest0, dest1, h, y, gates, fg, *, tm):
    t, d = h.shape
    assert tm % SUBLANES == 0
    row = lambda i, *_: (i, 0)
    grid_spec = pltpu.PrefetchScalarGridSpec(
        num_scalar_prefetch=2,
        grid=(t // tm,),
        in_specs=[pl.BlockSpec((tm, d), row), pl.BlockSpec(memory_space=pl.ANY),
                  pl.BlockSpec((tm, LANES), row), pl.BlockSpec(fg.shape, lambda i, *_: (0, 0))],
        out_specs=pl.BlockSpec((tm, d), row),
        scratch_shapes=[pltpu.VMEM((FINAL_SLOTS, 2, tm // SUBLANES, SUBLANES, d), F32),
                        pltpu.SemaphoreType.DMA((FINAL_SLOTS,))],
    )
    return pl.pallas_call(
        _final_kernel,
        grid_spec=grid_spec,
        out_shape=jax.ShapeDtypeStruct((t, d), F32),
        compiler_params=pltpu.CompilerParams(dimension_semantics=("arbitrary",), vmem_limit_bytes=VMEM_LIMIT),
        name="final",
    )(dest0, dest1, h, y, gates, fg)


def _wsplit_kernel(lat_ref, u_ref, kr_ref, lat_out, u_out, kr_out, *, n_lat, n_kr):
    j = pl.program_id(0)
    u_out[...] = jnp.transpose(u_ref[...]).astype(BF16)

    @pl.when(j < n_lat)
    def _():
        lat_out[...] = jnp.transpose(lat_ref[...]).astype(BF16)

    @pl.when(j == 0)
    def _():
        kr = jnp.transpose(kr_ref[...])
        lane = lax.broadcasted_iota(I32, kr.shape, 1)
        kr_out[...] = jnp.where(lane < n_kr, kr, 0.0).astype(BF16)


def _wsplit_call(w_t, *, o_kr, o_u):
    cols, d = w_t.shape
    assert o_kr % LANES == 0 and (cols - o_u) % LANES == 0 and o_u % SUBLANES == 0
    blk = 2 * LANES if o_kr % (2 * LANES) == 0 and (cols - o_u) % (2 * LANES) == 0 else LANES
    n_u = (cols - o_u) // blk
    n_lat = o_kr // blk
    assert 0 < o_u - o_kr <= LANES and n_lat <= n_u and o_kr + LANES <= cols
    lat_blk = lambda j: jnp.minimum(j, n_lat - 1)
    return pl.pallas_call(
        functools.partial(_wsplit_kernel, n_lat=n_lat, n_kr=o_u - o_kr),
        grid=(n_u,),
        in_specs=[pl.BlockSpec((blk, d), lambda j: (lat_blk(j), 0)),
                  pl.BlockSpec((pl.Element(blk), pl.Element(d)),
                               lambda j: (pl.multiple_of(o_u + j * blk, SUBLANES), 0)),
                  pl.BlockSpec((pl.Element(LANES), pl.Element(d)), lambda j: (o_kr, 0))],
        out_specs=[pl.BlockSpec((d, blk), lambda j: (0, lat_blk(j))), pl.BlockSpec((d, blk), lambda j: (0, j)),
                   pl.BlockSpec((d, LANES), lambda j: (0, 0))],
        out_shape=[jax.ShapeDtypeStruct((d, o_kr), BF16), jax.ShapeDtypeStruct((d, cols - o_u), BF16),
                   jax.ShapeDtypeStruct((d, LANES), BF16)],
        compiler_params=pltpu.CompilerParams(dimension_semantics=("arbitrary",), vmem_limit_bytes=VMEM_LIMIT),
        name="wsplit",
    )(w_t, w_t, w_t)


def _rope_tables(seq):
    pos = np.arange(seq, dtype=np.float64)
    inv_freq = ROPE_THETA ** (-np.arange(0, QK_ROPE_DIM, 2, dtype=np.float64) / QK_ROPE_DIM)
    ang = pos[:, None] * inv_freq[None, :]
    cos, sin = np.cos(ang).astype(np.float32), np.sin(ang).astype(np.float32)
    zero = np.zeros_like(sin)
    cos_t = np.concatenate([cos, cos, cos, cos], axis=1)
    s1_t = np.concatenate([zero, sin, zero, sin], axis=1)
    s2_t = np.concatenate([-sin, zero, -sin, zero], axis=1)
    return jnp.asarray(cos_t), jnp.asarray(s1_t), jnp.asarray(s2_t)


def kernel(x, ln1_g, w_in, b_glu, q_norm_g, w_uq, kv_norm_g, w_ukv, w_dw, b_dw, conv_ln_g, conv_ln_b,
           w_o, ln2_g, w_group, b_group, w_router, b_router, w_gate, w_up, w_down, final_g):
    batch, seq, d = x.shape
    assert ln1_g.shape[0] == 1, "single-layer trunk"
    t = batch * seq
    q_rank = q_norm_g.shape[1]
    kv_rank = kv_norm_g.shape[1]
    x2 = x.reshape(t, d)

    wi = w_in[0]
    o_kr = q_rank + kv_rank
    o_u = o_kr + QK_ROPE_DIM
    wlat, wu, wkr = _wsplit_call(jnp.transpose(wi), o_kr=o_kr, o_u=o_u)
    wuq = w_uq[0].reshape(q_rank, MLA_HEADS, QK_NOPE_DIM + QK_ROPE_DIM)
    wuq = jnp.concatenate([wuq[:, :, :QK_NOPE_DIM].reshape(q_rank, MLA_HEADS * QK_NOPE_DIM),
                           wuq[:, :, QK_NOPE_DIM:].reshape(q_rank, MLA_HEADS * QK_ROPE_DIM)], axis=1).astype(BF16)
    wukv = w_ukv[0]
    wo = w_o[0]
    wr = jnp.concatenate([w_router[0], w_group[0],
                          jnp.zeros((d, LANES - N_EXPERTS - N_EXPERT_GROUPS), F32)], axis=1)
    wr_hi = wr.astype(BF16)
    wr_lo = (wr - wr_hi.astype(F32)).astype(BF16)
    wr2 = jnp.concatenate([wr_hi, wr_lo], axis=1)
    br = jnp.concatenate([b_router[0], b_group[0],
                          jnp.zeros((LANES - N_EXPERTS - N_EXPERT_GROUPS,), F32)])[None, :]
    cos_t, s1_t, s2_t = _rope_tables(seq)

    tm = min(512, seq)
    q, k, v, c = _proj_call(x2, ln1_g, wlat, wu, wkr, b_glu, q_norm_g, kv_norm_g, wuq, wukv, cos_t, s1_t, s2_t,
                            seq=seq, tm=tm)
    attn, conv = _attn_conv_call(q, k, v, c, w_dw[0], b_dw, batch=batch, seq=seq, tq=min(256, seq))
    h, logits = _oproj_call(attn, conv, x2, wo, conv_ln_g, conv_ln_b, ln2_g, wr2, br, tm=tm)

    n_blocks = -(-(2 * t + N_EXPERTS * (MOE_ROWS - 1)) // MOE_ROWS)
    dest0, dest1, gates, meta = _route_call(logits, rows_per_block=MOE_ROWS, n_blocks=n_blocks)
    y = _expert_call(meta[:n_blocks, 0], meta[:n_blocks, 1], meta[0:1, 2], meta[:n_blocks, 3], meta[0:1, 4],
                     dest0, dest1, h, ln2_g, w_gate[0], w_up[0], w_down[0],
                     rows_per_block=MOE_ROWS, n_blocks=n_blocks)
    out = _final_call(dest0, dest1, h, y, gates, final_g[None, :], tm=tm)
    return out.reshape(batch, seq, d)
```
